```python
import math
import jax, jax.numpy as jnp
from jax import lax
import numpy as np

D_MODEL = 1024
BATCH = 4
SEQ = 4096
DEPTH = 2

PLE_DIM = 256
D_FF = 2816
N_EVEN = (DEPTH + 1) // 2
N_ODD = DEPTH // 2
DN_ALPHA = (2.0 * DEPTH) ** 0.25
DN_BETA = (8.0 * DEPTH) ** -0.25
LN_EPS = 1e-5
SSD_INNER = D_MODEL
SSD_HEAD_DIM = 64
SSD_HEADS = SSD_INNER // SSD_HEAD_DIM
SSD_GROUPS = 4
SSD_STATE = 128
SSD_CONV = 4
SSD_CHUNK = 128
SSD_CONV_DIM = SSD_INNER + 2 * SSD_GROUPS * SSD_STATE
S5_WIDTH = D_MODEL
S5_GROUP = 16
S5_GROUPS = S5_WIDTH // S5_GROUP
S5_STATE = 64
HGRN_HEADS = 4
HGRN_KEY = 128
HGRN_VAL = 128
HGRN_WIDTH = HGRN_HEADS * HGRN_VAL
GLA_HEADS = 4
GLA_DK = 64
GLA_DV = 128
GLA_RANK = 16
GLA_TAU = 16.0
GLA_WIDTH = GLA_HEADS * GLA_DV
LIN_CHUNK = 64
AB_SPLITS = (SSD_INNER, SSD_CONV_DIM, SSD_HEADS, S5_WIDTH)
AB_IN = sum(AB_SPLITS)
AB_OUT = SSD_INNER + S5_WIDTH
CD_SPLITS = (HGRN_HEADS * HGRN_KEY, HGRN_HEADS * HGRN_KEY, HGRN_WIDTH, HGRN_WIDTH,
             GLA_HEADS * GLA_DK, GLA_HEADS * GLA_DK, GLA_WIDTH, GLA_RANK, GLA_WIDTH)
CD_IN = sum(CD_SPLITS)
CD_OUT = HGRN_WIDTH + GLA_WIDTH

kernel_name = "hybrid_ssd_s5_hgrn2_gla_macaron_deepnorm"


def split_cols(h, sizes):
    return jnp.split(h, np.cumsum(sizes)[:-1].tolist(), axis=-1)


def layer_norm(x, g, b):
    xf = x.astype(jnp.float32)
    mu = jnp.mean(xf, -1, keepdims=True)
    xc = xf - mu
    var = jnp.mean(xc * xc, -1, keepdims=True)
    return (xc * lax.rsqrt(var + LN_EPS) * g + b).astype(x.dtype)


def rms_norm(x, w):
    xf = x.astype(jnp.float32)
    return (xf * lax.rsqrt(jnp.mean(xf * xf, -1, keepdims=True) + LN_EPS) * w).astype(x.dtype)


def swiglu(x, w_gate, w_up, w_down):
    return (jax.nn.silu(x @ w_gate) * (x @ w_up)) @ w_down


def causal_depthwise_conv(x, w, b):
    k = w.shape[0]
    y = lax.conv_general_dilated(x, w[:, None, :], window_strides=(1,), padding=[(k - 1, 0)],
                                 dimension_numbers=('NWC', 'WIO', 'NWC'),
                                 feature_group_count=x.shape[-1])
    return y + b


def ssd_chunked(x, dt, a, bmat, cmat):
    b, l, h, p = x.shape
    g, n = bmat.shape[2:]
    r = h // g
    nc = l // SSD_CHUNK
    xdt = (x * dt[..., None]).reshape(b, nc, SSD_CHUNK, g, r, p)
    a_cum = jnp.cumsum((dt * a).reshape(b, nc, SSD_CHUNK, g, r), axis=2)
    bc = bmat.reshape(b, nc, SSD_CHUNK, g, n)
    cc = cmat.reshape(b, nc, SSD_CHUNK, g, n)
    causal = jnp.tril(jnp.ones((SSD_CHUNK, SSD_CHUNK), bool))[None, None, :, :, None, None]
    seg = a_cum[:, :, :, None] - a_cum[:, :, None, :]
    decay = jnp.exp(jnp.where(causal, seg, -jnp.inf))
    scores = jnp.einsum('bclgn,bcsgn->bclsg', cc, bc)
    y_diag = jnp.einsum('bclsgr,bcsgrp->bclgrp', scores[..., None] * decay, xdt)
    decay_to_end = jnp.exp(a_cum[:, :, -1:] - a_cum)
    states = jnp.einsum('bclgn,bclgrp->bcgrpn', bc, xdt * decay_to_end[..., None])
    chunk_decay = jnp.exp(a_cum[:, :, -1])

    def step(hstate, inp):
        dec, st = inp
        return dec[..., None, None] * hstate + st, hstate

    h0 = jnp.zeros((b, g, r, p, n), jnp.float32)
    _, prev = lax.scan(step, h0, (jnp.moveaxis(chunk_decay, 1, 0), jnp.moveaxis(states, 1, 0)))
    prev = jnp.moveaxis(prev, 0, 1)
    y_off = jnp.einsum('bclgn,bcgrpn->bclgrp', cc, prev) * jnp.exp(a_cum)[..., None]
    return (y_diag + y_off).reshape(b, l, h, p)


def ssd_mixer(z, xbc, dt_raw, conv_w, conv_b, dt_bias, a_log, d_skip, norm_w):
    b, l, _ = z.shape
    xbc = jax.nn.silu(causal_depthwise_conv(xbc, conv_w, conv_b))
    xs, bm, cm = jnp.split(xbc, [SSD_INNER, SSD_INNER + SSD_GROUPS * SSD_STATE], axis=-1)
    xs = xs.reshape(b, l, SSD_HEADS, SSD_HEAD_DIM)
    bm = bm.reshape(b, l, SSD_GROUPS, SSD_STATE)
    cm = cm.reshape(b, l, SSD_GROUPS, SSD_STATE)
    dt = jax.nn.softplus((dt_raw + dt_bias).astype(jnp.float32))
    a = -jnp.exp(a_log.astype(jnp.float32))
    y = ssd_chunked(xs, dt, a, bm, cm) + xs * d_skip[:, None]
    y = y.reshape(b, l, SSD_INNER) * jax.nn.silu(z)
    y = rms_norm(y.reshape(b, l, SSD_GROUPS, -1), norm_w.reshape(SSD_GROUPS, -1))
    return y.reshape(b, l, SSD_INNER).astype(z.dtype)


def linear_recurrence_combine(e1, e2):
    a1, b1 = e1
    a2, b2 = e2
    return a1 * a2, a2 * b1 + b2


def s5_mixer(u, lam_re, lam_im, log_dt, b_re, b_im, c_re, c_im, d_skip, w_glu, b_glu):
    b, l, _ = u.shape
    f32 = jnp.float32
    ug = u.reshape(b, l, S5_GROUPS, S5_GROUP).astype(f32)
    lam = lax.complex(lam_re.astype(f32), lam_im.astype(f32))
    dt = jnp.exp(log_dt.astype(f32))[:, None]
    lam_bar = jnp.exp(lam * dt)
    b_bar = ((lam_bar - 1.0) / lam)[..., None] * lax.complex(b_re.astype(f32), b_im.astype(f32))
    bu = jnp.einsum('blgp,gnp->blgn', ug.astype(jnp.complex64), b_bar)
    a = jnp.broadcast_to(lam_bar, (1, l) + lam_bar.shape)
    _, states = lax.associative_scan(linear_recurrence_combine, (a, bu), axis=1)
    c = lax.complex(c_re.astype(f32), c_im.astype(f32))
    y = jnp.real(jnp.einsum('blgn,gpn->blgp', states, c)) + d_skip.astype(f32) * ug
    y = jax.nn.gelu(y.reshape(b, l, S5_WIDTH))
    y = y * jax.nn.sigmoid(y @ w_glu.astype(f32) + b_glu.astype(f32))
    return y.astype(u.dtype)


def gla_chunked(q, k, v, log_f):
    b, l, h, dk = q.shape
    dv = v.shape[-1]
    nc = l // LIN_CHUNK

    def to_chunks(t):
        return jnp.moveaxis(t.reshape(b, nc, LIN_CHUNK, *t.shape[2:]), 1, 0)

    qc, kc, vc, gc = to_chunks(q), to_chunks(k), to_chunks(v), to_chunks(log_f.astype(jnp.float32))
    causal = jnp.tril(jnp.ones((LIN_CHUNK, LIN_CHUNK), bool))[None, :, :, None, None]

    def step(s, inp):
        q_, k_, v_, g_ = inp
        gcum = jnp.cumsum(g_, axis=1)
        diff = gcum[:, :, None] - gcum[:, None, :]
        decay = jnp.exp(jnp.where(causal, diff, -jnp.inf))
        attn = jnp.einsum('blshd,bshd->bhls', decay * q_[:, :, None], k_)
        o = (jnp.einsum('bhls,bshv->blhv', attn, v_)
             + jnp.einsum('blhd,bhdv->blhv', q_ * jnp.exp(gcum), s))
        g_last = gcum[:, -1]
        s = (jnp.exp(g_last)[..., None] * s
             + jnp.einsum('bshd,bshv->bhdv', k_ * jnp.exp(g_last[:, None] - gcum), v_))
        return s, o

    s0 = jnp.zeros((b, h, dk, dv), jnp.float32)
    _, o = lax.scan(step, s0, (qc, kc, vc, gc))
    return jnp.moveaxis(o, 0, 1).reshape(b, l, h, dv).astype(v.dtype)


def hgrn_lower_bound(lb_logits, layer):
    cum = jnp.cumsum(jax.nn.softmax(lb_logits.astype(jnp.float32), axis=0), axis=0)
    return cum[layer] - cum[0]


def mixer_ab(x, w_in, w_out, conv_w, conv_b, dt_bias, a_log, d_skip, norm_w,
             lam_re, lam_im, log_dt, b_re, b_im, c_re, c_im, s5_d, w_glu, b_glu):
    z, xbc, dt_raw, u = split_cols(x @ w_in, AB_SPLITS)
    y_a = ssd_mixer(z, xbc, dt_raw, conv_w, conv_b, dt_bias, a_log, d_skip, norm_w)
    y_b = s5_mixer(u, lam_re, lam_im, log_dt, b_re, b_im, c_re, c_im, s5_d, w_glu, b_glu)
    return jnp.concatenate([y_a, y_b], axis=-1) @ w_out


def mixer_cd(x, w_in, w_out, lb, hgrn_norm_w, gla_w_gate_up, gla_b_gate, gla_norm_w):
    b, l, _ = x.shape
    hq, hf, hi, hg, gq, gk, gv, glr, gr = split_cols(x @ w_in, CD_SPLITS)
    q = jax.nn.silu(hq).reshape(b, l, HGRN_HEADS, HGRN_KEY)
    f_logit = hf.astype(jnp.float32).reshape(b, l, HGRN_HEADS, HGRN_KEY)
    lb = lb.reshape(HGRN_HEADS, HGRN_KEY)
    log_f = jnp.logaddexp(jnp.log(lb), jnp.log1p(-lb) + jax.nn.log_sigmoid(f_logit))
    k = (1.0 - lb) * jax.nn.sigmoid(-f_logit)
    v = hi.reshape(b, l, HGRN_HEADS, HGRN_VAL)
    o_c = gla_chunked(q, k, v, log_f)
    o_c = rms_norm(o_c, hgrn_norm_w.reshape(HGRN_HEADS, HGRN_VAL)).reshape(b, l, HGRN_WIDTH) * jax.nn.silu(hg)
    q = gq.reshape(b, l, GLA_HEADS, GLA_DK) * (GLA_DK ** -0.5)
    k = gk.reshape(b, l, GLA_HEADS, GLA_DK)
    v = gv.reshape(b, l, GLA_HEADS, GLA_DV)
    log_a = jax.nn.log_sigmoid((glr @ gla_w_gate_up + gla_b_gate).astype(jnp.float32)) / GLA_TAU
    o_d = gla_chunked(q, k, v, log_a.reshape(b, l, GLA_HEADS, GLA_DK))
    o_d = rms_norm(o_d, gla_norm_w.reshape(GLA_HEADS, GLA_DV)).reshape(b, l, GLA_WIDTH) * jax.nn.silu(gr)
    return jnp.concatenate([o_c.astype(x.dtype), o_d.astype(x.dtype)], axis=-1) @ w_out


def setup_inputs(seed: int = 0) -> dict:
    key = jax.random.key(seed)
    ks = iter(jax.random.split(key, 48))
    f32 = jnp.float32

    def nrm(shape, scale):
        return scale * jax.random.normal(next(ks), shape, f32)

    def unif(shape, lo, hi):
        return jax.random.uniform(next(ks), shape, f32, minval=lo, maxval=hi)

    dt0 = jnp.exp(unif((N_EVEN, SSD_HEADS), math.log(1e-3), math.log(1e-1)))
    return {
        "x": nrm((BATCH, SEQ, D_MODEL), 1.0),
        "p": nrm((DEPTH, BATCH, SEQ, PLE_DIM), 1.0),
        "ln_g": 1.0 + nrm((DEPTH, 3, D_MODEL), 0.02),
        "ln_b": nrm((DEPTH, 3, D_MODEL), 0.02),
        "ffn_w_gate": nrm((DEPTH, 2, D_MODEL, D_FF), D_MODEL ** -0.5),
        "ffn_w_up": nrm((DEPTH, 2, D_MODEL, D_FF), D_MODEL ** -0.5),
        "ffn_w_down": nrm((DEPTH, 2, D_FF, D_MODEL), DN_BETA * D_FF ** -0.5),
        "ple_w_gate": nrm((DEPTH, D_MODEL, D_MODEL), D_MODEL ** -0.5),
        "ple_w_proj": nrm((DEPTH, PLE_DIM, D_MODEL), PLE_DIM ** -0.5),
        "ab_w_in": nrm((N_EVEN, D_MODEL, AB_IN), D_MODEL ** -0.5),
        "ab_w_out": nrm((N_EVEN, AB_OUT, D_MODEL), DN_BETA * AB_OUT ** -0.5),
        "ssd_conv_w": nrm((N_EVEN, SSD_CONV, SSD_CONV_DIM), SSD_CONV ** -0.5),
        "ssd_conv_b": nrm((N_EVEN, SSD_CONV_DIM), 0.02),
        "ssd_dt_bias": dt0 + jnp.log(-jnp.expm1(-dt0)),
        "ssd_a_log": jnp.log(unif((N_EVEN, SSD_HEADS), 1.0, 16.0)),
        "ssd_d": 1.0 + nrm((N_EVEN, SSD_HEADS), 0.02),
        "ssd_norm_w": 1.0 + nrm((N_EVEN, SSD_INNER), 0.02),
        "s5_lambda_re": -0.5 + nrm((N_EVEN, S5_GROUPS, S5_STATE), 0.01),
        "s5_lambda_im": jnp.pi * jnp.arange(S5_STATE, dtype=f32) + nrm((N_EVEN, S5_GROUPS, S5_STATE), 0.01),
        "s5_log_dt": unif((N_EVEN, S5_GROUPS), math.log(1e-3), math.log(1e-1)),
        "s5_b_re": nrm((N_EVEN, S5_GROUPS, S5_STATE, S5_GROUP), (2.0 * S5_GROUP) ** -0.5),
        "s5_b_im": nrm((N_EVEN, S5_GROUPS, S5_STATE, S5_GROUP), (2.0 * S5_GROUP) ** -0.5),
        "s5_c_re": nrm((N_EVEN, S5_GROUPS, S5_GROUP, S5_STATE), S5_STATE ** -0.5),
        "s5_c_im": nrm((N_EVEN, S5_GROUPS, S5_GROUP, S5_STATE), S5_STATE ** -0.5),
        "s5_d": nrm((N_EVEN, S5_GROUPS, S5_GROUP), 1.0),
        "s5_w_glu": nrm((N_EVEN, S5_WIDTH, S5_WIDTH), S5_WIDTH ** -0.5),
        "s5_b_glu": nrm((N_EVEN, S5_WIDTH), 0.02),
        "cd_w_in": nrm((N_ODD, D_MODEL, CD_IN), D_MODEL ** -0.5),
        "cd_w_out": nrm((N_ODD, CD_OUT, D_MODEL), DN_BETA * CD_OUT ** -0.5),
        "hgrn_lb_logits": nrm((DEPTH, HGRN_HEADS * HGRN_KEY), 0.5),
        "hgrn_norm_w": 1.0 + nrm((N_ODD, HGRN_WIDTH), 0.02),
        "gla_w_gate_up": nrm((N_ODD, GLA_RANK, GLA_HEADS * GLA_DK), GLA_RANK ** -0.5),
        "gla_b_gate": nrm((N_ODD, GLA_HEADS * GLA_DK), 0.1),
        "gla_norm_w": 1.0 + nrm((N_ODD, GLA_WIDTH), 0.02),
    }


def reference(x, p, ln_g, ln_b, ffn_w_gate, ffn_w_up, ffn_w_down, ple_w_gate, ple_w_proj,
              ab_w_in, ab_w_out, ssd_conv_w, ssd_conv_b, ssd_dt_bias, ssd_a_log, ssd_d, ssd_norm_w,
              s5_lambda_re, s5_lambda_im, s5_log_dt, s5_b_re, s5_b_im, s5_c_re, s5_c_im, s5_d,
              s5_w_glu, s5_b_glu, cd_w_in, cd_w_out, hgrn_lb_logits, hgrn_norm_w,
              gla_w_gate_up, gla_b_gate, gla_norm_w):
    for i in range(DEPTH):
        j = i // 2
        x = layer_norm(DN_ALPHA * x + 0.5 * swiglu(x, ffn_w_gate[i, 0], ffn_w_up[i, 0], ffn_w_down[i, 0]),
                       ln_g[i, 0], ln_b[i, 0])
        if i % 2 == 0:
            mix = mixer_ab(x, ab_w_in[j], ab_w_out[j], ssd_conv_w[j], ssd_conv_b[j], ssd_dt_bias[j],
                           ssd_a_log[j], ssd_d[j], ssd_norm_w[j], s5_lambda_re[j], s5_lambda_im[j],
                           s5_log_dt[j], s5_b_re[j], s5_b_im[j], s5_c_re[j], s5_c_im[j], s5_d[j],
                           s5_w_glu[j], s5_b_glu[j])
        else:
            mix = mixer_cd(x, cd_w_in[j], cd_w_out[j], hgrn_lower_bound(hgrn_lb_logits, i),
                           hgrn_norm_w[j], gla_w_gate_up[j], gla_b_gate[j], gla_norm_w[j])
        x = layer_norm(DN_ALPHA * x + mix, ln_g[i, 1], ln_b[i, 1])
        x = layer_norm(DN_ALPHA * x + 0.5 * swiglu(x, ffn_w_gate[i, 1], ffn_w_up[i, 1], ffn_w_down[i, 1]),
                       ln_g[i, 2], ln_b[i, 2])
        x = x + jax.nn.sigmoid(x @ ple_w_gate[i]) * (p[i] @ ple_w_proj[i])
    return x
```

```python
import functools
import math

import jax
import jax.numpy as jnp
import numpy as np
from jax import lax
from jax.experimental import pallas as pl
from jax.experimental.pallas import tpu as pltpu

F32 = jnp.float32
BF16 = jnp.bfloat16

D_MODEL = 1024
D_FF = 2816
PLE_DIM = 256
DEPTH = 2
DN_ALPHA = (2.0 * DEPTH) ** 0.25
LN_EPS = 1e-5
SSD_HEADS = 16
SSD_HEAD_DIM = 64
SSD_GROUPS = 4
SSD_STATE = 128
SSD_CONV = 4
SSD_INNER = 1024
SSD_BC = SSD_GROUPS * SSD_STATE
SSD_CONV_DIM = SSD_INNER + 2 * SSD_BC
SSD_GROUP_WIDTH = SSD_INNER // SSD_GROUPS
SSD_HEADS_PER_GROUP = SSD_HEADS // SSD_GROUPS
S5_WIDTH = 1024
S5_GROUPS = 64
S5_GROUP = 16
S5_STATE = 64
LIN_HEADS = 4
HGRN_WIDTH = 512
GLA_DK = 64
GLA_QK = LIN_HEADS * GLA_DK
GLA_WIDTH = 512
GLA_RANK = 16
GLA_TAU = 16.0
HEAD_W = 128

LANES = 128
SUBLANES = 8
VMEM_LIMIT = 56 * 1024 * 1024

TM = 512
FF_CHUNK = 256
SSD_CHUNK = 128
LIN_CHUNK = 128
S5_Q = 32
S5_ROW = S5_Q * S5_GROUP

AB_XBC, AB_Z, AB_U, AB_DT, AB_PACK = 0, 2048, 3072, 4096, 4224
CD_H, CD_GQK, CD_GV, CD_GR, CD_GLR, CD_PACK = 0, 2048, 2560, 3072, 3584, 3712


def _resident(shape):
    n = len(shape)
    return pl.BlockSpec(shape, lambda *_: (0,) * n)


def _dot(a, b):
    return jnp.dot(a, b, preferred_element_type=F32)


def _dot_nt(a, b):
    return lax.dot_general(a, b, (((1,), (1,)), ((), ())), preferred_element_type=F32)


def _dot_tn(a, b):
    return lax.dot_general(a, b, (((0,), (0,)), ((), ())), preferred_element_type=F32)


def _split3(v):
    hi = v.astype(BF16)
    r = v - hi.astype(F32)
    mid = r.astype(BF16)
    lo = (r - mid.astype(F32)).astype(BF16)
    return hi, mid, lo


def _sel_dot(sel, v):
    hi, mid, lo = _split3(v)
    return _dot(sel, hi) + _dot(sel, mid) + _dot(sel, lo)


def _dot_sel(v, sel):
    hi, mid, lo = _split3(v)
    return _dot(hi, sel) + _dot(mid, sel) + _dot(lo, sel)


def _sigmoid(x):
    return 1.0 / (1.0 + jnp.exp(-x))


def _silu(x):
    return x * _sigmoid(x)


def _log_sigmoid(x):
    return jnp.minimum(x, 0.0) - jnp.log1p(jnp.exp(-jnp.abs(x)))


def _softplus(x):
    return jnp.maximum(x, 0.0) + jnp.log1p(jnp.exp(-jnp.abs(x)))


def _gelu_tanh(x):
    return 0.5 * x * (1.0 + jnp.tanh(math.sqrt(2.0 / math.pi) * (x + 0.044715 * (x * x * x))))


def _layer_norm(y, g, b):
    mu = jnp.mean(y, axis=-1, keepdims=True)
    yc = y - mu
    var = jnp.mean(yc * yc, axis=-1, keepdims=True)
    return yc * lax.rsqrt(var + LN_EPS) * g + b


def _rms_norm(y, w):
    return y * lax.rsqrt(jnp.mean(y * y, axis=-1, keepdims=True) + LN_EPS) * w


def _params(*sem):
    return pltpu.CompilerParams(dimension_semantics=sem, vmem_limit_bytes=VMEM_LIMIT)


def _ffn_body(with_ple, x_ref, wg_ref, wu_ref, wd_ref, g_ref, b_ref, *rest):
    if with_ple:
        p_ref, pg_ref, pp_ref, o_ref, acc_ref = rest
    else:
        o_ref, acc_ref = rest
    x = x_ref[...]
    xb = x.astype(BF16)
    for c in range(D_FF // FF_CHUNK):
        sl = slice(c * FF_CHUNK, (c + 1) * FF_CHUNK)
        gate = _dot(xb, wg_ref[:, sl])
        up = _dot(xb, wu_ref[:, sl])
        h = (_silu(gate) * up).astype(BF16)
        d = _dot(h, wd_ref[sl, :])
        if c == 0:
            acc_ref[...] = d
        else:
            acc_ref[...] += d
    y = _layer_norm(DN_ALPHA * x + 0.5 * acc_ref[...], g_ref[...], b_ref[...])
    if with_ple:
        gate = _sigmoid(_dot(y.astype(BF16), pg_ref[...]))
        y = y + gate * _dot(p_ref[...].astype(BF16), pp_ref[...])
    o_ref[...] = y


def _ffn_ln(x, wg, wu, wd, g, b, ple=None):
    t = x.shape[0]
    row = lambda i: (i, 0)
    in_specs = [pl.BlockSpec((TM, D_MODEL), row), _resident(wg.shape), _resident(wu.shape),
                _resident(wd.shape), _resident(g.shape), _resident(b.shape)]
    args = [x, wg, wu, wd, g, b]
    if ple is not None:
        p, pg, pp = ple
        in_specs += [pl.BlockSpec((TM, PLE_DIM), row), _resident(pg.shape), _resident(pp.shape)]
        args += [p, pg, pp]
    return pl.pallas_call(
        functools.partial(_ffn_body, ple is not None),
        grid=(t // TM,),
        in_specs=in_specs,
        out_specs=pl.BlockSpec((TM, D_MODEL), row),
        out_shape=jax.ShapeDtypeStruct((t, D_MODEL), F32),
        scratch_shapes=[pltpu.VMEM((TM, D_MODEL), F32)],
        compiler_params=_params("parallel"),
        name="ffn_ln_ple" if ple is not None else "ffn_ln",
    )(*args)


def _proj_in_body(n_out, x_ref, w_ref, o_ref):
    xb = x_ref[...].astype(BF16)
    step = 512
    for c0 in range(0, n_out, step):
        c1 = min(c0 + step, n_out)
        o_ref[:, c0:c1] = _dot(xb, w_ref[:, c0:c1])


def _proj_in(x, w):
    t, n_out = x.shape[0], w.shape[1]
    return pl.pallas_call(
        functools.partial(_proj_in_body, n_out),
        grid=(t // TM,),
        in_specs=[pl.BlockSpec((TM, D_MODEL), lambda i: (i, 0)), _resident(w.shape)],
        out_specs=pl.BlockSpec((TM, n_out), lambda i: (i, 0)),
        out_shape=jax.ShapeDtypeStruct((t, n_out), F32),
        compiler_params=_params("parallel"),
        name="proj_in",
    )(x, w)


def _proj_ln_body(n_pairs, x_ref, *rest):
    a_refs = rest[:n_pairs]
    w_refs = rest[n_pairs:2 * n_pairs]
    g_ref, b_ref, o_ref = rest[2 * n_pairs:]
    mix = _dot(a_refs[0][...], w_refs[0][...])
    for a_ref, w_ref in zip(a_refs[1:], w_refs[1:]):
        mix = mix + _dot(a_ref[...], w_ref[...])
    o_ref[...] = _layer_norm(DN_ALPHA * x_ref[...] + mix, g_ref[...], b_ref[...])


def _proj_ln(x, acts, weights, g, b):
    t = x.shape[0]
    row = lambda i: (i, 0)
    in_specs = [pl.BlockSpec((TM, D_MODEL), row)]
    in_specs += [pl.BlockSpec((TM, a.shape[1]), row) for a in acts]
    in_specs += [_resident(w.shape) for w in weights]
    in_specs += [_resident(g.shape), _resident(b.shape)]
    return pl.pallas_call(
        functools.partial(_proj_ln_body, len(acts)),
        grid=(t // TM,),
        in_specs=in_specs,
        out_specs=pl.BlockSpec((TM, D_MODEL), row),
        out_shape=jax.ShapeDtypeStruct((t, D_MODEL), F32),
        compiler_params=_params("parallel"),
        name="proj_ln",
    )(x, *acts, *weights, g, b)


def _ssd_body(xbc_ref, z_ref, dt_ref, convw_ref, convb_ref, dtb_ref, alog_ref, dskip_ref,
              normw_ref, tril_ref, expand_ref, o_ref, xpad_ref, state_ref):
    ch = SSD_CHUNK
    gw = SSD_GROUP_WIDTH

    @pl.when(pl.program_id(1) == 0)
    def _():
        xpad_ref[0:SUBLANES, :] = jnp.zeros((SUBLANES, SSD_CONV_DIM), F32)
        state_ref[...] = jnp.zeros(state_ref.shape, F32)

    cur = xbc_ref[...]
    xpad_ref[SUBLANES:SUBLANES + ch, :] = cur
    acc = cur * convw_ref[SSD_CONV - 1:SSD_CONV, :] + convb_ref[...]
    for j in range(1, SSD_CONV):
        acc = acc + xpad_ref[pl.ds(SUBLANES - j, ch), :] * convw_ref[SSD_CONV - 1 - j:SSD_CONV - j, :]
    xpad_ref[0:SUBLANES, :] = cur[ch - SUBLANES:ch, :]
    xc = _silu(acc)
    xs = xc[:, :SSD_INNER]
    bm = xc[:, SSD_INNER:SSD_INNER + SSD_BC].astype(BF16)
    cm = xc[:, SSD_INNER + SSD_BC:].astype(BF16)

    dt = _softplus(dt_ref[...] + dtb_ref[...])
    da = dt * (-jnp.exp(alog_ref[...]))
    a_cum = _sel_dot(tril_ref[...], da)
    a_cum_t = a_cum.T
    dt_e = _dot_sel(dt, expand_ref[...])
    ac_e = _dot_sel(a_cum, expand_ref[...])
    ac_last = ac_e[ch - 1:ch, :]
    xdt = xs * dt_e
    xdt_b = xdt.astype(BF16)
    xdt_end = (xdt * jnp.exp(ac_last - ac_e)).astype(BF16)
    carry_scale = jnp.exp(ac_e)
    chunk_decay = jnp.exp(ac_last)

    row = lax.broadcasted_iota(jnp.int32, (ch, ch), 0)
    col = lax.broadcasted_iota(jnp.int32, (ch, ch), 1)
    causal = col <= row
    lane_head = lax.broadcasted_iota(jnp.int32, (ch, gw), 1) >> int(math.log2(SSD_HEAD_DIM))

    for g in range(SSD_GROUPS):
        gs = slice(g * gw, (g + 1) * gw)
        ns = slice(g * SSD_STATE, (g + 1) * SSD_STATE)
        b_g = bm[:, ns]
        c_g = cm[:, ns]
        scores = _dot_nt(c_g, b_g)
        x_g = xdt_b[:, gs]
        y_g = jnp.zeros((ch, gw), F32)
        for hh in range(SSD_HEADS_PER_GROUP):
            h = g * SSD_HEADS_PER_GROUP + hh
            seg = a_cum[:, h:h + 1] - a_cum_t[h:h + 1, :]
            decay = jnp.where(causal, jnp.exp(jnp.minimum(seg, 0.0)), 0.0)
            full = _dot((scores * decay).astype(BF16), x_g)
            y_g = jnp.where(lane_head == hh, full, y_g)
        state = state_ref[g]
        y_g = y_g + _dot(c_g, state.astype(BF16)) * carry_scale[:, gs]
        state_ref[g] = state * chunk_decay[:, gs] + _dot_tn(b_g, xdt_end[:, gs])
        y_g = y_g + xs[:, gs] * dskip_ref[:, gs]
        y_g = y_g * _silu(z_ref[:, gs])
        o_ref[:, gs] = _rms_norm(y_g, normw_ref[:, gs]).astype(o_ref.dtype)


def _ssd_mixer(h_ab, batch, seq, conv_w, conv_b, dt_bias, a_log, d_skip, norm_w):
    ch = SSD_CHUNK
    nc = seq // ch
    tok = lambda cb: (lambda b, c: (b * nc + c, cb))
    pad_heads = lambda v: jnp.zeros((1, LANES), F32).at[0, :SSD_HEADS].set(v.astype(F32))
    tril = jnp.asarray(np.tril(np.ones((ch, ch), np.float32)), BF16)
    expand = np.zeros((LANES, SSD_INNER), np.float32)
    for h in range(SSD_HEADS):
        expand[h, h * SSD_HEAD_DIM:(h + 1) * SSD_HEAD_DIM] = 1.0
    expand = jnp.asarray(expand, BF16)
    dskip_e = jnp.repeat(d_skip.astype(F32), SSD_HEAD_DIM)[None, :]
    consts = [conv_w, conv_b[None, :], pad_heads(dt_bias), pad_heads(a_log), dskip_e,
              norm_w[None, :], tril, expand]
    in_specs = [
        pl.BlockSpec((ch, SSD_CONV_DIM), tok(AB_XBC // SSD_CONV_DIM)),
        pl.BlockSpec((ch, SSD_INNER), tok(AB_Z // SSD_INNER)),
        pl.BlockSpec((ch, LANES), tok(AB_DT // LANES)),
    ] + [_resident(a.shape) for a in consts]
    return pl.pallas_call(
        _ssd_body,
        grid=(batch, nc),
        in_specs=in_specs,
        out_specs=pl.BlockSpec((ch, SSD_INNER), tok(0)),
        out_shape=jax.ShapeDtypeStruct((batch * seq, SSD_INNER), BF16),
        scratch_shapes=[pltpu.VMEM((SUBLANES + ch, SSD_CONV_DIM), F32),
                        pltpu.VMEM((SSD_GROUPS, SSD_STATE, SSD_GROUP_WIDTH), F32)],
        compiler_params=_params("parallel", "arbitrary"),
        name="ssd_mixer",
    )(h_ab, h_ab, h_ab, *consts)


def _s5_body(nblk, u_ref, m_ref, wst_ref, woff_ref, a1_ref, a2_ref, o_ref):
    u = u_ref[0]
    y = _dot(u, m_ref[0])
    x = _dot(u, wst_ref[0])
    blk = lax.broadcasted_iota(jnp.int32, x.shape, 0) & (nblk - 1)
    a1 = a1_ref[0]
    a2 = a2_ref[0]
    for k in range(int(math.log2(nblk))):
        sh = 1 << k
        prev = jnp.where(blk >= sh, pltpu.roll(x, sh, axis=0), 0.0)
        x = x + prev * a1[k:k + 1, :] + pltpu.roll(prev, S5_STATE, axis=1) * a2[k:k + 1, :]
    x_in = jnp.where(blk >= 1, pltpu.roll(x, 1, axis=0), 0.0)
    o_ref[0] = y + _dot(x_in.astype(BF16), woff_ref[0])


def _s5_tables(lam_re, lam_im, log_dt, b_re, b_im, c_re, c_im, nblk):
    q = S5_Q
    hp = lax.Precision.HIGHEST
    lam = lax.complex(lam_re.astype(F32), lam_im.astype(F32))
    ldt = lam * jnp.exp(log_dt.astype(F32))[:, None]
    lam_bar = jnp.exp(ldt)
    b_bar = ((lam_bar - 1.0) / lam)[..., None] * lax.complex(b_re.astype(F32), b_im.astype(F32))
    c = lax.complex(c_re.astype(F32), c_im.astype(F32))
    tau = jnp.arange(q + 1, dtype=F32)
    pw = jnp.exp(ldt[:, None, :] * tau[None, :, None])
    kern = jnp.real(jnp.einsum('gpn,gtn,gnq->gtpq', c, pw[:, :q], b_bar, precision=hp))
    kern = jnp.concatenate([kern, jnp.zeros_like(kern[:, :1])], axis=1)
    s_idx = np.arange(q)[:, None]
    t_idx = np.arange(q)[None, :]
    lag = np.where(t_idx >= s_idx, t_idx - s_idx, q)
    m = kern[:, lag]
    m = jnp.transpose(m, (0, 1, 4, 2, 3)).reshape(S5_GROUPS, S5_ROW, S5_ROW)
    st = pw[:, q - 1 - np.arange(q), :][:, :, None, :] * jnp.transpose(b_bar, (0, 2, 1))[:, None]
    st = st.reshape(S5_GROUPS, S5_ROW, S5_STATE)
    wst = jnp.concatenate([jnp.real(st), jnp.imag(st)], axis=-1)
    off = jnp.transpose(c, (0, 2, 1))[:, :, None, :] * jnp.transpose(pw[:, 1:], (0, 2, 1))[..., None]
    off = off.reshape(S5_GROUPS, S5_STATE, S5_ROW)
    woff = jnp.concatenate([jnp.real(off), -jnp.imag(off)], axis=1)
    nlev = int(math.log2(nblk))
    step = jnp.exp(ldt[:, None, :] * (q * 2.0 ** jnp.arange(nlev, dtype=F32))[None, :, None])
    a1 = jnp.concatenate([jnp.real(step), jnp.real(step)], axis=-1)
    a2 = jnp.concatenate([-jnp.imag(step), jnp.imag(step)], axis=-1)
    return m.astype(BF16), wst.astype(BF16), woff.astype(BF16), a1, a2


def _s5_core(u_g, tables, nblk):
    m, wst, woff, a1, a2 = tables
    rows = u_g.shape[1]
    grp = lambda shape: pl.BlockSpec((1,) + shape, lambda g: (g, 0, 0))
    return pl.pallas_call(
        functools.partial(_s5_body, nblk),
        grid=(S5_GROUPS,),
        in_specs=[grp((rows, S5_ROW)), grp((S5_ROW, S5_ROW)), grp((S5_ROW, 2 * S5_STATE)),
                  grp((2 * S5_STATE, S5_ROW)), grp(a1.shape[1:]), grp(a2.shape[1:])],
        out_specs=grp((rows, S5_ROW)),
        out_shape=jax.ShapeDtypeStruct((S5_GROUPS, rows, S5_ROW), F32),
        compiler_params=_params("parallel"),
        name="s5_core",
    )(u_g, m, wst, woff, a1, a2)


def _s5_post_body(yc_ref, u_ref, d_ref, w_ref, b_ref, o_ref):
    y = _gelu_tanh(yc_ref[...] + d_ref[...] * u_ref[...])
    gate = _sigmoid(_dot(y.astype(BF16), w_ref[...]) + b_ref[...])
    o_ref[...] = (y * gate).astype(o_ref.dtype)


def _s5_post(y_conv, h_ab, d_skip, w_glu, b_glu):
    t = y_conv.shape[0]
    consts = [d_skip, w_glu, b_glu]
    return pl.pallas_call(
        _s5_post_body,
        grid=(t // TM,),
        in_specs=[pl.BlockSpec((TM, S5_WIDTH), lambda i: (i, 0)),
                  pl.BlockSpec((TM, S5_WIDTH), lambda i: (i, AB_U // S5_WIDTH))]
        + [_resident(a.shape) for a in consts],
        out_specs=pl.BlockSpec((TM, S5_WIDTH), lambda i: (i, 0)),
        out_shape=jax.ShapeDtypeStruct((t, S5_WIDTH), BF16),
        compiler_params=_params("parallel"),
        name="s5_post",
    )(y_conv, h_ab, *consts)


def _lin_levels(c):
    return [c >> (i + 1) for i in range(int(math.log2(c)))]


def _lin_tables(c):
    t = np.arange(c)[:, None]
    j = np.arange(c)[None, :]
    blocks = [(j <= t), (j > t)]
    lvl = np.full((c, c), -1, np.int32)
    lvl[np.arange(c), np.arange(c)] = 0
    for i, b in enumerate(_lin_levels(c)):
        mid = (t // (2 * b)) * (2 * b) + b - 1
        second = (t % (2 * b)) >= b
        blocks.append(np.where(second, (j > mid) & (j <= t), (j > t) & (j <= mid)))
        same = (t // (2 * b)) == (j // (2 * b))
        pair = same & second & ((j % (2 * b)) < b)
        lvl[pair] = i + 1
    sel = np.concatenate(blocks, axis=0).astype(np.float32)
    return jnp.asarray(sel, BF16), jnp.asarray(lvl)


def _lin_body(h_ref, gqk_ref, gv_ref, gr_ref, glr_ref, loglb_ref, log1mlb_ref, onemlb_ref, hnorm_ref,
              wup_ref, bgate_ref, gnorm_ref, sel_ref, lvl_ref, o_ref, state_ref):
    c = LIN_CHUNK
    levels = _lin_levels(c)

    @pl.when(pl.program_id(1) == 0)
    def _():
        state_ref[...] = jnp.zeros(state_ref.shape, F32)

    hq = h_ref[:, 0:HGRN_WIDTH]
    hf = h_ref[:, HGRN_WIDTH:2 * HGRN_WIDTH]
    lo = loglb_ref[...]
    hi_ = log1mlb_ref[...] + _log_sigmoid(hf)
    log_f = jnp.maximum(lo, hi_) + jnp.log1p(jnp.exp(-jnp.abs(lo - hi_)))
    q_c = _silu(hq)
    k_c = onemlb_ref[...] * _sigmoid(-hf)
    pre = _dot(glr_ref[...].astype(BF16), wup_ref[...]) + bgate_ref[...]
    log_a = _log_sigmoid(pre) * (1.0 / GLA_TAU)
    q_d = gqk_ref[:, 0:GLA_QK] * (GLA_DK ** -0.5)
    k_d = gqk_ref[:, GLA_QK:2 * GLA_QK]

    lvl = lvl_ref[...]
    lane = lax.broadcasted_iota(jnp.int32, (c, LANES), 1)
    sel = sel_ref[...]

    def head(idx, q, k, e_all, v, norm_w, gate):
        attn = jnp.where(lvl == 0, _dot_nt(q.astype(BF16), k.astype(BF16)), 0.0)
        for i in range(len(levels)):
            e = e_all[(2 + i) * c:(3 + i) * c, :]
            a = _dot_nt((q * e).astype(BF16), (k * e).astype(BF16))
            attn = jnp.where(lvl == i + 1, a, attn)
        vb = v.astype(BF16)
        state_t = state_ref[idx]
        o = _dot(attn.astype(BF16), vb) + _dot_nt((q * e_all[0:c, :]).astype(BF16), state_t.astype(BF16))
        k_end = (k * e_all[c:2 * c, :]).astype(BF16)
        state_ref[idx] = state_t * e_all[c - 1:c, :] + _dot_tn(vb, k_end)
        return (_rms_norm(o, norm_w) * _silu(gate)).astype(o_ref.dtype)

    for hd in range(LIN_HEADS):
        ls = slice(hd * HEAD_W, (hd + 1) * HEAD_W)
        e_all = jnp.exp(_sel_dot(sel, log_f[:, ls]))
        o_ref[:, ls] = head(hd, q_c[:, ls], k_c[:, ls], e_all,
                            h_ref[:, 2 * HGRN_WIDTH + hd * HEAD_W:2 * HGRN_WIDTH + (hd + 1) * HEAD_W],
                            hnorm_ref[:, ls],
                            h_ref[:, 3 * HGRN_WIDTH + hd * HEAD_W:3 * HGRN_WIDTH + (hd + 1) * HEAD_W])
    for tile in range(GLA_QK // LANES):
        ts = slice(tile * LANES, (tile + 1) * LANES)
        e_all = jnp.exp(_sel_dot(sel, log_a[:, ts]))
        for half in range(LANES // GLA_DK):
            hd = tile * (LANES // GLA_DK) + half
            mine = (lane >> int(math.log2(GLA_DK))) == half
            ls = slice(hd * HEAD_W, (hd + 1) * HEAD_W)
            o_ref[:, HGRN_WIDTH + hd * HEAD_W:HGRN_WIDTH + (hd + 1) * HEAD_W] = head(
                LIN_HEADS + hd, jnp.where(mine, q_d[:, ts], 0.0), jnp.where(mine, k_d[:, ts], 0.0),
                e_all, gv_ref[:, ls], gnorm_ref[:, ls], gr_ref[:, ls])


def _lin_mixer(h_cd, batch, seq, lb, hgrn_norm_w, gla_w_gate_up, gla_b_gate, gla_norm_w):
    c = LIN_CHUNK
    nc = seq // c
    tok = lambda cb: (lambda b, i: (b * nc + i, cb))
    lb = lb.astype(F32)[None, :]
    wup = jnp.zeros((LANES, GLA_QK), F32).at[:GLA_RANK].set(gla_w_gate_up).astype(BF16)
    sel, lvl = _lin_tables(c)
    consts = [jnp.log(lb), jnp.log1p(-lb), 1.0 - lb, hgrn_norm_w[None, :], wup, gla_b_gate[None, :],
              gla_norm_w[None, :], sel, lvl]
    in_specs = [
        pl.BlockSpec((c, 4 * HGRN_WIDTH), tok(CD_H // (4 * HGRN_WIDTH))),
        pl.BlockSpec((c, 2 * GLA_QK), tok(CD_GQK // (2 * GLA_QK))),
        pl.BlockSpec((c, GLA_WIDTH), tok(CD_GV // GLA_WIDTH)),
        pl.BlockSpec((c, GLA_WIDTH), tok(CD_GR // GLA_WIDTH)),
        pl.BlockSpec((c, LANES), tok(CD_GLR // LANES)),
    ] + [_resident(a.shape) for a in consts]
    return pl.pallas_call(
        _lin_body,
        grid=(batch, nc),
        in_specs=in_specs,
        out_specs=pl.BlockSpec((c, HGRN_WIDTH + GLA_WIDTH), tok(0)),
        out_shape=jax.ShapeDtypeStruct((batch * seq, HGRN_WIDTH + GLA_WIDTH), BF16),
        scratch_shapes=[pltpu.VMEM((2 * LIN_HEADS, HEAD_W, HEAD_W), F32)],
        compiler_params=_params("parallel", "arbitrary"),
        name="lin_mixer",
    )(h_cd, h_cd, h_cd, h_cd, h_cd, *consts)


def _pack_cols(w, splits, order, total):
    bounds = np.concatenate([[0], np.cumsum(splits)])
    out = jnp.zeros((w.shape[0], total), w.dtype)
    for f, start in order:
        out = out.at[:, start:start + splits[f]].set(w[:, bounds[f]:bounds[f + 1]])
    return out.astype(BF16)


def _hgrn_lower_bound(lb_logits, layer):
    cum = jnp.cumsum(jax.nn.softmax(lb_logits.astype(F32), axis=0), axis=0)
    return cum[layer] - cum[0]


def kernel(x, p, ln_g, ln_b, ffn_w_gate, ffn_w_up, ffn_w_down, ple_w_gate, ple_w_proj, ab_w_in, ab_w_out,
           ssd_conv_w, ssd_conv_b, ssd_dt_bias, ssd_a_log, ssd_d, ssd_norm_w, s5_lambda_re, s5_lambda_im,
           s5_log_dt, s5_b_re, s5_b_im, s5_c_re, s5_c_im, s5_d, s5_w_glu, s5_b_glu, cd_w_in, cd_w_out,
           hgrn_lb_logits, hgrn_norm_w, gla_w_gate_up, gla_b_gate, gla_norm_w):
    batch, seq, _ = x.shape
    t = batch * seq
    x = x.reshape(t, D_MODEL)
    bf = lambda w: w.astype(BF16)
    ab_splits = (SSD_INNER, SSD_CONV_DIM, SSD_HEADS, S5_WIDTH)
    cd_splits = (HGRN_WIDTH, HGRN_WIDTH, HGRN_WIDTH, HGRN_WIDTH, GLA_QK, GLA_QK, GLA_WIDTH, GLA_RANK,
                 GLA_WIDTH)
    for i in range(DEPTH):
        j = i // 2
        ln = lambda k: (ln_g[i, k][None, :], ln_b[i, k][None, :])
        x = _ffn_ln(x, bf(ffn_w_gate[i, 0]), bf(ffn_w_up[i, 0]), bf(ffn_w_down[i, 0]), *ln(0))
        if i % 2 == 0:
            w_in = _pack_cols(ab_w_in[j], ab_splits, [(1, AB_XBC), (0, AB_Z), (3, AB_U), (2, AB_DT)], AB_PACK)
            h_ab = _proj_in(x, w_in)
            y_a = _ssd_mixer(h_ab, batch, seq, ssd_conv_w[j], ssd_conv_b[j], ssd_dt_bias[j], ssd_a_log[j],
                             ssd_d[j], ssd_norm_w[j])
            nblk = seq // S5_Q
            u_g = h_ab[:, AB_U:AB_U + S5_WIDTH].astype(BF16).reshape(t // S5_Q, S5_Q, S5_GROUPS, S5_GROUP)
            u_g = jnp.transpose(u_g, (2, 0, 1, 3)).reshape(S5_GROUPS, t // S5_Q, S5_ROW)
            tables = _s5_tables(s5_lambda_re[j], s5_lambda_im[j], s5_log_dt[j], s5_b_re[j], s5_b_im[j],
                                s5_c_re[j], s5_c_im[j], nblk)
            y_conv = _s5_core(u_g, tables, nblk).reshape(S5_GROUPS, t // S5_Q, S5_Q, S5_GROUP)
            y_conv = jnp.transpose(y_conv, (1, 2, 0, 3)).reshape(t, S5_WIDTH)
            y_b = _s5_post(y_conv, h_ab, s5_d[j].reshape(1, S5_WIDTH).astype(F32), bf(s5_w_glu[j]),
                           s5_b_glu[j][None, :])
            w_out = bf(ab_w_out[j])
            x = _proj_ln(x, [y_a, y_b], [w_out[:SSD_INNER], w_out[SSD_INNER:]], *ln(1))
        else:
            w_in = _pack_cols(cd_w_in[j], cd_splits,
                              [(0, CD_H), (1, CD_H + HGRN_WIDTH), (2, CD_H + 2 * HGRN_WIDTH),
                               (3, CD_H + 3 * HGRN_WIDTH), (4, CD_GQK), (5, CD_GQK + GLA_QK), (6, CD_GV),
                               (8, CD_GR), (7, CD_GLR)], CD_PACK)
            h_cd = _proj_in(x, w_in)
            o_cd = _lin_mixer(h_cd, batch, seq, _hgrn_lower_bound(hgrn_lb_logits, i), hgrn_norm_w[j],
                              gla_w_gate_up[j], gla_b_gate[j], gla_norm_w[j])
            x = _proj_ln(x, [o_cd], [bf(cd_w_out[j])], *ln(1))
        x = _ffn_ln(x, bf(ffn_w_gate[i, 1]), bf(ffn_w_up[i, 1]), bf(ffn_w_down[i, 1]), *ln(2),
                    ple=(p[i].reshape(t, PLE_DIM), bf(ple_w_gate[i]), bf(ple_w_proj[i])))
    return x.reshape(batch, seq, D_MODEL)
```

```python
import functools
import math

import jax
import jax.numpy as jnp
import numpy as np
from jax import lax
from jax.experimental import pallas as pl
from jax.experimental.pallas import tpu as pltpu

F32 = jnp.float32
BF16 = jnp.bfloat16

D_MODEL = 1024
D_FF = 2816
PLE_DIM = 256
DEPTH = 2
DN_ALPHA = (2.0 * DEPTH) ** 0.25
LN_EPS = 1e-5
SSD_HEADS = 16
SSD_HEAD_DIM = 64
SSD_GROUPS = 4
SSD_STATE = 128
SSD_CONV = 4
SSD_INNER = 1024
SSD_BC = SSD_GROUPS * SSD_STATE
SSD_CONV_DIM = SSD_INNER + 2 * SSD_BC
SSD_GROUP_WIDTH = SSD_INNER // SSD_GROUPS
SSD_HEADS_PER_GROUP = SSD_HEADS // SSD_GROUPS
S5_WIDTH = 1024
S5_GROUPS = 64
S5_GROUP = 16
S5_STATE = 64
LIN_HEADS = 4
HGRN_WIDTH = 512
GLA_DK = 64
GLA_QK = LIN_HEADS * GLA_DK
GLA_WIDTH = 512
GLA_RANK = 16
GLA_TAU = 16.0
HEAD_W = 128

LANES = 128
SUBLANES = 8
VMEM_LIMIT = 56 * 1024 * 1024

TM = 512
FF_CHUNK = 256
SSD_CHUNK = 128
LIN_CHUNK = 128
S5_Q = 32
S5_ROW = S5_Q * S5_GROUP
S5_TILE_GROUPS = LANES // S5_GROUP
S5_PITCH = 40

AB_XBC, AB_Z, AB_U, AB_DT, AB_PACK = 0, 2048, 3072, 4096, 4224
CD_H, CD_GQK, CD_GV, CD_GR, CD_GLR, CD_PACK = 0, 2048, 2560, 3072, 3584, 3712


def _resident(shape):
    n = len(shape)
    return pl.BlockSpec(shape, lambda *_: (0,) * n)


def _dot(a, b):
    return jnp.dot(a, b, preferred_element_type=F32)


def _dot_nt(a, b):
    return lax.dot_general(a, b, (((1,), (1,)), ((), ())), preferred_element_type=F32)


def _dot_tn(a, b):
    return lax.dot_general(a, b, (((0,), (0,)), ((), ())), preferred_element_type=F32)


def _split3(v):
    hi = v.astype(BF16)
    r = v - hi.astype(F32)
    mid = r.astype(BF16)
    lo = (r - mid.astype(F32)).astype(BF16)
    return hi, mid, lo


def _sel_dot(sel, v):
    hi, mid, lo = _split3(v)
    return _dot(sel, hi) + _dot(sel, mid) + _dot(sel, lo)


def _dot_sel(v, sel):
    hi, mid, lo = _split3(v)
    return _dot(hi, sel) + _dot(mid, sel) + _dot(lo, sel)


def _sigmoid(x):
    return 1.0 / (1.0 + jnp.exp(-x))


def _silu(x):
    return x * _sigmoid(x)


def _log_sigmoid(x):
    return jnp.minimum(x, 0.0) - jnp.log1p(jnp.exp(-jnp.abs(x)))


def _softplus(x):
    return jnp.maximum(x, 0.0) + jnp.log1p(jnp.exp(-jnp.abs(x)))


def _gelu_tanh(x):
    return 0.5 * x * (1.0 + jnp.tanh(math.sqrt(2.0 / math.pi) * (x + 0.044715 * (x * x * x))))


def _layer_norm(y, g, b):
    mu = jnp.mean(y, axis=-1, keepdims=True)
    yc = y - mu
    var = jnp.mean(yc * yc, axis=-1, keepdims=True)
    return yc * lax.rsqrt(var + LN_EPS) * g + b


def _rms_norm(y, w):
    return y * lax.rsqrt(jnp.mean(y * y, axis=-1, keepdims=True) + LN_EPS) * w


def _params(*sem):
    return pltpu.CompilerParams(dimension_semantics=sem, vmem_limit_bytes=VMEM_LIMIT)


def _ffn_body(with_ple, x_ref, wg_ref, wu_ref, wd_ref, g_ref, b_ref, *rest):
    if with_ple:
        p_ref, pg_ref, pp_ref, o_ref, acc_ref = rest
    else:
        o_ref, acc_ref = rest
    x = x_ref[...]
    xb = x.astype(BF16)
    for c in range(D_FF // FF_CHUNK):
        sl = slice(c * FF_CHUNK, (c + 1) * FF_CHUNK)
        gate = _dot(xb, wg_ref[:, sl])
        up = _dot(xb, wu_ref[:, sl])
        h = (_silu(gate) * up).astype(BF16)
        d = _dot(h, wd_ref[sl, :])
        if c == 0:
            acc_ref[...] = d
        else:
            acc_ref[...] += d
    y = _layer_norm(DN_ALPHA * x + 0.5 * acc_ref[...], g_ref[...], b_ref[...])
    if with_ple:
        gate = _sigmoid(_dot(y.astype(BF16), pg_ref[...]))
        y = y + gate * _dot(p_ref[...].astype(BF16), pp_ref[...])
    o_ref[...] = y


def _ffn_ln(x, wg, wu, wd, g, b, ple=None):
    t = x.shape[0]
    row = lambda i: (i, 0)
    in_specs = [pl.BlockSpec((TM, D_MODEL), row), _resident(wg.shape), _resident(wu.shape),
                _resident(wd.shape), _resident(g.shape), _resident(b.shape)]
    args = [x, wg, wu, wd, g, b]
    if ple is not None:
        p, pg, pp = ple
        in_specs += [pl.BlockSpec((TM, PLE_DIM), row), _resident(pg.shape), _resident(pp.shape)]
        args += [p, pg, pp]
    return pl.pallas_call(
        functools.partial(_ffn_body, ple is not None),
        grid=(t // TM,),
        in_specs=in_specs,
        out_specs=pl.BlockSpec((TM, D_MODEL), row),
        out_shape=jax.ShapeDtypeStruct((t, D_MODEL), F32),
        scratch_shapes=[pltpu.VMEM((TM, D_MODEL), F32)],
        compiler_params=_params("parallel"),
        name="ffn_ln_ple" if ple is not None else "ffn_ln",
    )(*args)


def _proj_in_body(n_out, x_ref, w_ref, o_ref):
    xb = x_ref[...].astype(BF16)
    step = 512
    for c0 in range(0, n_out, step):
        c1 = min(c0 + step, n_out)
        o_ref[:, c0:c1] = _dot(xb, w_ref[:, c0:c1])


def _proj_in(x, w):
    t, n_out = x.shape[0], w.shape[1]
    return pl.pallas_call(
        functools.partial(_proj_in_body, n_out),
        grid=(t // TM,),
        in_specs=[pl.BlockSpec((TM, D_MODEL), lambda i: (i, 0)), _resident(w.shape)],
        out_specs=pl.BlockSpec((TM, n_out), lambda i: (i, 0)),
        out_shape=jax.ShapeDtypeStruct((t, n_out), F32),
        compiler_params=_params("parallel"),
        name="proj_in",
    )(x, w)


def _proj_ln_body(widths, x_ref, *rest):
    a_refs = rest[:len(widths)]
    w_ref, g_ref, b_ref, o_ref = rest[len(widths):]
    mix, off = None, 0
    for a_ref, k in zip(a_refs, widths):
        d = _dot(a_ref[...], w_ref[off:off + k, :])
        mix = d if mix is None else mix + d
        off += k
    o_ref[...] = _layer_norm(DN_ALPHA * x_ref[...] + mix, g_ref[...], b_ref[...])


def _proj_ln(x, acts, w, g, b):
    t = x.shape[0]
    row = lambda i: (i, 0)
    widths = tuple(a.shape[1] for a in acts)
    in_specs = [pl.BlockSpec((TM, D_MODEL), row)]
    in_specs += [pl.BlockSpec((TM, k), row) for k in widths]
    in_specs += [_resident(w.shape), _resident(g.shape), _resident(b.shape)]
    return pl.pallas_call(
        functools.partial(_proj_ln_body, widths),
        grid=(t // TM,),
        in_specs=in_specs,
        out_specs=pl.BlockSpec((TM, D_MODEL), row),
        out_shape=jax.ShapeDtypeStruct((t, D_MODEL), F32),
        compiler_params=_params("parallel"),
        name="proj_ln",
    )(x, *acts, w, g, b)


def _ssd_body(xbc_ref, z_ref, dt_ref, convw_ref, convb_ref, dtb_ref, alog_ref, dskip_ref,
              normw_ref, tril_ref, expand_ref, o_ref, xpad_ref, state_ref):
    ch = SSD_CHUNK
    gw = SSD_GROUP_WIDTH

    @pl.when(pl.program_id(1) == 0)
    def _():
        xpad_ref[0:SUBLANES, :] = jnp.zeros((SUBLANES, SSD_CONV_DIM), F32)
        state_ref[...] = jnp.zeros(state_ref.shape, F32)

    cur = xbc_ref[...]
    xpad_ref[SUBLANES:SUBLANES + ch, :] = cur
    acc = cur * convw_ref[SSD_CONV - 1:SSD_CONV, :] + convb_ref[...]
    for j in range(1, SSD_CONV):
        acc = acc + xpad_ref[pl.ds(SUBLANES - j, ch), :] * convw_ref[SSD_CONV - 1 - j:SSD_CONV - j, :]
    xpad_ref[0:SUBLANES, :] = cur[ch - SUBLANES:ch, :]
    xc = _silu(acc)
    xs = xc[:, :SSD_INNER]
    bm = xc[:, SSD_INNER:SSD_INNER + SSD_BC].astype(BF16)
    cm = xc[:, SSD_INNER + SSD_BC:].astype(BF16)

    dt = _softplus(dt_ref[...] + dtb_ref[...])
    da = dt * (-jnp.exp(alog_ref[...]))
    a_cum = _sel_dot(tril_ref[...], da)
    a_cum_t = a_cum.T
    dt_e = _dot_sel(dt, expand_ref[...])
    ac_e = _dot_sel(a_cum, expand_ref[...])
    ac_last = ac_e[ch - 1:ch, :]
    xdt = xs * dt_e
    xdt_b = xdt.astype(BF16)
    xdt_end = (xdt * jnp.exp(ac_last - ac_e)).astype(BF16)
    carry_scale = jnp.exp(ac_e)
    chunk_decay = jnp.exp(ac_last)

    row = lax.broadcasted_iota(jnp.int32, (ch, ch), 0)
    col = lax.broadcasted_iota(jnp.int32, (ch, ch), 1)
    causal = col <= row
    lane_head = lax.broadcasted_iota(jnp.int32, (ch, gw), 1) >> int(math.log2(SSD_HEAD_DIM))

    for g in range(SSD_GROUPS):
        gs = slice(g * gw, (g + 1) * gw)
        ns = slice(g * SSD_STATE, (g + 1) * SSD_STATE)
        b_g = bm[:, ns]
        c_g = cm[:, ns]
        scores = _dot_nt(c_g, b_g)
        x_g = xdt_b[:, gs]
        y_g = jnp.zeros((ch, gw), F32)
        for hh in range(SSD_HEADS_PER_GROUP):
            h = g * SSD_HEADS_PER_GROUP + hh
            seg = a_cum[:, h:h + 1] - a_cum_t[h:h + 1, :]
            decay = jnp.where(causal, jnp.exp(jnp.minimum(seg, 0.0)), 0.0)
            full = _dot((scores * decay).astype(BF16), x_g)
            y_g = jnp.where(lane_head == hh, full, y_g)
        state = state_ref[g]
        y_g = y_g + _dot(c_g, state.astype(BF16)) * carry_scale[:, gs]
        state_ref[g] = state * chunk_decay[:, gs] + _dot_tn(b_g, xdt_end[:, gs])
        y_g = y_g + xs[:, gs] * dskip_ref[:, gs]
        y_g = y_g * _silu(z_ref[:, gs])
        o_ref[:, gs] = _rms_norm(y_g, normw_ref[:, gs]).astype(o_ref.dtype)


def _ssd_mixer(h_ab, batch, seq, conv_w, conv_b, dt_bias, a_log, d_skip, norm_w):
    ch = SSD_CHUNK
    nc = seq // ch
    tok = lambda cb: (lambda b, c: (b * nc + c, cb))
    pad_heads = lambda v: jnp.zeros((1, LANES), F32).at[0, :SSD_HEADS].set(v.astype(F32))
    tril = jnp.asarray(np.tril(np.ones((ch, ch), np.float32)), BF16)
    expand = np.zeros((LANES, SSD_INNER), np.float32)
    for h in range(SSD_HEADS):
        expand[h, h * SSD_HEAD_DIM:(h + 1) * SSD_HEAD_DIM] = 1.0
    expand = jnp.asarray(expand, BF16)
    dskip_e = jnp.repeat(d_skip.astype(F32), SSD_HEAD_DIM)[None, :]
    consts = [conv_w, conv_b[None, :], pad_heads(dt_bias), pad_heads(a_log), dskip_e,
              norm_w[None, :], tril, expand]
    in_specs = [
        pl.BlockSpec((ch, SSD_CONV_DIM), tok(AB_XBC // SSD_CONV_DIM)),
        pl.BlockSpec((ch, SSD_INNER), tok(AB_Z // SSD_INNER)),
        pl.BlockSpec((ch, LANES), tok(AB_DT // LANES)),
    ] + [_resident(a.shape) for a in consts]
    return pl.pallas_call(
        _ssd_body,
        grid=(batch, nc),
        in_specs=in_specs,
        out_specs=pl.BlockSpec((ch, SSD_INNER), tok(0)),
        out_shape=jax.ShapeDtypeStruct((batch * seq, SSD_INNER), BF16),
        scratch_shapes=[pltpu.VMEM((SUBLANES + ch, SSD_CONV_DIM), F32),
                        pltpu.VMEM((SSD_GROUPS, SSD_STATE, SSD_GROUP_WIDTH), F32)],
        compiler_params=_params("parallel", "arbitrary"),
        name="ssd_mixer",
    )(h_ab, h_ab, h_ab, *consts)


def _s5_slot_tokens():
    k = np.arange(S5_Q)
    g8 = np.arange(S5_TILE_GROUPS)[:, None]
    return S5_TILE_GROUPS * (k // S5_TILE_GROUPS)[None, :] + ((k % S5_TILE_GROUPS)[None, :] - g8) % S5_TILE_GROUPS


def _s5_body(nblk, u_ref, m_ref, wst_ref, woff_ref, a1_ref, a2_ref, o_ref, uscr_ref, yscr_ref):
    ng = S5_TILE_GROUPS
    for c in range(nblk):
        uscr_ref[S5_PITCH * c:S5_PITCH * c + S5_Q, :] = u_ref[S5_Q * c:S5_Q * (c + 1), :]
    slot = lax.broadcasted_iota(jnp.int32, (nblk, LANES), 1) >> int(math.log2(S5_GROUP))
    blk = lax.broadcasted_iota(jnp.int32, (nblk, 2 * S5_STATE), 0)
    rolled = []
    for s in range(S5_Q):
        us = uscr_ref[pl.ds(s, nblk, stride=S5_PITCH), :]
        sh = S5_GROUP * (s % ng)
        rolled.append(pltpu.roll(us, sh, axis=1) if sh else us)
    ys = []
    for g in range(ng):
        cols = []
        for m in range(S5_Q // ng):
            v = rolled[ng * m]
            for s in range(1, ng):
                v = jnp.where(slot == (g + s) % ng, rolled[ng * m + s], v)
            cols.append(v.astype(BF16))
        u = jnp.concatenate(cols, axis=1)
        y = _dot(u, m_ref[g])
        x = _dot(u, wst_ref[g])
        a1 = a1_ref[g]
        a2 = a2_ref[g]
        for k in range(int(math.log2(nblk))):
            sh = 1 << k
            prev = jnp.where(blk >= sh, pltpu.roll(x, sh, axis=0), 0.0)
            x = x + prev * a1[k:k + 1, :] + pltpu.roll(prev, S5_STATE, axis=1) * a2[k:k + 1, :]
        x_in = jnp.where(blk >= 1, pltpu.roll(x, 1, axis=0), 0.0)
        ys.append(y + _dot(x_in.astype(BF16), woff_ref[g]))
    for t in range(S5_Q):
        m, tt = divmod(t, ng)
        z = ys[0][:, LANES * m:LANES * (m + 1)]
        for g in range(1, ng):
            z = jnp.where(slot == (g + tt) % ng, ys[g][:, LANES * m:LANES * (m + 1)], z)
        sh = (LANES - S5_GROUP * tt) % LANES
        yscr_ref[pl.ds(t, nblk, stride=S5_PITCH), :] = pltpu.roll(z, sh, axis=1) if sh else z
    for c in range(nblk):
        o_ref[S5_Q * c:S5_Q * (c + 1), :] = yscr_ref[S5_PITCH * c:S5_PITCH * c + S5_Q, :]


def _s5_tables(lam_re, lam_im, log_dt, b_re, b_im, c_re, c_im, nblk):
    q, ng, nt = S5_Q, S5_TILE_GROUPS, S5_GROUPS // S5_TILE_GROUPS
    hp = lax.Precision.HIGHEST
    lam = lax.complex(lam_re.astype(F32), lam_im.astype(F32))
    ldt = lam * jnp.exp(log_dt.astype(F32))[:, None]
    lam_bar = jnp.exp(ldt)
    b_bar = ((lam_bar - 1.0) / lam)[..., None] * lax.complex(b_re.astype(F32), b_im.astype(F32))
    c = lax.complex(c_re.astype(F32), c_im.astype(F32))
    tau = jnp.arange(q + 1, dtype=F32)
    pw = jnp.exp(ldt[:, None, :] * tau[None, :, None])
    kern = jnp.real(jnp.einsum('gpn,gtn,gnq->gtpq', c, pw[:, :q], b_bar, precision=hp))
    kern = jnp.concatenate([kern, jnp.zeros_like(kern[:, :1])], axis=1)
    tok = _s5_slot_tokens()
    g8 = np.arange(ng)
    src, dst = tok[:, :, None], tok[:, None, :]
    lag = np.where(dst >= src, dst - src, q) + (q + 1) * g8[:, None, None]
    m = kern.reshape(nt, ng * (q + 1), S5_GROUP, S5_GROUP)[:, lag]
    m = jnp.transpose(m, (0, 1, 2, 5, 3, 4)).reshape(S5_GROUPS, S5_ROW, S5_ROW)
    pw_t = pw.reshape(nt, ng, q + 1, S5_STATE)
    st = pw_t[:, g8[:, None], q - 1 - tok][:, :, :, None, :] * jnp.transpose(b_bar, (0, 2, 1)).reshape(
        nt, ng, 1, S5_GROUP, S5_STATE)
    st = st.reshape(S5_GROUPS, S5_ROW, S5_STATE)
    wst = jnp.concatenate([jnp.real(st), jnp.imag(st)], axis=-1)
    off = jnp.transpose(c, (0, 2, 1)).reshape(nt, ng, S5_STATE, 1, S5_GROUP) * jnp.transpose(
        pw_t[:, g8[:, None], tok + 1], (0, 1, 3, 2))[..., None]
    off = off.reshape(S5_GROUPS, S5_STATE, S5_ROW)
    woff = jnp.concatenate([jnp.real(off), -jnp.imag(off)], axis=1)
    nlev = int(math.log2(nblk))
    step = jnp.exp(ldt[:, None, :] * (q * 2.0 ** jnp.arange(nlev, dtype=F32))[None, :, None])
    a1 = jnp.concatenate([jnp.real(step), jnp.real(step)], axis=-1)
    a2 = jnp.concatenate([-jnp.imag(step), jnp.imag(step)], axis=-1)
    return m.astype(BF16), wst.astype(BF16), woff.astype(BF16), a1, a2


def _s5_core(h_ab, batch, seq, tables):
    m, wst, woff, a1, a2 = tables
    nblk = seq // S5_Q
    ng = S5_TILE_GROUPS
    grp = lambda a: pl.BlockSpec((ng,) + a.shape[1:], lambda j, b: (j, 0, 0))
    return pl.pallas_call(
        functools.partial(_s5_body, nblk),
        grid=(S5_GROUPS // ng, batch),
        in_specs=[pl.BlockSpec((seq, LANES), lambda j, b: (b, AB_U // LANES + j))]
        + [grp(a) for a in tables],
        out_specs=pl.BlockSpec((seq, LANES), lambda j, b: (b, j)),
        out_shape=jax.ShapeDtypeStruct((batch * seq, S5_WIDTH), F32),
        scratch_shapes=[pltpu.VMEM((nblk * S5_PITCH, LANES), F32), pltpu.VMEM((nblk * S5_PITCH, LANES), F32)],
        compiler_params=_params("parallel", "parallel"),
        name="s5_core",
    )(h_ab, m, wst, woff, a1, a2)


def _s5_post_body(yc_ref, u_ref, d_ref, w_ref, b_ref, o_ref):
    y = _gelu_tanh(yc_ref[...] + d_ref[...] * u_ref[...])
    gate = _sigmoid(_dot(y.astype(BF16), w_ref[...]) + b_ref[...])
    o_ref[...] = (y * gate).astype(o_ref.dtype)


def _s5_post(y_conv, h_ab, d_skip, w_glu, b_glu):
    t = y_conv.shape[0]
    consts = [d_skip, w_glu, b_glu]
    return pl.pallas_call(
        _s5_post_body,
        grid=(t // TM,),
        in_specs=[pl.BlockSpec((TM, S5_WIDTH), lambda i: (i, 0)),
                  pl.BlockSpec((TM, S5_WIDTH), lambda i: (i, AB_U // S5_WIDTH))]
        + [_resident(a.shape) for a in consts],
        out_specs=pl.BlockSpec((TM, S5_WIDTH), lambda i: (i, 0)),
        out_shape=jax.ShapeDtypeStruct((t, S5_WIDTH), BF16),
        compiler_params=_params("parallel"),
        name="s5_post",
    )(y_conv, h_ab, *consts)


def _lin_levels(c):
    return [c >> (i + 1) for i in range(int(math.log2(c)))]


def _lin_tables(c):
    t = np.arange(c)[:, None]
    j = np.arange(c)[None, :]
    blocks = [(j <= t), (j > t)]
    lvl = np.full((c, c), -1, np.int32)
    lvl[np.arange(c), np.arange(c)] = 0
    for i, b in enumerate(_lin_levels(c)):
        mid = (t // (2 * b)) * (2 * b) + b - 1
        second = (t % (2 * b)) >= b
        blocks.append(np.where(second, (j > mid) & (j <= t), (j > t) & (j <= mid)))
        same = (t // (2 * b)) == (j // (2 * b))
        pair = same & second & ((j % (2 * b)) < b)
        lvl[pair] = i + 1
    sel = np.concatenate(blocks, axis=0).astype(np.float32)
    return jnp.asarray(sel, BF16), jnp.asarray(lvl)


def _lin_body(h_ref, gqk_ref, gv_ref, gr_ref, glr_ref, loglb_ref, log1mlb_ref, onemlb_ref, hnorm_ref,
              wup_ref, bgate_ref, gnorm_ref, sel_ref, lvl_ref, o_ref, state_ref):
    c = LIN_CHUNK
    levels = _lin_levels(c)

    @pl.when(pl.program_id(1) == 0)
    def _():
        state_ref[...] = jnp.zeros(state_ref.shape, F32)

    hq = h_ref[:, 0:HGRN_WIDTH]
    hf = h_ref[:, HGRN_WIDTH:2 * HGRN_WIDTH]
    lo = loglb_ref[...]
    hi_ = log1mlb_ref[...] + _log_sigmoid(hf)
    log_f = jnp.maximum(lo, hi_) + jnp.log1p(jnp.exp(-jnp.abs(lo - hi_)))
    q_c = _silu(hq)
    k_c = onemlb_ref[...] * _sigmoid(-hf)
    pre = _dot(glr_ref[...].astype(BF16), wup_ref[...]) + bgate_ref[...]
    log_a = _log_sigmoid(pre) * (1.0 / GLA_TAU)
    q_d = gqk_ref[:, 0:GLA_QK] * (GLA_DK ** -0.5)
    k_d = gqk_ref[:, GLA_QK:2 * GLA_QK]

    lvl = lvl_ref[...]
    lane = lax.broadcasted_iota(jnp.int32, (c, LANES), 1)
    sel = sel_ref[...]

    def head(idx, q, k, e_all, v, norm_w, gate):
        attn = jnp.where(lvl == 0, _dot_nt(q.astype(BF16), k.astype(BF16)), 0.0)
        for i in range(len(levels)):
            e = e_all[(2 + i) * c:(3 + i) * c, :]
            a = _dot_nt((q * e).astype(BF16), (k * e).astype(BF16))
            attn = jnp.where(lvl == i + 1, a, attn)
        vb = v.astype(BF16)
        state_t = state_ref[idx]
        o = _dot(attn.astype(BF16), vb) + _dot_nt((q * e_all[0:c, :]).astype(BF16), state_t.astype(BF16))
        k_end = (k * e_all[c:2 * c, :]).astype(BF16)
        state_ref[idx] = state_t * e_all[c - 1:c, :] + _dot_tn(vb, k_end)
        return (_rms_norm(o, norm_w) * _silu(gate)).astype(o_ref.dtype)

    for hd in range(LIN_HEADS):
        ls = slice(hd * HEAD_W, (hd + 1) * HEAD_W)
        e_all = jnp.exp(_sel_dot(sel, log_f[:, ls]))
        o_ref[:, ls] = head(hd, q_c[:, ls], k_c[:, ls], e_all,
                            h_ref[:, 2 * HGRN_WIDTH + hd * HEAD_W:2 * HGRN_WIDTH + (hd + 1) * HEAD_W],
                            hnorm_ref[:, ls],
                            h_ref[:, 3 * HGRN_WIDTH + hd * HEAD_W:3 * HGRN_WIDTH + (hd + 1) * HEAD_W])
    for tile in range(GLA_QK // LANES):
        ts = slice(tile * LANES, (tile + 1) * LANES)
        e_all = jnp.exp(_sel_dot(sel, log_a[:, ts]))
        for half in range(LANES // GLA_DK):
            hd = tile * (LANES // GLA_DK) + half
            mine = (lane >> int(math.log2(GLA_DK))) == half
            ls = slice(hd * HEAD_W, (hd + 1) * HEAD_W)
            o_ref[:, HGRN_WIDTH + hd * HEAD_W:HGRN_WIDTH + (hd + 1) * HEAD_W] = head(
                LIN_HEADS + hd, jnp.where(mine, q_d[:, ts], 0.0), jnp.where(mine, k_d[:, ts], 0.0),
                e_all, gv_ref[:, ls], gnorm_ref[:, ls], gr_ref[:, ls])


def _lin_mixer(h_cd, batch, seq, lb, hgrn_norm_w, gla_w_gate_up, gla_b_gate, gla_norm_w):
    c = LIN_CHUNK
    nc = seq // c
    tok = lambda cb: (lambda b, i: (b * nc + i, cb))
    lb = lb.astype(F32)[None, :]
    wup = jnp.zeros((LANES, GLA_QK), F32).at[:GLA_RANK].set(gla_w_gate_up).astype(BF16)
    sel, lvl = _lin_tables(c)
    consts = [jnp.log(lb), jnp.log1p(-lb), 1.0 - lb, hgrn_norm_w[None, :], wup, gla_b_gate[None, :],
              gla_norm_w[None, :], sel, lvl]
    in_specs = [
        pl.BlockSpec((c, 4 * HGRN_WIDTH), tok(CD_H // (4 * HGRN_WIDTH))),
        pl.BlockSpec((c, 2 * GLA_QK), tok(CD_GQK // (2 * GLA_QK))),
        pl.BlockSpec((c, GLA_WIDTH), tok(CD_GV // GLA_WIDTH)),
        pl.BlockSpec((c, GLA_WIDTH), tok(CD_GR // GLA_WIDTH)),
        pl.BlockSpec((c, LANES), tok(CD_GLR // LANES)),
    ] + [_resident(a.shape) for a in consts]
    return pl.pallas_call(
        _lin_body,
        grid=(batch, nc),
        in_specs=in_specs,
        out_specs=pl.BlockSpec((c, HGRN_WIDTH + GLA_WIDTH), tok(0)),
        out_shape=jax.ShapeDtypeStruct((batch * seq, HGRN_WIDTH + GLA_WIDTH), BF16),
        scratch_shapes=[pltpu.VMEM((2 * LIN_HEADS, HEAD_W, HEAD_W), F32)],
        compiler_params=_params("parallel", "arbitrary"),
        name="lin_mixer",
    )(h_cd, h_cd, h_cd, h_cd, h_cd, *consts)


def _pack_cols(w, splits, order, total):
    bounds = np.concatenate([[0], np.cumsum(splits)])
    parts = [w[:, bounds[f]:bounds[f + 1]] for f in order]
    parts.append(jnp.zeros((w.shape[0], total - sum(splits[f] for f in order)), w.dtype))
    return jnp.concatenate(parts, axis=1).astype(BF16)


def _hgrn_lower_bound(lb_logits, layer):
    cum = jnp.cumsum(jax.nn.softmax(lb_logits.astype(F32), axis=0), axis=0)
    return cum[layer] - cum[0]


def kernel(x, p, ln_g, ln_b, ffn_w_gate, ffn_w_up, ffn_w_down, ple_w_gate, ple_w_proj, ab_w_in, ab_w_out,
           ssd_conv_w, ssd_conv_b, ssd_dt_bias, ssd_a_log, ssd_d, ssd_norm_w, s5_lambda_re, s5_lambda_im,
           s5_log_dt, s5_b_re, s5_b_im, s5_c_re, s5_c_im, s5_d, s5_w_glu, s5_b_glu, cd_w_in, cd_w_out,
           hgrn_lb_logits, hgrn_norm_w, gla_w_gate_up, gla_b_gate, gla_norm_w):
    batch, seq, _ = x.shape
    t = batch * seq
    x = x.reshape(t, D_MODEL)
    bf = lambda w: w.astype(BF16)
    ab_splits = (SSD_INNER, SSD_CONV_DIM, SSD_HEADS, S5_WIDTH)
    cd_splits = (HGRN_WIDTH, HGRN_WIDTH, HGRN_WIDTH, HGRN_WIDTH, GLA_QK, GLA_QK, GLA_WIDTH, GLA_RANK,
                 GLA_WIDTH)
    for i in range(DEPTH):
        j = i // 2
        ln = lambda k: (ln_g[i, k][None, :], ln_b[i, k][None, :])
        x = _ffn_ln(x, bf(ffn_w_gate[i, 0]), bf(ffn_w_up[i, 0]), bf(ffn_w_down[i, 0]), *ln(0))
        if i % 2 == 0:
            w_in = _pack_cols(ab_w_in[j], ab_splits, [1, 0, 3, 2], AB_PACK)
            h_ab = _proj_in(x, w_in)
            y_a = _ssd_mixer(h_ab, batch, seq, ssd_conv_w[j], ssd_conv_b[j], ssd_dt_bias[j], ssd_a_log[j],
                             ssd_d[j], ssd_norm_w[j])
            tables = _s5_tables(s5_lambda_re[j], s5_lambda_im[j], s5_log_dt[j], s5_b_re[j], s5_b_im[j],
                                s5_c_re[j], s5_c_im[j], seq // S5_Q)
            y_conv = _s5_core(h_ab, batch, seq, tables)
            y_b = _s5_post(y_conv, h_ab, s5_d[j].reshape(1, S5_WIDTH).astype(F32), bf(s5_w_glu[j]),
                           s5_b_glu[j][None, :])
            x = _proj_ln(x, [y_a, y_b], bf(ab_w_out[j]), *ln(1))
        else:
            w_in = _pack_cols(cd_w_in[j], cd_splits, [0, 1, 2, 3, 4, 5, 6, 8, 7], CD_PACK)
            h_cd = _proj_in(x, w_in)
            o_cd = _lin_mixer(h_cd, batch, seq, _hgrn_lower_bound(hgrn_lb_logits, i), hgrn_norm_w[j],
                              gla_w_gate_up[j], gla_b_gate[j], gla_norm_w[j])
            x = _proj_ln(x, [o_cd], bf(cd_w_out[j]), *ln(1))
        x = _ffn_ln(x, bf(ffn_w_gate[i, 1]), bf(ffn_w_up[i, 1]), bf(ffn_w_down[i, 1]), *ln(2),
                    ple=(p[i].reshape(t, PLE_DIM), bf(ple_w_gate[i]), bf(ple_w_proj[i])))
    return x.reshape(batch, seq, D_MODEL)
```

```python
import functools
import math

import jax
import jax.numpy as jnp
import numpy as np
from jax import lax
from jax.experimental import pallas as pl
from jax.experimental.pallas import tpu as pltpu

F32 = jnp.float32
BF16 = jnp.bfloat16

D_MODEL = 1024
D_FF = 2816
PLE_DIM = 256
DEPTH = 2
DN_ALPHA = (2.0 * DEPTH) ** 0.25
LN_EPS = 1e-5
SSD_HEADS = 16
SSD_HEAD_DIM = 64
SSD_GROUPS = 4
SSD_STATE = 128
SSD_CONV = 4
SSD_INNER = 1024
SSD_BC = SSD_GROUPS * SSD_STATE
SSD_CONV_DIM = SSD_INNER + 2 * SSD_BC
SSD_GROUP_WIDTH = SSD_INNER // SSD_GROUPS
SSD_HEADS_PER_GROUP = SSD_HEADS // SSD_GROUPS
S5_WIDTH = 1024
S5_GROUPS = 64
S5_GROUP = 16
S5_STATE = 64
LIN_HEADS = 4
HGRN_WIDTH = 512
GLA_DK = 64
GLA_QK = LIN_HEADS * GLA_DK
GLA_WIDTH = 512
GLA_RANK = 16
GLA_TAU = 16.0
HEAD_W = 128

LANES = 128
SUBLANES = 8
VMEM_LIMIT = 56 * 1024 * 1024

TM = 512
FF_CHUNK = 256
SSD_CHUNK = 128
LIN_CHUNK = 128
S5_Q = 32
S5_ROW = S5_Q * S5_GROUP
S5_TILE_GROUPS = LANES // S5_GROUP
S5_PITCH = 40

AB_XBC, AB_Z, AB_U, AB_DT, AB_PACK = 0, 2048, 3072, 4096, 4224
CD_H, CD_GQK, CD_GV, CD_GR, CD_GLR, CD_PACK = 0, 2048, 2560, 3072, 3584, 3712


def _resident(shape):
    n = len(shape)
    return pl.BlockSpec(shape, lambda *_: (0,) * n)


def _dot(a, b):
    return jnp.dot(a, b, preferred_element_type=F32)


def _dot_nt(a, b):
    return lax.dot_general(a, b, (((1,), (1,)), ((), ())), preferred_element_type=F32)


def _dot_tn(a, b):
    return lax.dot_general(a, b, (((0,), (0,)), ((), ())), preferred_element_type=F32)


def _split3(v):
    hi = v.astype(BF16)
    r = v - hi.astype(F32)
    mid = r.astype(BF16)
    lo = (r - mid.astype(F32)).astype(BF16)
    return hi, mid, lo


def _sel_dot(sel, v):
    hi, mid, lo = _split3(v)
    return _dot(sel, hi) + _dot(sel, mid) + _dot(sel, lo)


def _dot_sel(v, sel):
    hi, mid, lo = _split3(v)
    return _dot(hi, sel) + _dot(mid, sel) + _dot(lo, sel)


def _sigmoid(x):
    return 1.0 / (1.0 + jnp.exp(-x))


def _silu(x):
    return x * _sigmoid(x)


def _log_sigmoid(x):
    return jnp.minimum(x, 0.0) - jnp.log1p(jnp.exp(-jnp.abs(x)))


def _softplus(x):
    return jnp.maximum(x, 0.0) + jnp.log1p(jnp.exp(-jnp.abs(x)))


def _gelu_tanh(x):
    return 0.5 * x * (1.0 + jnp.tanh(math.sqrt(2.0 / math.pi) * (x + 0.044715 * (x * x * x))))


def _layer_norm(y, g, b):
    mu = jnp.mean(y, axis=-1, keepdims=True)
    yc = y - mu
    var = jnp.mean(yc * yc, axis=-1, keepdims=True)
    return yc * lax.rsqrt(var + LN_EPS) * g + b


def _rms_norm(y, w):
    return y * lax.rsqrt(jnp.mean(y * y, axis=-1, keepdims=True) + LN_EPS) * w


def _params(*sem):
    return pltpu.CompilerParams(dimension_semantics=sem, vmem_limit_bytes=VMEM_LIMIT)


def _ffn_body(with_ple, x_ref, wg_ref, wu_ref, wd_ref, g_ref, b_ref, *rest):
    if with_ple:
        p_ref, pg_ref, pp_ref, o_ref, acc_ref = rest
    else:
        o_ref, acc_ref = rest
    x = x_ref[...]
    xb = x.astype(BF16)
    for c in range(D_FF // FF_CHUNK):
        sl = slice(c * FF_CHUNK, (c + 1) * FF_CHUNK)
        gate = _dot(xb, wg_ref[:, sl])
        up = _dot(xb, wu_ref[:, sl])
        h = (_silu(gate) * up).astype(BF16)
        d = _dot(h, wd_ref[sl, :])
        if c == 0:
            acc_ref[...] = d
        else:
            acc_ref[...] += d
    y = _layer_norm(DN_ALPHA * x + 0.5 * acc_ref[...], g_ref[...], b_ref[...])
    if with_ple:
        gate = _sigmoid(_dot(y.astype(BF16), pg_ref[...]))
        y = y + gate * _dot(p_ref[...].astype(BF16), pp_ref[...])
    o_ref[...] = y


def _ffn_ln(x, wg, wu, wd, g, b, ple=None):
    t = x.shape[0]
    row = lambda i: (i, 0)
    in_specs = [pl.BlockSpec((TM, D_MODEL), row), _resident(wg.shape), _resident(wu.shape),
                _resident(wd.shape), _resident(g.shape), _resident(b.shape)]
    args = [x, wg, wu, wd, g, b]
    if ple is not None:
        p, pg, pp = ple
        in_specs += [pl.BlockSpec((TM, PLE_DIM), row), _resident(pg.shape), _resident(pp.shape)]
        args += [p, pg, pp]
    return pl.pallas_call(
        functools.partial(_ffn_body, ple is not None),
        grid=(t // TM,),
        in_specs=in_specs,
        out_specs=pl.BlockSpec((TM, D_MODEL), row),
        out_shape=jax.ShapeDtypeStruct((t, D_MODEL), F32),
        scratch_shapes=[pltpu.VMEM((TM, D_MODEL), F32)],
        compiler_params=_params("parallel"),
        name="ffn_ln_ple" if ple is not None else "ffn_ln",
    )(*args)


def _proj_in_body(n_out, x_ref, w_ref, o_ref):
    xb = x_ref[...].astype(BF16)
    step = 512
    for c0 in range(0, n_out, step):
        c1 = min(c0 + step, n_out)
        o_ref[:, c0:c1] = _dot(xb, w_ref[:, c0:c1])


def _proj_in(x, w):
    t, n_out = x.shape[0], w.shape[1]
    return pl.pallas_call(
        functools.partial(_proj_in_body, n_out),
        grid=(t // TM,),
        in_specs=[pl.BlockSpec((TM, D_MODEL), lambda i: (i, 0)), _resident(w.shape)],
        out_specs=pl.BlockSpec((TM, n_out), lambda i: (i, 0)),
        out_shape=jax.ShapeDtypeStruct((t, n_out), F32),
        compiler_params=_params("parallel"),
        name="proj_in",
    )(x, w)


def _proj_ln_body(widths, x_ref, *rest):
    a_refs = rest[:len(widths)]
    w_ref, g_ref, b_ref, o_ref = rest[len(widths):]
    mix, off = None, 0
    for a_ref, k in zip(a_refs, widths):
        d = _dot(a_ref[...], w_ref[off:off + k, :])
        mix = d if mix is None else mix + d
        off += k
    o_ref[...] = _layer_norm(DN_ALPHA * x_ref[...] + mix, g_ref[...], b_ref[...])


def _proj_ln(x, acts, w, g, b):
    t = x.shape[0]
    row = lambda i: (i, 0)
    widths = tuple(a.shape[1] for a in acts)
    in_specs = [pl.BlockSpec((TM, D_MODEL), row)]
    in_specs += [pl.BlockSpec((TM, k), row) for k in widths]
    in_specs += [_resident(w.shape), _resident(g.shape), _resident(b.shape)]
    return pl.pallas_call(
        functools.partial(_proj_ln_body, widths),
        grid=(t // TM,),
        in_specs=in_specs,
        out_specs=pl.BlockSpec((TM, D_MODEL), row),
        out_shape=jax.ShapeDtypeStruct((t, D_MODEL), F32),
        compiler_params=_params("parallel"),
        name="proj_ln",
    )(x, *acts, w, g, b)


def _ssd_body(xbc_ref, z_ref, dt_ref, convw_ref, convb_ref, dtb_ref, alog_ref, dskip_ref,
              normw_ref, tril_ref, expand_ref, o_ref, xpad_ref, state_ref):
    ch = SSD_CHUNK
    gw = SSD_GROUP_WIDTH

    @pl.when(pl.program_id(1) == 0)
    def _():
        xpad_ref[0:SUBLANES, :] = jnp.zeros((SUBLANES, SSD_CONV_DIM), F32)
        state_ref[...] = jnp.zeros(state_ref.shape, F32)

    cur = xbc_ref[...]
    xpad_ref[SUBLANES:SUBLANES + ch, :] = cur
    acc = cur * convw_ref[SSD_CONV - 1:SSD_CONV, :] + convb_ref[...]
    for j in range(1, SSD_CONV):
        acc = acc + xpad_ref[pl.ds(SUBLANES - j, ch), :] * convw_ref[SSD_CONV - 1 - j:SSD_CONV - j, :]
    xpad_ref[0:SUBLANES, :] = cur[ch - SUBLANES:ch, :]
    xc = _silu(acc)
    xs = xc[:, :SSD_INNER]
    bm = xc[:, SSD_INNER:SSD_INNER + SSD_BC].astype(BF16)
    cm = xc[:, SSD_INNER + SSD_BC:].astype(BF16)

    dt = _softplus(dt_ref[...] + dtb_ref[...])
    da = dt * (-jnp.exp(alog_ref[...]))
    a_cum = _sel_dot(tril_ref[...], da)
    a_cum_t = a_cum.T
    dt_e = _dot_sel(dt, expand_ref[...])
    ac_e = _dot_sel(a_cum, expand_ref[...])
    ac_last = ac_e[ch - 1:ch, :]
    xdt = xs * dt_e
    xdt_b = xdt.astype(BF16)
    xdt_end = (xdt * jnp.exp(ac_last - ac_e)).astype(BF16)
    carry_scale = jnp.exp(ac_e)
    chunk_decay = jnp.exp(ac_last)

    row = lax.broadcasted_iota(jnp.int32, (ch, ch), 0)
    col = lax.broadcasted_iota(jnp.int32, (ch, ch), 1)
    causal = col <= row
    lane_head = lax.broadcasted_iota(jnp.int32, (ch, gw), 1) >> int(math.log2(SSD_HEAD_DIM))

    for g in range(SSD_GROUPS):
        gs = slice(g * gw, (g + 1) * gw)
        ns = slice(g * SSD_STATE, (g + 1) * SSD_STATE)
        b_g = bm[:, ns]
        c_g = cm[:, ns]
        scores = _dot_nt(c_g, b_g)
        x_g = xdt_b[:, gs]
        y_g = jnp.zeros((ch, gw), F32)
        for hh in range(SSD_HEADS_PER_GROUP):
            h = g * SSD_HEADS_PER_GROUP + hh
            seg = a_cum[:, h:h + 1] - a_cum_t[h:h + 1, :]
            decay = jnp.where(causal, jnp.exp(jnp.minimum(seg, 0.0)), 0.0)
            full = _dot((scores * decay).astype(BF16), x_g)
            y_g = jnp.where(lane_head == hh, full, y_g)
        state = state_ref[g]
        y_g = y_g + _dot(c_g, state.astype(BF16)) * carry_scale[:, gs]
        state_ref[g] = state * chunk_decay[:, gs] + _dot_tn(b_g, xdt_end[:, gs])
        y_g = y_g + xs[:, gs] * dskip_ref[:, gs]
        y_g = y_g * _silu(z_ref[:, gs])
        o_ref[:, gs] = _rms_norm(y_g, normw_ref[:, gs]).astype(o_ref.dtype)


def _ssd_mixer(h_ab, batch, seq, conv_w, conv_b, dt_bias, a_log, d_skip, norm_w):
    ch = SSD_CHUNK
    nc = seq // ch
    tok = lambda cb: (lambda b, c: (b * nc + c, cb))
    pad_heads = lambda v: jnp.zeros((1, LANES), F32).at[0, :SSD_HEADS].set(v.astype(F32))
    tril = jnp.asarray(np.tril(np.ones((ch, ch), np.float32)), BF16)
    expand = np.zeros((LANES, SSD_INNER), np.float32)
    for h in range(SSD_HEADS):
        expand[h, h * SSD_HEAD_DIM:(h + 1) * SSD_HEAD_DIM] = 1.0
    expand = jnp.asarray(expand, BF16)
    dskip_e = jnp.repeat(d_skip.astype(F32), SSD_HEAD_DIM)[None, :]
    consts = [conv_w, conv_b[None, :], pad_heads(dt_bias), pad_heads(a_log), dskip_e,
              norm_w[None, :], tril, expand]
    in_specs = [
        pl.BlockSpec((ch, SSD_CONV_DIM), tok(AB_XBC // SSD_CONV_DIM)),
        pl.BlockSpec((ch, SSD_INNER), tok(AB_Z // SSD_INNER)),
        pl.BlockSpec((ch, LANES), tok(AB_DT // LANES)),
    ] + [_resident(a.shape) for a in consts]
    return pl.pallas_call(
        _ssd_body,
        grid=(batch, nc),
        in_specs=in_specs,
        out_specs=pl.BlockSpec((ch, SSD_INNER), tok(0)),
        out_shape=jax.ShapeDtypeStruct((batch * seq, SSD_INNER), BF16),
        scratch_shapes=[pltpu.VMEM((SUBLANES + ch, SSD_CONV_DIM), F32),
                        pltpu.VMEM((SSD_GROUPS, SSD_STATE, SSD_GROUP_WIDTH), F32)],
        compiler_params=_params("parallel", "arbitrary"),
        name="ssd_mixer",
    )(h_ab, h_ab, h_ab, *consts)


def _s5_slot(g, s):
    return (s + g) % S5_TILE_GROUPS


def _s5_body(nblk, u_ref, m_ref, wst_ref, woff_ref, a1_ref, a2_ref, o_ref, uscr_ref, yscr_ref, ug_ref, yg_ref):
    ng = S5_TILE_GROUPS
    ncol = S5_Q // ng
    nstrip = nblk // SUBLANES
    slot = lax.broadcasted_iota(jnp.int32, (SUBLANES, LANES), 1) >> int(math.log2(S5_GROUP))
    blk = lax.broadcasted_iota(jnp.int32, (nblk, 2 * S5_STATE), 0)

    def pitch_in(c, carry):
        src = pl.multiple_of(c * S5_Q, S5_Q)
        dst = pl.multiple_of(c * S5_PITCH, SUBLANES)
        uscr_ref[pl.ds(dst, S5_Q), :] = u_ref[pl.ds(src, S5_Q), :]
        return carry
    lax.fori_loop(0, nblk, pitch_in, 0, unroll=8)

    def gather(i, carry):
        base = pl.multiple_of(i * (SUBLANES * S5_PITCH), SUBLANES)
        row = pl.multiple_of(i * SUBLANES, SUBLANES)
        rolled = []
        for s in range(S5_Q):
            us = uscr_ref[pl.ds(base + s, SUBLANES, stride=S5_PITCH), :]
            sh = S5_GROUP * (s % ng)
            rolled.append(pltpu.roll(us, sh, axis=1) if sh else us)
        for g in range(ng):
            for m in range(ncol):
                v = rolled[ng * m]
                for s in range(1, ng):
                    v = jnp.where(slot == _s5_slot(g, s), rolled[ng * m + s], v)
                ug_ref[pl.ds(row, SUBLANES), (g * ncol + m) * LANES:(g * ncol + m + 1) * LANES] = v
        return carry
    lax.fori_loop(0, nstrip, gather, 0)

    for g in range(ng):
        u = ug_ref[:, g * S5_ROW:(g + 1) * S5_ROW].astype(BF16)
        y = _dot(u, m_ref[g])
        x = _dot(u, wst_ref[g])
        a1 = a1_ref[g]
        a2 = a2_ref[g]
        for k in range(int(math.log2(nblk))):
            sh = 1 << k
            prev = jnp.where(blk >= sh, pltpu.roll(x, sh, axis=0), 0.0)
            x = x + prev * a1[k:k + 1, :] + pltpu.roll(prev, S5_STATE, axis=1) * a2[k:k + 1, :]
        x_in = jnp.where(blk >= 1, pltpu.roll(x, 1, axis=0), 0.0)
        yg_ref[:, g * S5_ROW:(g + 1) * S5_ROW] = y + _dot(x_in.astype(BF16), woff_ref[g])

    def scatter(i, carry):
        base = pl.multiple_of(i * (SUBLANES * S5_PITCH), SUBLANES)
        row = pl.multiple_of(i * SUBLANES, SUBLANES)
        for m in range(ncol):
            cols = [yg_ref[pl.ds(row, SUBLANES), (g * ncol + m) * LANES:(g * ncol + m + 1) * LANES]
                    for g in range(ng)]
            for tt in range(ng):
                z = cols[0]
                for g in range(1, ng):
                    z = jnp.where(slot == _s5_slot(g, tt), cols[g], z)
                sh = (LANES - S5_GROUP * tt) % LANES
                yscr_ref[pl.ds(base + ng * m + tt, SUBLANES, stride=S5_PITCH), :] = (
                    pltpu.roll(z, sh, axis=1) if sh else z)
        return carry
    lax.fori_loop(0, nstrip, scatter, 0)

    def pitch_out(c, carry):
        src = pl.multiple_of(c * S5_PITCH, SUBLANES)
        dst = pl.multiple_of(c * S5_Q, S5_Q)
        o_ref[pl.ds(dst, S5_Q), :] = yscr_ref[pl.ds(src, S5_Q), :]
        return carry
    lax.fori_loop(0, nblk, pitch_out, 0, unroll=8)


def _s5_core(h_ab, batch, seq, tables):
    m, wst, woff, a1, a2 = tables
    nblk = seq // S5_Q
    ng = S5_TILE_GROUPS
    grp = lambda a: pl.BlockSpec((ng,) + a.shape[1:], lambda j, b: (j, 0, 0))
    return pl.pallas_call(
        functools.partial(_s5_body, nblk),
        grid=(S5_GROUPS // ng, batch),
        in_specs=[pl.BlockSpec((seq, LANES), lambda j, b: (b, AB_U // LANES + j))]
        + [grp(a) for a in tables],
        out_specs=pl.BlockSpec((seq, LANES), lambda j, b: (b, j)),
        out_shape=jax.ShapeDtypeStruct((batch * seq, S5_WIDTH), F32),
        scratch_shapes=[pltpu.VMEM((nblk * S5_PITCH, LANES), F32), pltpu.VMEM((nblk * S5_PITCH, LANES), F32),
                        pltpu.VMEM((nblk, ng * S5_ROW), F32), pltpu.VMEM((nblk, ng * S5_ROW), F32)],
        compiler_params=_params("parallel", "parallel"),
        name="s5_core",
    )(h_ab, m, wst, woff, a1, a2)


def _s5_table_body(kt_ref, st_ref, off_ref, m_ref, wst_ref, woff_ref):
    ng = S5_TILE_GROUPS
    ncol = S5_Q // ng
    lane = lax.broadcasted_iota(jnp.int32, (S5_GROUP, LANES), 1)
    for g in range(ng):
        strip = [jnp.zeros((S5_GROUP, LANES), F32)] * ncol + [kt_ref[g, :, c * LANES:(c + 1) * LANES]
                                                            for c in range(ncol)]
        for s in range(S5_Q):
            start = ncol * LANES - S5_GROUP * s
            a, sh = start // LANES, start % LANES
            k = ng * (s // ng) + _s5_slot(g, s % ng)
            for c in range(ncol):
                if sh:
                    w = jnp.where(lane < LANES - sh, pltpu.roll(strip[a + c], LANES - sh, axis=1),
                                  pltpu.roll(strip[a + c + 1], LANES - sh, axis=1))
                else:
                    w = strip[a + c]
                if g:
                    w = pltpu.roll(w, S5_GROUP * g, axis=1)
                m_ref[g, S5_GROUP * k:S5_GROUP * (k + 1), c * LANES:(c + 1) * LANES] = w.astype(BF16)
            wst_ref[g, S5_GROUP * k:S5_GROUP * (k + 1), :] = st_ref[g, S5_GROUP * s:S5_GROUP * (s + 1), :].astype(BF16)
        for c in range(ncol):
            w = off_ref[g, :, c * LANES:(c + 1) * LANES]
            if g:
                w = pltpu.roll(w, S5_GROUP * g, axis=1)
            woff_ref[g, :, c * LANES:(c + 1) * LANES] = w.astype(BF16)


def _s5_tables(lam_re, lam_im, log_dt, b_re, b_im, c_re, c_im, nblk):
    q, ng = S5_Q, S5_TILE_GROUPS
    hp = lax.Precision.HIGHEST
    lam = lax.complex(lam_re.astype(F32), lam_im.astype(F32))
    ldt = lam * jnp.exp(log_dt.astype(F32))[:, None]
    lam_bar = jnp.exp(ldt)
    b_bar = ((lam_bar - 1.0) / lam)[..., None] * lax.complex(b_re.astype(F32), b_im.astype(F32))
    c = lax.complex(c_re.astype(F32), c_im.astype(F32))
    tau = jnp.arange(q + 1, dtype=F32)
    pw = jnp.exp(ldt[:, None, :] * tau[None, :, None])
    kt = jnp.real(jnp.einsum('gpn,gtn,gnq->gqtp', c, pw[:, :q], b_bar, precision=hp)).reshape(
        S5_GROUPS, S5_GROUP, S5_ROW)
    st = pw[:, q - 1 - np.arange(q), :][:, :, None, :] * jnp.transpose(b_bar, (0, 2, 1))[:, None]
    st = st.reshape(S5_GROUPS, S5_ROW, S5_STATE)
    st = jnp.concatenate([jnp.real(st), jnp.imag(st)], axis=-1)
    off = jnp.transpose(c, (0, 2, 1))[:, :, None, :] * jnp.transpose(pw[:, 1:], (0, 2, 1))[..., None]
    off = off.reshape(S5_GROUPS, S5_STATE, S5_ROW)
    off = jnp.concatenate([jnp.real(off), -jnp.imag(off)], axis=1)
    grp = lambda shape: pl.BlockSpec((ng,) + shape, lambda j: (j, 0, 0))
    shapes = [(S5_ROW, S5_ROW), (S5_ROW, 2 * S5_STATE), (2 * S5_STATE, S5_ROW)]
    m, wst, woff = pl.pallas_call(
        _s5_table_body,
        grid=(S5_GROUPS // ng,),
        in_specs=[grp((S5_GROUP, S5_ROW)), grp(shapes[1]), grp(shapes[2])],
        out_specs=[grp(sh) for sh in shapes],
        out_shape=[jax.ShapeDtypeStruct((S5_GROUPS,) + sh, BF16) for sh in shapes],
        compiler_params=_params("parallel"),
        name="s5_tables",
    )(kt, st, off)
    nlev = int(math.log2(nblk))
    step = jnp.exp(ldt[:, None, :] * (q * 2.0 ** jnp.arange(nlev, dtype=F32))[None, :, None])
    a1 = jnp.concatenate([jnp.real(step), jnp.real(step)], axis=-1)
    a2 = jnp.concatenate([-jnp.imag(step), jnp.imag(step)], axis=-1)
    return m, wst, woff, a1, a2


def _s5_post_body(yc_ref, u_ref, d_ref, w_ref, b_ref, o_ref):
    y = _gelu_tanh(yc_ref[...] + d_ref[...] * u_ref[...])
    gate = _sigmoid(_dot(y.astype(BF16), w_ref[...]) + b_ref[...])
    o_ref[...] = (y * gate).astype(o_ref.dtype)


def _s5_post(y_conv, h_ab, d_skip, w_glu, b_glu):
    t = y_conv.shape[0]
    consts = [d_skip, w_glu, b_glu]
    return pl.pallas_call(
        _s5_post_body,
        grid=(t // TM,),
        in_specs=[pl.BlockSpec((TM, S5_WIDTH), lambda i: (i, 0)),
                  pl.BlockSpec((TM, S5_WIDTH), lambda i: (i, AB_U // S5_WIDTH))]
        + [_resident(a.shape) for a in consts],
        out_specs=pl.BlockSpec((TM, S5_WIDTH), lambda i: (i, 0)),
        out_shape=jax.ShapeDtypeStruct((t, S5_WIDTH), BF16),
        compiler_params=_params("parallel"),
        name="s5_post",
    )(y_conv, h_ab, *consts)


def _lin_levels(c):
    return [c >> (i + 1) for i in range(int(math.log2(c)))]


def _lin_tables(c):
    t = np.arange(c)[:, None]
    j = np.arange(c)[None, :]
    lvl = np.full((c, c), -1, np.int32)
    lvl[np.arange(c), np.arange(c)] = 0
    for i, b in enumerate(_lin_levels(c)):
        same = (t // (2 * b)) == (j // (2 * b))
        pair = same & ((t % (2 * b)) >= b) & ((j % (2 * b)) < b)
        lvl[pair] = i + 1
    return jnp.asarray((j <= t).astype(np.float32), BF16), jnp.asarray(lvl)


def _gate_factors(g, tril, gcum_ref):
    c = LIN_CHUNK
    gcum = _sel_dot(tril, g)
    gcum_ref[...] = gcum
    pos = lax.broadcasted_iota(jnp.int32, (c, LANES), 0)
    factors = []
    for b in _lin_levels(c):
        if 2 * b >= SUBLANES:
            mids = [jnp.broadcast_to(gcum_ref[blk * 2 * b + b - 1:blk * 2 * b + b, :], (2 * b, LANES))
                    for blk in range(c // (2 * b))]
            gmid = mids[0] if len(mids) == 1 else jnp.concatenate(mids, axis=0)
        elif b == 2:
            p4 = pos & 3
            gmid = jnp.where(p4 == 0, pltpu.roll(gcum, c - 1, axis=0),
                             jnp.where(p4 == 1, gcum,
                                       jnp.where(p4 == 2, pltpu.roll(gcum, 1, axis=0),
                                                 pltpu.roll(gcum, 2, axis=0))))
        else:
            gmid = jnp.where((pos & 1) == 1, pltpu.roll(gcum, 1, axis=0), gcum)
        factors.append(jnp.exp(-jnp.abs(gcum - gmid)))
    return jnp.exp(gcum), jnp.exp(gcum[c - 1:c, :] - gcum), factors


def _lin_body(h_ref, gqk_ref, gv_ref, gr_ref, glr_ref, loglb_ref, log1mlb_ref, onemlb_ref, hnorm_ref,
              wup_ref, bgate_ref, gnorm_ref, tril_ref, lvl_ref, o_ref, state_ref, gcum_ref):
    c = LIN_CHUNK

    @pl.when(pl.program_id(1) == 0)
    def _():
        state_ref[...] = jnp.zeros(state_ref.shape, F32)

    hq = h_ref[:, 0:HGRN_WIDTH]
    hf = h_ref[:, HGRN_WIDTH:2 * HGRN_WIDTH]
    lo = loglb_ref[...]
    hi_ = log1mlb_ref[...] + _log_sigmoid(hf)
    log_f = jnp.maximum(lo, hi_) + jnp.log1p(jnp.exp(-jnp.abs(lo - hi_)))
    q_c = _silu(hq)
    k_c = onemlb_ref[...] * _sigmoid(-hf)
    pre = _dot(glr_ref[...].astype(BF16), wup_ref[...]) + bgate_ref[...]
    log_a = _log_sigmoid(pre) * (1.0 / GLA_TAU)
    q_d = gqk_ref[:, 0:GLA_QK] * (GLA_DK ** -0.5)
    k_d = gqk_ref[:, GLA_QK:2 * GLA_QK]

    lvl = lvl_ref[...]
    lane = lax.broadcasted_iota(jnp.int32, (c, LANES), 1)
    tril = tril_ref[...]

    def head(idx, q, k, factors, v, norm_w, gate):
        e_cum, e_end, e_lvl = factors
        attn = jnp.where(lvl == 0, _dot_nt(q.astype(BF16), k.astype(BF16)), 0.0)
        for i, e in enumerate(e_lvl):
            a = _dot_nt((q * e).astype(BF16), (k * e).astype(BF16))
            attn = jnp.where(lvl == i + 1, a, attn)
        vb = v.astype(BF16)
        state_t = state_ref[idx]
        o = _dot(attn.astype(BF16), vb) + _dot_nt((q * e_cum).astype(BF16), state_t.astype(BF16))
        state_ref[idx] = state_t * e_cum[c - 1:c, :] + _dot_tn(vb, (k * e_end).astype(BF16))
        return (_rms_norm(o, norm_w) * _silu(gate)).astype(o_ref.dtype)

    for hd in range(LIN_HEADS):
        ls = slice(hd * HEAD_W, (hd + 1) * HEAD_W)
        factors = _gate_factors(log_f[:, ls], tril, gcum_ref.at[hd])
        o_ref[:, ls] = head(hd, q_c[:, ls], k_c[:, ls], factors,
                            h_ref[:, 2 * HGRN_WIDTH + hd * HEAD_W:2 * HGRN_WIDTH + (hd + 1) * HEAD_W],
                            hnorm_ref[:, ls],
                            h_ref[:, 3 * HGRN_WIDTH + hd * HEAD_W:3 * HGRN_WIDTH + (hd + 1) * HEAD_W])
    for tile in range(GLA_QK // LANES):
        ts = slice(tile * LANES, (tile + 1) * LANES)
        factors = _gate_factors(log_a[:, ts], tril, gcum_ref.at[LIN_HEADS + tile])
        for half in range(LANES // GLA_DK):
            hd = tile * (LANES // GLA_DK) + half
            mine = (lane >> int(math.log2(GLA_DK))) == half
            ls = slice(hd * HEAD_W, (hd + 1) * HEAD_W)
            o_ref[:, HGRN_WIDTH + hd * HEAD_W:HGRN_WIDTH + (hd + 1) * HEAD_W] = head(
                LIN_HEADS + hd, jnp.where(mine, q_d[:, ts], 0.0), jnp.where(mine, k_d[:, ts], 0.0),
                factors, gv_ref[:, ls], gnorm_ref[:, ls], gr_ref[:, ls])


def _lin_mixer(h_cd, batch, seq, lb, hgrn_norm_w, gla_w_gate_up, gla_b_gate, gla_norm_w):
    c = LIN_CHUNK
    nc = seq // c
    tok = lambda cb: (lambda b, i: (b * nc + i, cb))
    lb = lb.astype(F32)[None, :]
    wup = jnp.zeros((LANES, GLA_QK), F32).at[:GLA_RANK].set(gla_w_gate_up).astype(BF16)
    tril, lvl = _lin_tables(c)
    consts = [jnp.log(lb), jnp.log1p(-lb), 1.0 - lb, hgrn_norm_w[None, :], wup, gla_b_gate[None, :],
              gla_norm_w[None, :], tril, lvl]
    in_specs = [
        pl.BlockSpec((c, 4 * HGRN_WIDTH), tok(CD_H // (4 * HGRN_WIDTH))),
        pl.BlockSpec((c, 2 * GLA_QK), tok(CD_GQK // (2 * GLA_QK))),
        pl.BlockSpec((c, GLA_WIDTH), tok(CD_GV // GLA_WIDTH)),
        pl.BlockSpec((c, GLA_WIDTH), tok(CD_GR // GLA_WIDTH)),
        pl.BlockSpec((c, LANES), tok(CD_GLR // LANES)),
    ] + [_resident(a.shape) for a in consts]
    return pl.pallas_call(
        _lin_body,
        grid=(batch, nc),
        in_specs=in_specs,
        out_specs=pl.BlockSpec((c, HGRN_WIDTH + GLA_WIDTH), tok(0)),
        out_shape=jax.ShapeDtypeStruct((batch * seq, HGRN_WIDTH + GLA_WIDTH), BF16),
        scratch_shapes=[pltpu.VMEM((2 * LIN_HEADS, HEAD_W, HEAD_W), F32),
                        pltpu.VMEM((LIN_HEADS + GLA_QK // LANES, c, LANES), F32)],
        compiler_params=_params("parallel", "arbitrary"),
        name="lin_mixer",
    )(h_cd, h_cd, h_cd, h_cd, h_cd, *consts)


def _pack_cols(w, splits, order, total):
    bounds = np.concatenate([[0], np.cumsum(splits)])
    parts = [w[:, bounds[f]:bounds[f + 1]] for f in order]
    parts.append(jnp.zeros((w.shape[0], total - sum(splits[f] for f in order)), w.dtype))
    return jnp.concatenate(parts, axis=1).astype(BF16)


def _hgrn_lower_bound(lb_logits, layer):
    cum = jnp.cumsum(jax.nn.softmax(lb_logits.astype(F32), axis=0), axis=0)
    return cum[layer] - cum[0]


def kernel(x, p, ln_g, ln_b, ffn_w_gate, ffn_w_up, ffn_w_down, ple_w_gate, ple_w_proj, ab_w_in, ab_w_out,
           ssd_conv_w, ssd_conv_b, ssd_dt_bias, ssd_a_log, ssd_d, ssd_norm_w, s5_lambda_re, s5_lambda_im,
           s5_log_dt, s5_b_re, s5_b_im, s5_c_re, s5_c_im, s5_d, s5_w_glu, s5_b_glu, cd_w_in, cd_w_out,
           hgrn_lb_logits, hgrn_norm_w, gla_w_gate_up, gla_b_gate, gla_norm_w):
    batch, seq, _ = x.shape
    t = batch * seq
    x = x.reshape(t, D_MODEL)
    bf = lambda w: w.astype(BF16)
    ab_splits = (SSD_INNER, SSD_CONV_DIM, SSD_HEADS, S5_WIDTH)
    cd_splits = (HGRN_WIDTH, HGRN_WIDTH, HGRN_WIDTH, HGRN_WIDTH, GLA_QK, GLA_QK, GLA_WIDTH, GLA_RANK,
                 GLA_WIDTH)
    for i in range(DEPTH):
        j = i // 2
        ln = lambda k: (ln_g[i, k][None, :], ln_b[i, k][None, :])
        x = _ffn_ln(x, bf(ffn_w_gate[i, 0]), bf(ffn_w_up[i, 0]), bf(ffn_w_down[i, 0]), *ln(0))
        if i % 2 == 0:
            w_in = _pack_cols(ab_w_in[j], ab_splits, [1, 0, 3, 2], AB_PACK)
            h_ab = _proj_in(x, w_in)
            y_a = _ssd_mixer(h_ab, batch, seq, ssd_conv_w[j], ssd_conv_b[j], ssd_dt_bias[j], ssd_a_log[j],
                             ssd_d[j], ssd_norm_w[j])
            tables = _s5_tables(s5_lambda_re[j], s5_lambda_im[j], s5_log_dt[j], s5_b_re[j], s5_b_im[j],
                                s5_c_re[j], s5_c_im[j], seq // S5_Q)
            y_conv = _s5_core(h_ab, batch, seq, tables)
            y_b = _s5_post(y_conv, h_ab, s5_d[j].reshape(1, S5_WIDTH).astype(F32), bf(s5_w_glu[j]),
                           s5_b_glu[j][None, :])
            x = _proj_ln(x, [y_a, y_b], bf(ab_w_out[j]), *ln(1))
        else:
            w_in = _pack_cols(cd_w_in[j], cd_splits, [0, 1, 2, 3, 4, 5, 6, 8, 7], CD_PACK)
            h_cd = _proj_in(x, w_in)
            o_cd = _lin_mixer(h_cd, batch, seq, _hgrn_lower_bound(hgrn_lb_logits, i), hgrn_norm_w[j],
                              gla_w_gate_up[j], gla_b_gate[j], gla_norm_w[j])
            x = _proj_ln(x, [o_cd], bf(cd_w_out[j]), *ln(1))
        x = _ffn_ln(x, bf(ffn_w_gate[i, 1]), bf(ffn_w_up[i, 1]), bf(ffn_w_down[i, 1]), *ln(2),
                    ple=(p[i].reshape(t, PLE_DIM), bf(ple_w_gate[i]), bf(ple_w_proj[i])))
    return x.reshape(batch, seq, D_MODEL)
```

```python
import functools
import math

import jax
import jax.numpy as jnp
import numpy as np
from jax import lax
from jax.experimental import pallas as pl
from jax.experimental.pallas import tpu as pltpu

F32 = jnp.float32
BF16 = jnp.bfloat16

D_MODEL = 1024
D_FF = 2816
PLE_DIM = 256
DEPTH = 2
DN_ALPHA = (2.0 * DEPTH) ** 0.25
LN_EPS = 1e-5
SSD_HEADS = 16
SSD_HEAD_DIM = 64
SSD_GROUPS = 4
SSD_STATE = 128
SSD_CONV = 4
SSD_INNER = 1024
SSD_BC = SSD_GROUPS * SSD_STATE
SSD_CONV_DIM = SSD_INNER + 2 * SSD_BC
SSD_GROUP_WIDTH = SSD_INNER // SSD_GROUPS
SSD_HEADS_PER_GROUP = SSD_HEADS // SSD_GROUPS
S5_WIDTH = 1024
S5_GROUPS = 64
S5_GROUP = 16
S5_STATE = 64
LIN_HEADS = 4
HGRN_WIDTH = 512
GLA_DK = 64
GLA_QK = LIN_HEADS * GLA_DK
GLA_WIDTH = 512
GLA_RANK = 16
GLA_TAU = 16.0
HEAD_W = 128

LANES = 128
SUBLANES = 8
VMEM_LIMIT = 56 * 1024 * 1024

TM = 512
FF_CHUNK = 256
SSD_CHUNK = 128
LIN_CHUNK = 128
S5_Q = 32
S5_ROW = S5_Q * S5_GROUP
S5_TILE_GROUPS = LANES // S5_GROUP
S5_PITCH = 40

AB_XBC, AB_Z, AB_U, AB_DT, AB_PACK = 0, 2048, 3072, 4096, 4224
CD_H, CD_GQK, CD_GV, CD_GR, CD_GLR, CD_PACK = 0, 2048, 2560, 3072, 3584, 3712


def _resident(shape):
    n = len(shape)
    return pl.BlockSpec(shape, lambda *_: (0,) * n)


def _dot(a, b):
    return jnp.dot(a, b, preferred_element_type=F32)


def _dot_nt(a, b):
    return lax.dot_general(a, b, (((1,), (1,)), ((), ())), preferred_element_type=F32)


def _dot_tn(a, b):
    return lax.dot_general(a, b, (((0,), (0,)), ((), ())), preferred_element_type=F32)


def _split3(v):
    hi = v.astype(BF16)
    r = v - hi.astype(F32)
    mid = r.astype(BF16)
    lo = (r - mid.astype(F32)).astype(BF16)
    return hi, mid, lo


def _sel_dot(sel, v):
    hi, mid, lo = _split3(v)
    return _dot(sel, hi) + _dot(sel, mid) + _dot(sel, lo)


def _dot_sel(v, sel):
    hi, mid, lo = _split3(v)
    return _dot(hi, sel) + _dot(mid, sel) + _dot(lo, sel)


def _sigmoid(x):
    return 1.0 / (1.0 + jnp.exp(-x))


def _silu(x):
    return x * _sigmoid(x)


def _log_sigmoid(x):
    return jnp.minimum(x, 0.0) - jnp.log1p(jnp.exp(-jnp.abs(x)))


def _softplus(x):
    return jnp.maximum(x, 0.0) + jnp.log1p(jnp.exp(-jnp.abs(x)))


def _gelu_tanh(x):
    return 0.5 * x * (1.0 + jnp.tanh(math.sqrt(2.0 / math.pi) * (x + 0.044715 * (x * x * x))))


def _layer_norm(y, g, b):
    mu = jnp.mean(y, axis=-1, keepdims=True)
    yc = y - mu
    var = jnp.mean(yc * yc, axis=-1, keepdims=True)
    return yc * lax.rsqrt(var + LN_EPS) * g + b


def _rms_norm(y, w):
    return y * lax.rsqrt(jnp.mean(y * y, axis=-1, keepdims=True) + LN_EPS) * w


def _params(*sem):
    return pltpu.CompilerParams(dimension_semantics=sem, vmem_limit_bytes=VMEM_LIMIT)


def _ffn_body(with_ple, x_ref, wg_ref, wu_ref, wd_ref, g_ref, b_ref, *rest):
    if with_ple:
        p_ref, pg_ref, pp_ref, o_ref, acc_ref = rest
    else:
        o_ref, acc_ref = rest
    x = x_ref[...]
    xb = x.astype(BF16)
    for c in range(D_FF // FF_CHUNK):
        sl = slice(c * FF_CHUNK, (c + 1) * FF_CHUNK)
        gate = _dot(xb, wg_ref[:, sl])
        up = _dot(xb, wu_ref[:, sl])
        h = (_silu(gate) * up).astype(BF16)
        d = _dot(h, wd_ref[sl, :])
        if c == 0:
            acc_ref[...] = d
        else:
            acc_ref[...] += d
    y = _layer_norm(DN_ALPHA * x + 0.5 * acc_ref[...], g_ref[...], b_ref[...])
    if with_ple:
        gate = _sigmoid(_dot(y.astype(BF16), pg_ref[...]))
        y = y + gate * _dot(p_ref[...].astype(BF16), pp_ref[...])
    o_ref[...] = y


def _pick(a, *idx):
    rest = a.shape[len(idx):]
    return pl.BlockSpec((None,) * len(idx) + rest, lambda *_: idx + (0,) * len(rest))


def _ffn_ln(x, wg, wu, wd, ln_g, ln_b, layer, pos, ln_idx, ple=None):
    t = x.shape[0]
    row = lambda i: (i, 0)
    in_specs = [pl.BlockSpec((TM, D_MODEL), row), _pick(wg, layer, pos), _pick(wu, layer, pos),
                _pick(wd, layer, pos), _pick(ln_g, ln_idx), _pick(ln_b, ln_idx)]
    args = [x, wg, wu, wd, ln_g, ln_b]
    if ple is not None:
        p, pg, pp = ple
        in_specs += [pl.BlockSpec((None, TM, PLE_DIM), lambda i: (layer, i, 0)), _pick(pg, layer), _pick(pp, layer)]
        args += [p, pg, pp]
    return pl.pallas_call(
        functools.partial(_ffn_body, ple is not None),
        grid=(t // TM,),
        in_specs=in_specs,
        out_specs=pl.BlockSpec((TM, D_MODEL), row),
        out_shape=jax.ShapeDtypeStruct((t, D_MODEL), F32),
        scratch_shapes=[pltpu.VMEM((TM, D_MODEL), F32)],
        compiler_params=_params("parallel"),
        name="ffn_ln_ple" if ple is not None else "ffn_ln",
    )(*args)


def _proj_in_body(n_out, x_ref, w_ref, o_ref):
    xb = x_ref[...].astype(BF16)
    step = 512
    for c0 in range(0, n_out, step):
        c1 = min(c0 + step, n_out)
        o_ref[:, c0:c1] = _dot(xb, w_ref[:, c0:c1])


def _proj_in(x, w):
    t, n_out = x.shape[0], w.shape[1]
    return pl.pallas_call(
        functools.partial(_proj_in_body, n_out),
        grid=(t // TM,),
        in_specs=[pl.BlockSpec((TM, D_MODEL), lambda i: (i, 0)), _resident(w.shape)],
        out_specs=pl.BlockSpec((TM, n_out), lambda i: (i, 0)),
        out_shape=jax.ShapeDtypeStruct((t, n_out), F32),
        compiler_params=_params("parallel"),
        name="proj_in",
    )(x, w)


def _proj_ln_body(widths, x_ref, *rest):
    a_refs = rest[:len(widths)]
    w_ref, g_ref, b_ref, o_ref = rest[len(widths):]
    mix, off = None, 0
    for a_ref, k in zip(a_refs, widths):
        d = _dot(a_ref[...], w_ref[off:off + k, :])
        mix = d if mix is None else mix + d
        off += k
    o_ref[...] = _layer_norm(DN_ALPHA * x_ref[...] + mix, g_ref[...], b_ref[...])


def _proj_ln(x, acts, w, w_idx, ln_g, ln_b, ln_idx):
    t = x.shape[0]
    row = lambda i: (i, 0)
    widths = tuple(a.shape[1] for a in acts)
    in_specs = [pl.BlockSpec((TM, D_MODEL), row)]
    in_specs += [pl.BlockSpec((TM, k), row) for k in widths]
    in_specs += [_pick(w, w_idx), _pick(ln_g, ln_idx), _pick(ln_b, ln_idx)]
    return pl.pallas_call(
        functools.partial(_proj_ln_body, widths),
        grid=(t // TM,),
        in_specs=in_specs,
        out_specs=pl.BlockSpec((TM, D_MODEL), row),
        out_shape=jax.ShapeDtypeStruct((t, D_MODEL), F32),
        compiler_params=_params("parallel"),
        name="proj_ln",
    )(x, *acts, w, ln_g, ln_b)


def _ssd_body(xbc_ref, z_ref, dt_ref, convw_ref, convb_ref, dtb_ref, alog_ref, dskip_ref,
              normw_ref, tril_ref, expand_ref, o_ref, xpad_ref, state_ref):
    ch = SSD_CHUNK
    gw = SSD_GROUP_WIDTH

    @pl.when(pl.program_id(1) == 0)
    def _():
        xpad_ref[0:SUBLANES, :] = jnp.zeros((SUBLANES, SSD_CONV_DIM), F32)
        state_ref[...] = jnp.zeros(state_ref.shape, F32)

    cur = xbc_ref[...]
    xpad_ref[SUBLANES:SUBLANES + ch, :] = cur
    acc = cur * convw_ref[SSD_CONV - 1:SSD_CONV, :] + convb_ref[...]
    for j in range(1, SSD_CONV):
        acc = acc + xpad_ref[pl.ds(SUBLANES - j, ch), :] * convw_ref[SSD_CONV - 1 - j:SSD_CONV - j, :]
    xpad_ref[0:SUBLANES, :] = cur[ch - SUBLANES:ch, :]
    xc = _silu(acc)
    xs = xc[:, :SSD_INNER]
    bm = xc[:, SSD_INNER:SSD_INNER + SSD_BC].astype(BF16)
    cm = xc[:, SSD_INNER + SSD_BC:].astype(BF16)

    dt = _softplus(dt_ref[...] + dtb_ref[...])
    da = dt * (-jnp.exp(alog_ref[...]))
    a_cum = _sel_dot(tril_ref[...], da)
    a_cum_t = a_cum.T
    dt_e = _dot_sel(dt, expand_ref[...])
    ac_e = _dot_sel(a_cum, expand_ref[...])
    ac_last = ac_e[ch - 1:ch, :]
    xdt = xs * dt_e
    xdt_b = xdt.astype(BF16)
    xdt_end = (xdt * jnp.exp(ac_last - ac_e)).astype(BF16)
    carry_scale = jnp.exp(ac_e)
    chunk_decay = jnp.exp(ac_last)

    row = lax.broadcasted_iota(jnp.int32, (ch, ch), 0)
    col = lax.broadcasted_iota(jnp.int32, (ch, ch), 1)
    causal = col <= row
    lane_head = lax.broadcasted_iota(jnp.int32, (ch, gw), 1) >> int(math.log2(SSD_HEAD_DIM))

    for g in range(SSD_GROUPS):
        gs = slice(g * gw, (g + 1) * gw)
        ns = slice(g * SSD_STATE, (g + 1) * SSD_STATE)
        b_g = bm[:, ns]
        c_g = cm[:, ns]
        scores = _dot_nt(c_g, b_g)
        x_g = xdt_b[:, gs]
        y_g = jnp.zeros((ch, gw), F32)
        for hh in range(SSD_HEADS_PER_GROUP):
            h = g * SSD_HEADS_PER_GROUP + hh
            seg = a_cum[:, h:h + 1] - a_cum_t[h:h + 1, :]
            decay = jnp.where(causal, jnp.exp(jnp.minimum(seg, 0.0)), 0.0)
            full = _dot((scores * decay).astype(BF16), x_g)
            y_g = jnp.where(lane_head == hh, full, y_g)
        state = state_ref[g]
        y_g = y_g + _dot(c_g, state.astype(BF16)) * carry_scale[:, gs]
        state_ref[g] = state * chunk_decay[:, gs] + _dot_tn(b_g, xdt_end[:, gs])
        y_g = y_g + xs[:, gs] * dskip_ref[:, gs]
        y_g = y_g * _silu(z_ref[:, gs])
        o_ref[:, gs] = _rms_norm(y_g, normw_ref[:, gs]).astype(o_ref.dtype)


def _ssd_mixer(h_ab, batch, seq, conv_w, conv_b, dt_bias, a_log, d_skip, norm_w):
    ch = SSD_CHUNK
    nc = seq // ch
    tok = lambda cb: (lambda b, c: (b * nc + c, cb))
    pad_heads = lambda v: jnp.zeros((1, LANES), F32).at[0, :SSD_HEADS].set(v.astype(F32))
    tril = jnp.asarray(np.tril(np.ones((ch, ch), np.float32)), BF16)
    expand = np.zeros((LANES, SSD_INNER), np.float32)
    for h in range(SSD_HEADS):
        expand[h, h * SSD_HEAD_DIM:(h + 1) * SSD_HEAD_DIM] = 1.0
    expand = jnp.asarray(expand, BF16)
    dskip_e = jnp.repeat(d_skip.astype(F32), SSD_HEAD_DIM)[None, :]
    consts = [conv_w, conv_b[None, :], pad_heads(dt_bias), pad_heads(a_log), dskip_e,
              norm_w[None, :], tril, expand]
    in_specs = [
        pl.BlockSpec((ch, SSD_CONV_DIM), tok(AB_XBC // SSD_CONV_DIM)),
        pl.BlockSpec((ch, SSD_INNER), tok(AB_Z // SSD_INNER)),
        pl.BlockSpec((ch, LANES), tok(AB_DT // LANES)),
    ] + [_resident(a.shape) for a in consts]
    return pl.pallas_call(
        _ssd_body,
        grid=(batch, nc),
        in_specs=in_specs,
        out_specs=pl.BlockSpec((ch, SSD_INNER), tok(0)),
        out_shape=jax.ShapeDtypeStruct((batch * seq, SSD_INNER), BF16),
        scratch_shapes=[pltpu.VMEM((SUBLANES + ch, SSD_CONV_DIM), F32),
                        pltpu.VMEM((SSD_GROUPS, SSD_STATE, SSD_GROUP_WIDTH), F32)],
        compiler_params=_params("parallel", "arbitrary"),
        name="ssd_mixer",
    )(h_ab, h_ab, h_ab, *consts)


def _s5_slot(g, s):
    return (s + g) % S5_TILE_GROUPS


def _s5_body(nblk, u_ref, m_ref, wst_ref, wofft_ref, a1_ref, a2_ref, o_ref, uscr_ref, yscr_ref, ug_ref, yg_ref):
    ng = S5_TILE_GROUPS
    ncol = S5_Q // ng
    nstrip = nblk // SUBLANES
    slot = lax.broadcasted_iota(jnp.int32, (SUBLANES, LANES), 1) >> int(math.log2(S5_GROUP))
    blk = lax.broadcasted_iota(jnp.int32, (nblk, 2 * S5_STATE), 0)

    def pitch_in(c, carry):
        src = pl.multiple_of(c * S5_Q, S5_Q)
        dst = pl.multiple_of(c * S5_PITCH, SUBLANES)
        uscr_ref[pl.ds(dst, S5_Q), :] = u_ref[pl.ds(src, S5_Q), :]
        return carry
    lax.fori_loop(0, nblk, pitch_in, 0, unroll=8)

    def gather(i, carry):
        base = pl.multiple_of(i * (SUBLANES * S5_PITCH), SUBLANES)
        row = pl.multiple_of(i * SUBLANES, SUBLANES)
        for m in range(ncol):
            rolled = []
            for s in range(ng):
                us = uscr_ref[pl.ds(base + ng * m + s, SUBLANES, stride=S5_PITCH), :]
                rolled.append(pltpu.roll(us, S5_GROUP * s, axis=1) if s else us)
            for g in range(ng):
                v = rolled[0]
                for s in range(1, ng):
                    v = jnp.where(slot == _s5_slot(g, s), rolled[s], v)
                ug_ref[pl.ds(row, SUBLANES), (g * ncol + m) * LANES:(g * ncol + m + 1) * LANES] = v
        return carry
    lax.fori_loop(0, nstrip, gather, 0, unroll=2)

    us = [ug_ref[:, g * S5_ROW:(g + 1) * S5_ROW].astype(BF16) for g in range(ng)]
    xs = [_dot(us[g], wst_ref[g]) for g in range(ng)]
    for k in range(int(math.log2(nblk))):
        sh = 1 << k
        for g in range(ng):
            prev = jnp.where(blk >= sh, pltpu.roll(xs[g], sh, axis=0), 0.0)
            xs[g] = (xs[g] + prev * a1_ref[g, k:k + 1, :]
                     + pltpu.roll(prev, S5_STATE, axis=1) * a2_ref[g, k:k + 1, :])
    for g in range(ng):
        x_in = jnp.where(blk >= 1, pltpu.roll(xs[g], 1, axis=0), 0.0)
        yg_ref[:, g * S5_ROW:(g + 1) * S5_ROW] = (_dot(us[g], m_ref[g])
                                                   + _dot_nt(x_in.astype(BF16), wofft_ref[g]))

    def scatter(i, carry):
        base = pl.multiple_of(i * (SUBLANES * S5_PITCH), SUBLANES)
        row = pl.multiple_of(i * SUBLANES, SUBLANES)
        for m in range(ncol):
            cols = [yg_ref[pl.ds(row, SUBLANES), (g * ncol + m) * LANES:(g * ncol + m + 1) * LANES]
                    for g in range(ng)]
            for tt in range(ng):
                z = cols[0]
                for g in range(1, ng):
                    z = jnp.where(slot == _s5_slot(g, tt), cols[g], z)
                sh = (LANES - S5_GROUP * tt) % LANES
                yscr_ref[pl.ds(base + ng * m + tt, SUBLANES, stride=S5_PITCH), :] = (
                    pltpu.roll(z, sh, axis=1) if sh else z)
        return carry
    lax.fori_loop(0, nstrip, scatter, 0, unroll=2)

    def pitch_out(c, carry):
        src = pl.multiple_of(c * S5_PITCH, SUBLANES)
        dst = pl.multiple_of(c * S5_Q, S5_Q)
        o_ref[pl.ds(dst, S5_Q), :] = yscr_ref[pl.ds(src, S5_Q), :]
        return carry
    lax.fori_loop(0, nblk, pitch_out, 0, unroll=8)


def _s5_core(h_ab, batch, seq, tables):
    nblk = seq // S5_Q
    ng = S5_TILE_GROUPS
    grp = lambda a: pl.BlockSpec((ng,) + a.shape[1:], lambda j, b: (j, 0, 0))
    return pl.pallas_call(
        functools.partial(_s5_body, nblk),
        grid=(S5_GROUPS // ng, batch),
        in_specs=[pl.BlockSpec((seq, LANES), lambda j, b: (b, AB_U // LANES + j))]
        + [grp(a) for a in tables],
        out_specs=pl.BlockSpec((seq, LANES), lambda j, b: (b, j)),
        out_shape=jax.ShapeDtypeStruct((batch * seq, S5_WIDTH), F32),
        scratch_shapes=[pltpu.VMEM((nblk * S5_PITCH, LANES), F32), pltpu.VMEM((nblk * S5_PITCH, LANES), F32),
                        pltpu.VMEM((nblk, ng * S5_ROW), F32), pltpu.VMEM((nblk, ng * S5_ROW), F32)],
        compiler_params=_params("parallel", "parallel"),
        name="s5_core",
    )(h_ab, *tables)


def _s5_table_body(kt_ref, pw_rr_ref, pw_ii_ref, b1_ref, b2_ref, c1_ref, c2_ref, m_ref, wst_ref, wofft_ref):
    ng = S5_TILE_GROUPS
    ncol = S5_Q // ng
    lane = lax.broadcasted_iota(jnp.int32, (S5_GROUP, LANES), 1)
    for g in range(ng):
        strip = [jnp.zeros((S5_GROUP, LANES), F32)] * ncol + [kt_ref[g, :, c * LANES:(c + 1) * LANES]
                                                            for c in range(ncol)]
        b1, b2, c1, c2 = b1_ref[g], b2_ref[g], c1_ref[g], c2_ref[g]
        for s in range(S5_Q):
            start = ncol * LANES - S5_GROUP * s
            a, sh = start // LANES, start % LANES
            k = ng * (s // ng) + _s5_slot(g, s % ng)
            rows = slice(S5_GROUP * k, S5_GROUP * (k + 1))
            for c in range(ncol):
                if sh:
                    w = jnp.where(lane < LANES - sh, pltpu.roll(strip[a + c], LANES - sh, axis=1),
                                  pltpu.roll(strip[a + c + 1], LANES - sh, axis=1))
                else:
                    w = strip[a + c]
                if g:
                    w = pltpu.roll(w, S5_GROUP * g, axis=1)
                m_ref[g, rows, c * LANES:(c + 1) * LANES] = w.astype(BF16)
            e = S5_Q - 1 - s
            wst_ref[g, rows, :] = (pw_rr_ref[g, e:e + 1, :] * b1 + pw_ii_ref[g, e:e + 1, :] * b2).astype(BF16)
            wofft_ref[g, rows, :] = (pw_rr_ref[g, s + 1:s + 2, :] * c1
                                     + pw_ii_ref[g, s + 1:s + 2, :] * c2).astype(BF16)


def _s5_tables(lam_re, lam_im, log_dt, b_re, b_im, c_re, c_im, nblk):
    q, ng = S5_Q, S5_TILE_GROUPS
    hp = lax.Precision.HIGHEST
    lam = lax.complex(lam_re.astype(F32), lam_im.astype(F32))
    ldt = lam * jnp.exp(log_dt.astype(F32))[:, None]
    lam_bar = jnp.exp(ldt)
    b_bar = ((lam_bar - 1.0) / lam)[..., None] * lax.complex(b_re.astype(F32), b_im.astype(F32))
    c = lax.complex(c_re.astype(F32), c_im.astype(F32))
    tau = jnp.arange(q + 1, dtype=F32)
    pw = jnp.exp(ldt[:, None, :] * tau[None, :, None])
    kt = jnp.real(jnp.einsum('gpn,gtn,gnq->gqtp', c, pw[:, :q], b_bar, precision=hp)).reshape(
        S5_GROUPS, S5_GROUP, S5_ROW)
    halves = lambda lo, hi: jnp.concatenate([lo, hi], axis=-1)
    bt = jnp.transpose(b_bar, (0, 2, 1))
    small = [halves(jnp.real(pw), jnp.real(pw)), halves(jnp.imag(pw), jnp.imag(pw)),
             halves(jnp.real(bt), jnp.imag(bt)), halves(-jnp.imag(bt), jnp.real(bt)),
             halves(jnp.real(c), -jnp.imag(c)), halves(-jnp.imag(c), -jnp.real(c))]
    grp = lambda shape: pl.BlockSpec((ng,) + shape, lambda j: (j, 0, 0))
    shapes = [(S5_ROW, S5_ROW), (S5_ROW, 2 * S5_STATE), (S5_ROW, 2 * S5_STATE)]
    m, wst, wofft = pl.pallas_call(
        _s5_table_body,
        grid=(S5_GROUPS // ng,),
        in_specs=[grp(kt.shape[1:])] + [grp(a.shape[1:]) for a in small],
        out_specs=[grp(sh) for sh in shapes],
        out_shape=[jax.ShapeDtypeStruct((S5_GROUPS,) + sh, BF16) for sh in shapes],
        compiler_params=_params("parallel"),
        name="s5_tables",
    )(kt, *small)
    nlev = int(math.log2(nblk))
    step = jnp.exp(ldt[:, None, :] * (q * 2.0 ** jnp.arange(nlev, dtype=F32))[None, :, None])
    a1 = halves(jnp.real(step), jnp.real(step))
    a2 = halves(-jnp.imag(step), jnp.imag(step))
    return m, wst, wofft, a1, a2


def _s5_post_body(yc_ref, u_ref, d_ref, w_ref, b_ref, o_ref):
    y = _gelu_tanh(yc_ref[...] + d_ref[...] * u_ref[...])
    gate = _sigmoid(_dot(y.astype(BF16), w_ref[...]) + b_ref[...])
    o_ref[...] = (y * gate).astype(o_ref.dtype)


def _s5_post(y_conv, h_ab, d_skip, w_glu, b_glu):
    t = y_conv.shape[0]
    consts = [d_skip, w_glu, b_glu]
    return pl.pallas_call(
        _s5_post_body,
        grid=(t // TM,),
        in_specs=[pl.BlockSpec((TM, S5_WIDTH), lambda i: (i, 0)),
                  pl.BlockSpec((TM, S5_WIDTH), lambda i: (i, AB_U // S5_WIDTH))]
        + [_resident(a.shape) for a in consts],
        out_specs=pl.BlockSpec((TM, S5_WIDTH), lambda i: (i, 0)),
        out_shape=jax.ShapeDtypeStruct((t, S5_WIDTH), BF16),
        compiler_params=_params("parallel"),
        name="s5_post",
    )(y_conv, h_ab, *consts)


def _lin_levels(c):
    return [c >> (i + 1) for i in range(int(math.log2(c)))]


def _lin_tables(c):
    t = np.arange(c)[:, None]
    j = np.arange(c)[None, :]
    lvl = np.full((c, c), -1, np.int32)
    lvl[np.arange(c), np.arange(c)] = 0
    sgn = []
    for i, b in enumerate(_lin_levels(c)):
        same = (t // (2 * b)) == (j // (2 * b))
        second = (t % (2 * b)) >= b
        lvl[same & second & ((j % (2 * b)) < b)] = i + 1
        sgn.append(np.broadcast_to(np.where(second, 1.0, -1.0) * math.log2(math.e), (c, LANES)))
    return (jnp.asarray((j <= t).astype(np.float32), BF16), jnp.asarray(lvl),
            jnp.asarray(np.stack(sgn), F32))


def _gate_factors(g, tril, sgn_ref, gcum_ref):
    c = LIN_CHUNK
    gcum = _sel_dot(tril, g)
    gcum_ref[...] = gcum
    pos = lax.broadcasted_iota(jnp.int32, (c, LANES), 0)
    factors = []
    for i, b in enumerate(_lin_levels(c)):
        if 2 * b >= SUBLANES:
            mids = [jnp.broadcast_to(gcum_ref[blk * 2 * b + b - 1:blk * 2 * b + b, :], (2 * b, LANES))
                    for blk in range(c // (2 * b))]
            gmid = mids[0] if len(mids) == 1 else jnp.concatenate(mids, axis=0)
        elif b == 2:
            p4 = pos & 3
            gmid = jnp.where(p4 == 0, pltpu.roll(gcum, c - 1, axis=0),
                             jnp.where(p4 == 1, gcum,
                                       jnp.where(p4 == 2, pltpu.roll(gcum, 1, axis=0),
                                                 pltpu.roll(gcum, 2, axis=0))))
        else:
            gmid = jnp.where((pos & 1) == 1, pltpu.roll(gcum, 1, axis=0), gcum)
        factors.append(jnp.exp2((gcum - gmid) * sgn_ref[i]).astype(BF16))
    e_cum = jnp.exp(gcum)
    return e_cum, e_cum.astype(BF16), jnp.exp(gcum[c - 1:c, :] - gcum).astype(BF16), factors


def _lin_body(h_ref, gqk_ref, gv_ref, gr_ref, glr_ref, loglb_ref, log1mlb_ref, onemlb_ref, hnorm_ref,
              wup_ref, bgate_ref, gnorm_ref, tril_ref, lvl_ref, sgn_ref, o_ref, state_ref, gcum_ref):
    c = LIN_CHUNK

    @pl.when(pl.program_id(1) == 0)
    def _():
        state_ref[...] = jnp.zeros(state_ref.shape, F32)

    hq = h_ref[:, 0:HGRN_WIDTH]
    hf = h_ref[:, HGRN_WIDTH:2 * HGRN_WIDTH]
    lo = loglb_ref[...]
    hi_ = log1mlb_ref[...] + _log_sigmoid(hf)
    log_f = jnp.maximum(lo, hi_) + jnp.log1p(jnp.exp(-jnp.abs(lo - hi_)))
    q_c = _silu(hq)
    k_c = onemlb_ref[...] * _sigmoid(-hf)
    pre = _dot(glr_ref[...].astype(BF16), wup_ref[...]) + bgate_ref[...]
    log_a = _log_sigmoid(pre) * (1.0 / GLA_TAU)
    q_d = gqk_ref[:, 0:GLA_QK] * (GLA_DK ** -0.5)
    k_d = gqk_ref[:, GLA_QK:2 * GLA_QK]

    lvl = lvl_ref[...]
    lane = lax.broadcasted_iota(jnp.int32, (c, LANES), 1)
    tril = tril_ref[...]

    def head(idx, q, k, factors, v, norm_w, gate):
        e_cum, e_cum_b, e_end_b, e_lvl = factors
        qb = q.astype(BF16)
        kb = k.astype(BF16)
        attn = jnp.where(lvl == 0, _dot_nt(qb, kb), 0.0)
        for i, e in enumerate(e_lvl):
            attn = jnp.where(lvl == i + 1, _dot_nt(qb * e, kb * e), attn)
        vb = v.astype(BF16)
        state_t = state_ref[idx]
        o = _dot(attn.astype(BF16), vb) + _dot_nt(qb * e_cum_b, state_t.astype(BF16))
        state_ref[idx] = state_t * e_cum[c - 1:c, :] + _dot_tn(vb, kb * e_end_b)
        return (_rms_norm(o, norm_w) * _silu(gate)).astype(o_ref.dtype)

    for hd in range(LIN_HEADS):
        ls = slice(hd * HEAD_W, (hd + 1) * HEAD_W)
        factors = _gate_factors(log_f[:, ls], tril, sgn_ref, gcum_ref.at[hd])
        o_ref[:, ls] = head(hd, q_c[:, ls], k_c[:, ls], factors,
                            h_ref[:, 2 * HGRN_WIDTH + hd * HEAD_W:2 * HGRN_WIDTH + (hd + 1) * HEAD_W],
                            hnorm_ref[:, ls],
                            h_ref[:, 3 * HGRN_WIDTH + hd * HEAD_W:3 * HGRN_WIDTH + (hd + 1) * HEAD_W])
    for tile in range(GLA_QK // LANES):
        ts = slice(tile * LANES, (tile + 1) * LANES)
        factors = _gate_factors(log_a[:, ts], tril, sgn_ref, gcum_ref.at[LIN_HEADS + tile])
        for half in range(LANES // GLA_DK):
            hd = tile * (LANES // GLA_DK) + half
            mine = (lane >> int(math.log2(GLA_DK))) == half
            ls = slice(hd * HEAD_W, (hd + 1) * HEAD_W)
            o_ref[:, HGRN_WIDTH + hd * HEAD_W:HGRN_WIDTH + (hd + 1) * HEAD_W] = head(
                LIN_HEADS + hd, jnp.where(mine, q_d[:, ts], 0.0), jnp.where(mine, k_d[:, ts], 0.0),
                factors, gv_ref[:, ls], gnorm_ref[:, ls], gr_ref[:, ls])


def _lin_mixer(h_cd, batch, seq, lb, hgrn_norm_w, gla_w_gate_up, gla_b_gate, gla_norm_w):
    c = LIN_CHUNK
    nc = seq // c
    tok = lambda cb: (lambda b, i: (b * nc + i, cb))
    lb = lb.astype(F32)[None, :]
    wup = jnp.zeros((LANES, GLA_QK), F32).at[:GLA_RANK].set(gla_w_gate_up).astype(BF16)
    tril, lvl, sgn = _lin_tables(c)
    consts = [jnp.log(lb), jnp.log1p(-lb), 1.0 - lb, hgrn_norm_w[None, :], wup, gla_b_gate[None, :],
              gla_norm_w[None, :], tril, lvl, sgn]
    in_specs = [
        pl.BlockSpec((c, 4 * HGRN_WIDTH), tok(CD_H // (4 * HGRN_WIDTH))),
        pl.BlockSpec((c, 2 * GLA_QK), tok(CD_GQK // (2 * GLA_QK))),
        pl.BlockSpec((c, GLA_WIDTH), tok(CD_GV // GLA_WIDTH)),
        pl.BlockSpec((c, GLA_WIDTH), tok(CD_GR // GLA_WIDTH)),
        pl.BlockSpec((c, LANES), tok(CD_GLR // LANES)),
    ] + [_resident(a.shape) for a in consts]
    return pl.pallas_call(
        _lin_body,
        grid=(batch, nc),
        in_specs=in_specs,
        out_specs=pl.BlockSpec((c, HGRN_WIDTH + GLA_WIDTH), tok(0)),
        out_shape=jax.ShapeDtypeStruct((batch * seq, HGRN_WIDTH + GLA_WIDTH), BF16),
        scratch_shapes=[pltpu.VMEM((2 * LIN_HEADS, HEAD_W, HEAD_W), F32),
                        pltpu.VMEM((LIN_HEADS + GLA_QK // LANES, c, LANES), F32)],
        compiler_params=_params("parallel", "arbitrary"),
        name="lin_mixer",
    )(h_cd, h_cd, h_cd, h_cd, h_cd, *consts)


def _pack_cols(w, splits, order, total):
    bounds = np.concatenate([[0], np.cumsum(splits)])
    parts = [w[:, bounds[f]:bounds[f + 1]] for f in order]
    parts.append(jnp.zeros((w.shape[0], total - sum(splits[f] for f in order)), w.dtype))
    return jnp.concatenate(parts, axis=1).astype(BF16)


def _hgrn_lower_bound(lb_logits, layer):
    cum = jnp.cumsum(jax.nn.softmax(lb_logits.astype(F32), axis=0), axis=0)
    return cum[layer] - cum[0]


def kernel(x, p, ln_g, ln_b, ffn_w_gate, ffn_w_up, ffn_w_down, ple_w_gate, ple_w_proj, ab_w_in, ab_w_out,
           ssd_conv_w, ssd_conv_b, ssd_dt_bias, ssd_a_log, ssd_d, ssd_norm_w, s5_lambda_re, s5_lambda_im,
           s5_log_dt, s5_b_re, s5_b_im, s5_c_re, s5_c_im, s5_d, s5_w_glu, s5_b_glu, cd_w_in, cd_w_out,
           hgrn_lb_logits, hgrn_norm_w, gla_w_gate_up, gla_b_gate, gla_norm_w):
    batch, seq, _ = x.shape
    t = batch * seq
    x = x.reshape(t, D_MODEL)
    bf = lambda w: w.astype(BF16)
    ab_splits = (SSD_INNER, SSD_CONV_DIM, SSD_HEADS, S5_WIDTH)
    cd_splits = (HGRN_WIDTH, HGRN_WIDTH, HGRN_WIDTH, HGRN_WIDTH, GLA_QK, GLA_QK, GLA_WIDTH, GLA_RANK,
                 GLA_WIDTH)
    wg, wu, wd = bf(ffn_w_gate), bf(ffn_w_up), bf(ffn_w_down)
    pg, pp = bf(ple_w_gate), bf(ple_w_proj)
    p = p.reshape(DEPTH, t, PLE_DIM)
    ln_g = ln_g.reshape(DEPTH * 3, 1, D_MODEL)
    ln_b = ln_b.reshape(DEPTH * 3, 1, D_MODEL)
    w_ab_out, w_cd_out = bf(ab_w_out), bf(cd_w_out)
    for i in range(DEPTH):
        j = i // 2
        x = _ffn_ln(x, wg, wu, wd, ln_g, ln_b, i, 0, 3 * i)
        if i % 2 == 0:
            w_in = _pack_cols(ab_w_in[j], ab_splits, [1, 0, 3, 2], AB_PACK)
            h_ab = _proj_in(x, w_in)
            y_a = _ssd_mixer(h_ab, batch, seq, ssd_conv_w[j], ssd_conv_b[j], ssd_dt_bias[j], ssd_a_log[j],
                             ssd_d[j], ssd_norm_w[j])
            tables = _s5_tables(s5_lambda_re[j], s5_lambda_im[j], s5_log_dt[j], s5_b_re[j], s5_b_im[j],
                                s5_c_re[j], s5_c_im[j], seq // S5_Q)
            y_conv = _s5_core(h_ab, batch, seq, tables)
            y_b = _s5_post(y_conv, h_ab, s5_d[j].reshape(1, S5_WIDTH).astype(F32), bf(s5_w_glu[j]),
                           s5_b_glu[j][None, :])
            x = _proj_ln(x, [y_a, y_b], w_ab_out, j, ln_g, ln_b, 3 * i + 1)
        else:
            w_in = _pack_cols(cd_w_in[j], cd_splits, [0, 1, 2, 3, 4, 5, 6, 8, 7], CD_PACK)
            h_cd = _proj_in(x, w_in)
            o_cd = _lin_mixer(h_cd, batch, seq, _hgrn_lower_bound(hgrn_lb_logits, i), hgrn_norm_w[j],
                              gla_w_gate_up[j], gla_b_gate[j], gla_norm_w[j])
            x = _proj_ln(x, [o_cd], w_cd_out, j, ln_g, ln_b, 3 * i + 1)
        x = _ffn_ln(x, wg, wu, wd, ln_g, ln_b, i, 1, 3 * i + 2, ple=(p, pg, pp))
    return x.reshape(batch, seq, D_MODEL)
```

```python
import functools
import math

import jax
import jax.numpy as jnp
import numpy as np
from jax import lax
from jax.experimental import pallas as pl
from jax.experimental.pallas import tpu as pltpu

F32 = jnp.float32
BF16 = jnp.bfloat16

D_MODEL = 1024
D_FF = 2816
PLE_DIM = 256
DEPTH = 2
DN_ALPHA = (2.0 * DEPTH) ** 0.25
LN_EPS = 1e-5
NEG_BIG = -1e30
SSD_HEADS = 16
SSD_HEAD_DIM = 64
SSD_GROUPS = 4
SSD_STATE = 128
SSD_CONV = 4
SSD_INNER = 1024
SSD_BC = SSD_GROUPS * SSD_STATE
SSD_CONV_DIM = SSD_INNER + 2 * SSD_BC
SSD_GROUP_WIDTH = SSD_INNER // SSD_GROUPS
SSD_HEADS_PER_GROUP = SSD_HEADS // SSD_GROUPS
S5_WIDTH = 1024
S5_GROUPS = 64
S5_GROUP = 16
S5_STATE = 64
LIN_HEADS = 4
HGRN_WIDTH = 512
GLA_DK = 64
GLA_QK = LIN_HEADS * GLA_DK
GLA_WIDTH = 512
GLA_RANK = 16
GLA_TAU = 16.0
HEAD_W = 128

LANES = 128
SUBLANES = 8
VMEM_LIMIT = 56 * 1024 * 1024

TM = 512
TM_FFN = 1024
FF_CHUNK = 256
SSD_CHUNK = 128
LIN_CHUNK = 128
S5_Q = 32
S5_ROW = S5_Q * S5_GROUP
S5_TILE_GROUPS = LANES // S5_GROUP
S5_PITCH = 40

AB_XBC, AB_Z, AB_U, AB_DT, AB_PACK = 0, 2048, 3072, 4096, 4224
CD_H, CD_GQK, CD_GV, CD_GR, CD_GLR, CD_PACK = 0, 2048, 2560, 3072, 3584, 3712


def _resident(shape):
    n = len(shape)
    return pl.BlockSpec(shape, lambda *_: (0,) * n, pipeline_mode=pl.Buffered(1))


def _dot(a, b):
    return jnp.dot(a, b, preferred_element_type=F32)


def _dot_nt(a, b):
    return lax.dot_general(a, b, (((1,), (1,)), ((), ())), preferred_element_type=F32)


def _dot_tn(a, b):
    return lax.dot_general(a, b, (((0,), (0,)), ((), ())), preferred_element_type=F32)


def _split3(v):
    hi = v.astype(BF16)
    r = v - hi.astype(F32)
    mid = r.astype(BF16)
    lo = (r - mid.astype(F32)).astype(BF16)
    return hi, mid, lo


def _sel_dot(sel, v):
    hi, mid, lo = _split3(v)
    return _dot(sel, hi) + _dot(sel, mid) + _dot(sel, lo)


def _dot_sel(v, sel):
    hi, mid, lo = _split3(v)
    return _dot(hi, sel) + _dot(mid, sel) + _dot(lo, sel)


def _sigmoid(x):
    return 1.0 / (1.0 + jnp.exp(-x))


def _silu(x):
    return x * _sigmoid(x)


def _log_sigmoid(x):
    return jnp.minimum(x, 0.0) - jnp.log1p(jnp.exp(-jnp.abs(x)))


def _softplus(x):
    return jnp.maximum(x, 0.0) + jnp.log1p(jnp.exp(-jnp.abs(x)))


def _gelu_tanh(x):
    return 0.5 * x * (1.0 + jnp.tanh(math.sqrt(2.0 / math.pi) * (x + 0.044715 * (x * x * x))))


def _layer_norm(y, g, b):
    mu = jnp.mean(y, axis=-1, keepdims=True)
    yc = y - mu
    var = jnp.mean(yc * yc, axis=-1, keepdims=True)
    return yc * lax.rsqrt(var + LN_EPS) * g + b


def _rms_norm(y, w):
    return y * lax.rsqrt(jnp.mean(y * y, axis=-1, keepdims=True) + LN_EPS) * w


def _params(*sem):
    return pltpu.CompilerParams(dimension_semantics=sem, vmem_limit_bytes=VMEM_LIMIT)


def _ffn_body(with_ple, x_ref, wg_ref, wu_ref, wd_ref, g_ref, b_ref, *rest):
    if with_ple:
        p_ref, pg_ref, pp_ref, o_ref, acc_ref = rest
    else:
        o_ref, acc_ref = rest
    x = x_ref[...]
    xb = x.astype(BF16)
    for c in range(D_FF // FF_CHUNK):
        sl = slice(c * FF_CHUNK, (c + 1) * FF_CHUNK)
        gate = _dot(xb, wg_ref[:, sl])
        up = _dot(xb, wu_ref[:, sl])
        h = (_silu(gate) * up).astype(BF16)
        d = _dot(h, wd_ref[sl, :])
        if c == 0:
            acc_ref[...] = d
        else:
            acc_ref[...] += d
    y = _layer_norm(DN_ALPHA * x + 0.5 * acc_ref[...], g_ref[...], b_ref[...])
    if with_ple:
        gate = _sigmoid(_dot(y.astype(BF16), pg_ref[...]))
        y = y + gate * _dot(p_ref[...].astype(BF16), pp_ref[...])
    o_ref[...] = y


def _pick(a, *idx):
    rest = a.shape[len(idx):]
    return pl.BlockSpec((None,) * len(idx) + rest, lambda *_: idx + (0,) * len(rest),
                        pipeline_mode=pl.Buffered(1))


def _ffn_ln(x, wg, wu, wd, ln_g, ln_b, layer, pos, ln_idx, ple=None):
    t = x.shape[0]
    row = lambda i: (i, 0)
    in_specs = [pl.BlockSpec((TM_FFN, D_MODEL), row), _pick(wg, layer, pos), _pick(wu, layer, pos),
                _pick(wd, layer, pos), _pick(ln_g, ln_idx), _pick(ln_b, ln_idx)]
    args = [x, wg, wu, wd, ln_g, ln_b]
    if ple is not None:
        p, pg, pp = ple
        in_specs += [pl.BlockSpec((None, TM_FFN, PLE_DIM), lambda i: (layer, i, 0)), _pick(pg, layer), _pick(pp, layer)]
        args += [p, pg, pp]
    return pl.pallas_call(
        functools.partial(_ffn_body, ple is not None),
        grid=(t // TM_FFN,),
        in_specs=in_specs,
        out_specs=pl.BlockSpec((TM_FFN, D_MODEL), row),
        out_shape=jax.ShapeDtypeStruct((t, D_MODEL), F32),
        scratch_shapes=[pltpu.VMEM((TM_FFN, D_MODEL), F32)],
        compiler_params=_params("parallel"),
        name="ffn_ln_ple" if ple is not None else "ffn_ln",
    )(*args)


def _proj_in_body(n_out, x_ref, w_ref, o_ref):
    xb = x_ref[...].astype(BF16)
    step = 512
    for c0 in range(0, n_out, step):
        c1 = min(c0 + step, n_out)
        o_ref[:, c0:c1] = _dot(xb, w_ref[:, c0:c1])


def _proj_in(x, w):
    t, n_out = x.shape[0], w.shape[1]
    return pl.pallas_call(
        functools.partial(_proj_in_body, n_out),
        grid=(t // TM,),
        in_specs=[pl.BlockSpec((TM, D_MODEL), lambda i: (i, 0)), _resident(w.shape)],
        out_specs=pl.BlockSpec((TM, n_out), lambda i: (i, 0)),
        out_shape=jax.ShapeDtypeStruct((t, n_out), F32),
        compiler_params=_params("parallel"),
        name="proj_in",
    )(x, w)


def _proj_ln_body(widths, x_ref, *rest):
    a_refs = rest[:len(widths)]
    w_ref, g_ref, b_ref, o_ref = rest[len(widths):]
    mix, off = None, 0
    for a_ref, k in zip(a_refs, widths):
        d = _dot(a_ref[...], w_ref[off:off + k, :])
        mix = d if mix is None else mix + d
        off += k
    o_ref[...] = _layer_norm(DN_ALPHA * x_ref[...] + mix, g_ref[...], b_ref[...])


def _proj_ln(x, acts, w, w_idx, ln_g, ln_b, ln_idx):
    t = x.shape[0]
    row = lambda i: (i, 0)
    widths = tuple(a.shape[1] for a in acts)
    in_specs = [pl.BlockSpec((TM, D_MODEL), row)]
    in_specs += [pl.BlockSpec((TM, k), row) for k in widths]
    in_specs += [_pick(w, w_idx), _pick(ln_g, ln_idx), _pick(ln_b, ln_idx)]
    return pl.pallas_call(
        functools.partial(_proj_ln_body, widths),
        grid=(t // TM,),
        in_specs=in_specs,
        out_specs=pl.BlockSpec((TM, D_MODEL), row),
        out_shape=jax.ShapeDtypeStruct((t, D_MODEL), F32),
        compiler_params=_params("parallel"),
        name="proj_ln",
    )(x, *acts, w, ln_g, ln_b)


def _ssd_body(xbc_ref, z_ref, dt_ref, convw_ref, convb_ref, dtb_ref, alog_ref, dskip_ref,
              normw_ref, tril_ref, expand_ref, shift_ref, o_ref, xb_ref, state_ref):
    ch = SSD_CHUNK
    gw = SSD_GROUP_WIDTH

    @pl.when(pl.program_id(1) == 0)
    def _():
        xb_ref[0:ch, :] = jnp.zeros((ch, SSD_CONV_DIM), BF16)
        state_ref[...] = jnp.zeros(state_ref.shape, F32)

    def conv_silu(cols):
        cur = xbc_ref[:, cols]
        cur_b = cur.astype(BF16)
        xb_ref[ch:2 * ch, cols] = cur_b
        both = xb_ref[:, cols]
        acc = cur * convw_ref[SSD_CONV - 1:SSD_CONV, cols] + convb_ref[:, cols]
        for j in range(1, SSD_CONV):
            acc = acc + _dot(shift_ref[j - 1], both) * convw_ref[SSD_CONV - 1 - j:SSD_CONV - j, cols]
        xb_ref[0:ch, cols] = cur_b
        return _silu(acc)

    dt = _softplus(dt_ref[...] + dtb_ref[...])
    da = dt * (-jnp.exp(alog_ref[...]))
    a_cum = _sel_dot(tril_ref[...], da)
    a_cum_t = a_cum.T
    dt_hi, dt_mid, _ = _split3(dt)
    dt_e = _dot(dt_hi, expand_ref[...]) + _dot(dt_mid, expand_ref[...])
    ac_e = _dot_sel(a_cum, expand_ref[...])
    ac_last = ac_e[ch - 1:ch, :]
    to_end = jnp.exp(ac_last - ac_e)
    carry_scale = jnp.exp(ac_e)
    chunk_decay = jnp.exp(ac_last)

    row = lax.broadcasted_iota(jnp.int32, (ch, ch), 0)
    col = lax.broadcasted_iota(jnp.int32, (ch, ch), 1)
    causal = col <= row
    lane_head = lax.broadcasted_iota(jnp.int32, (ch, gw), 1) >> int(math.log2(SSD_HEAD_DIM))

    groups = range(SSD_GROUPS)
    cols = [slice(g * gw, (g + 1) * gw) for g in groups]
    xs = [conv_silu(cols[g]) for g in groups]
    bc = [conv_silu(slice(SSD_INNER + k * gw, SSD_INNER + (k + 1) * gw)).astype(BF16)
          for k in range(2 * SSD_BC // gw)]
    bc = [v[:, half * SSD_STATE:(half + 1) * SSD_STATE] for v in bc for half in range(gw // SSD_STATE)]
    bs, cs = bc[:SSD_GROUPS], bc[SSD_GROUPS:]
    xdt = [xs[g] * dt_e[:, cols[g]] for g in groups]
    xdt_b = [v.astype(BF16) for v in xdt]
    scores = [_dot_nt(cs[g], bs[g]).astype(BF16) for g in groups]
    ys = []
    for g in groups:
        y_g = jnp.zeros((ch, gw), F32)
        for hh in range(SSD_HEADS_PER_GROUP):
            h = g * SSD_HEADS_PER_GROUP + hh
            seg = a_cum[:, h:h + 1] - a_cum_t[h:h + 1, :]
            decay = jnp.exp(jnp.where(causal, seg, NEG_BIG)).astype(BF16)
            full = _dot(scores[g] * decay, xdt_b[g])
            y_g = jnp.where(lane_head == hh, full, y_g)
        ys.append(y_g)
    for g in groups:
        gs = cols[g]
        state = state_ref[g]
        y_g = ys[g] + _dot(cs[g], state.astype(BF16)) * carry_scale[:, gs]
        state_ref[g] = state * chunk_decay[:, gs] + _dot_tn(bs[g], (xdt[g] * to_end[:, gs]).astype(BF16))
        y_g = y_g + xs[g] * dskip_ref[:, gs]
        y_g = y_g * _silu(z_ref[:, gs])
        o_ref[:, gs] = _rms_norm(y_g, normw_ref[:, gs]).astype(o_ref.dtype)


def _ssd_mixer(h_ab, batch, seq, conv_w, conv_b, dt_bias, a_log, d_skip, norm_w):
    ch = SSD_CHUNK
    nc = seq // ch
    tok = lambda cb: (lambda b, c: (b * nc + c, cb))
    pad_heads = lambda v: jnp.zeros((1, LANES), F32).at[0, :SSD_HEADS].set(v.astype(F32))
    tril = jnp.asarray(np.tril(np.ones((ch, ch), np.float32)), BF16)
    expand = np.zeros((LANES, SSD_INNER), np.float32)
    for h in range(SSD_HEADS):
        expand[h, h * SSD_HEAD_DIM:(h + 1) * SSD_HEAD_DIM] = 1.0
    expand = jnp.asarray(expand, BF16)
    shift = np.zeros((SSD_CONV - 1, ch, 2 * ch), np.float32)
    for j in range(1, SSD_CONV):
        shift[j - 1, np.arange(ch), ch + np.arange(ch) - j] = 1.0
    shift = jnp.asarray(shift, BF16)
    dskip_e = jnp.repeat(d_skip.astype(F32), SSD_HEAD_DIM)[None, :]
    consts = [conv_w, conv_b[None, :], pad_heads(dt_bias), pad_heads(a_log), dskip_e,
              norm_w[None, :], tril, expand, shift]
    in_specs = [
        pl.BlockSpec((ch, SSD_CONV_DIM), tok(AB_XBC // SSD_CONV_DIM)),
        pl.BlockSpec((ch, SSD_INNER), tok(AB_Z // SSD_INNER)),
        pl.BlockSpec((ch, LANES), tok(AB_DT // LANES)),
    ] + [_resident(a.shape) for a in consts]
    return pl.pallas_call(
        _ssd_body,
        grid=(batch, nc),
        in_specs=in_specs,
        out_specs=pl.BlockSpec((ch, SSD_INNER), tok(0)),
        out_shape=jax.ShapeDtypeStruct((batch * seq, SSD_INNER), BF16),
        scratch_shapes=[pltpu.VMEM((2 * ch, SSD_CONV_DIM), BF16),
                        pltpu.VMEM((SSD_GROUPS, SSD_STATE, SSD_GROUP_WIDTH), F32)],
        compiler_params=_params("parallel", "arbitrary"),
        name="ssd_mixer",
    )(h_ab, h_ab, h_ab, *consts)


def _s5_slot(g, s):
    return (s + g) % S5_TILE_GROUPS


def _s5_body(nblk, u_ref, m_ref, wst_ref, wofft_ref, a1_ref, a2_ref, o_ref, uscr_ref, yscr_ref, ug_ref, yg_ref):
    ng = S5_TILE_GROUPS
    ncol = S5_Q // ng
    nstrip = nblk // SUBLANES
    slot = lax.broadcasted_iota(jnp.int32, (SUBLANES, LANES), 1) >> int(math.log2(S5_GROUP))
    blk = lax.broadcasted_iota(jnp.int32, (nblk, 2 * S5_STATE), 0)

    def pitch_in(c, carry):
        src = pl.multiple_of(c * S5_Q, S5_Q)
        dst = pl.multiple_of(c * S5_PITCH, SUBLANES)
        uscr_ref[pl.ds(dst, S5_Q), :] = u_ref[pl.ds(src, S5_Q), :]
        return carry
    lax.fori_loop(0, nblk, pitch_in, 0, unroll=8)

    def gather(i, carry):
        base = pl.multiple_of(i * (SUBLANES * S5_PITCH), SUBLANES)
        row = pl.multiple_of(i * SUBLANES, SUBLANES)
        for m in range(ncol):
            rolled = []
            for s in range(ng):
                us = uscr_ref[pl.ds(base + ng * m + s, SUBLANES, stride=S5_PITCH), :]
                rolled.append(pltpu.roll(us, S5_GROUP * s, axis=1) if s else us)
            for g in range(ng):
                v = rolled[0]
                for s in range(1, ng):
                    v = jnp.where(slot == _s5_slot(g, s), rolled[s], v)
                ug_ref[pl.ds(row, SUBLANES), (g * ncol + m) * LANES:(g * ncol + m + 1) * LANES] = v
        return carry
    lax.fori_loop(0, nstrip, gather, 0, unroll=2)

    us = [ug_ref[:, g * S5_ROW:(g + 1) * S5_ROW].astype(BF16) for g in range(ng)]
    xs = [_dot(us[g], wst_ref[g]) for g in range(ng)]
    for k in range(int(math.log2(nblk))):
        sh = 1 << k
        for g in range(ng):
            prev = jnp.where(blk >= sh, pltpu.roll(xs[g], sh, axis=0), 0.0)
            xs[g] = (xs[g] + prev * a1_ref[g, k:k + 1, :]
                     + pltpu.roll(prev, S5_STATE, axis=1) * a2_ref[g, k:k + 1, :])
    for g in range(ng):
        x_in = jnp.where(blk >= 1, pltpu.roll(xs[g], 1, axis=0), 0.0)
        yg_ref[:, g * S5_ROW:(g + 1) * S5_ROW] = (_dot(us[g], m_ref[g])
                                                   + _dot_nt(x_in.astype(BF16), wofft_ref[g]))

    def scatter(i, carry):
        base = pl.multiple_of(i * (SUBLANES * S5_PITCH), SUBLANES)
        row = pl.multiple_of(i * SUBLANES, SUBLANES)
        for m in range(ncol):
            cols = [yg_ref[pl.ds(row, SUBLANES), (g * ncol + m) * LANES:(g * ncol + m + 1) * LANES]
                    for g in range(ng)]
            for tt in range(ng):
                z = cols[0]
                for g in range(1, ng):
                    z = jnp.where(slot == _s5_slot(g, tt), cols[g], z)
                sh = (LANES - S5_GROUP * tt) % LANES
                yscr_ref[pl.ds(base + ng * m + tt, SUBLANES, stride=S5_PITCH), :] = (
                    pltpu.roll(z, sh, axis=1) if sh else z)
        return carry
    lax.fori_loop(0, nstrip, scatter, 0, unroll=2)

    def pitch_out(c, carry):
        src = pl.multiple_of(c * S5_PITCH, SUBLANES)
        dst = pl.multiple_of(c * S5_Q, S5_Q)
        o_ref[pl.ds(dst, S5_Q), :] = yscr_ref[pl.ds(src, S5_Q), :]
        return carry
    lax.fori_loop(0, nblk, pitch_out, 0, unroll=8)


def _s5_core(h_ab, batch, seq, tables):
    nblk = seq // S5_Q
    ng = S5_TILE_GROUPS
    grp = lambda a: pl.BlockSpec((ng,) + a.shape[1:], lambda j, b: (j, 0, 0))
    return pl.pallas_call(
        functools.partial(_s5_body, nblk),
        grid=(S5_GROUPS // ng, batch),
        in_specs=[pl.BlockSpec((seq, LANES), lambda j, b: (b, AB_U // LANES + j))]
        + [grp(a) for a in tables],
        out_specs=pl.BlockSpec((seq, LANES), lambda j, b: (b, j)),
        out_shape=jax.ShapeDtypeStruct((batch * seq, S5_WIDTH), F32),
        scratch_shapes=[pltpu.VMEM((nblk * S5_PITCH, LANES), F32), pltpu.VMEM((nblk * S5_PITCH, LANES), F32),
                        pltpu.VMEM((nblk, ng * S5_ROW), F32), pltpu.VMEM((nblk, ng * S5_ROW), F32)],
        compiler_params=_params("parallel", "parallel"),
        name="s5_core",
    )(h_ab, *tables)


def _s5_table_body(kt_ref, pw_rr_ref, pw_ii_ref, b1_ref, b2_ref, c1_ref, c2_ref, m_ref, wst_ref, wofft_ref):
    ng = S5_TILE_GROUPS
    ncol = S5_Q // ng
    lane = lax.broadcasted_iota(jnp.int32, (S5_GROUP, LANES), 1)
    for g in range(ng):
        strip = [jnp.zeros((S5_GROUP, LANES), F32)] * ncol + [kt_ref[g, :, c * LANES:(c + 1) * LANES]
                                                            for c in range(ncol)]
        b1, b2, c1, c2 = b1_ref[g], b2_ref[g], c1_ref[g], c2_ref[g]
        for s in range(S5_Q):
            start = ncol * LANES - S5_GROUP * s
            a, sh = start // LANES, start % LANES
            k = ng * (s // ng) + _s5_slot(g, s % ng)
            rows = slice(S5_GROUP * k, S5_GROUP * (k + 1))
            for c in range(ncol):
                if sh:
                    w = jnp.where(lane < LANES - sh, pltpu.roll(strip[a + c], LANES - sh, axis=1),
                                  pltpu.roll(strip[a + c + 1], LANES - sh, axis=1))
                else:
                    w = strip[a + c]
                if g:
                    w = pltpu.roll(w, S5_GROUP * g, axis=1)
                m_ref[g, rows, c * LANES:(c + 1) * LANES] = w.astype(BF16)
            e = S5_Q - 1 - s
            wst_ref[g, rows, :] = (pw_rr_ref[g, e:e + 1, :] * b1 + pw_ii_ref[g, e:e + 1, :] * b2).astype(BF16)
            wofft_ref[g, rows, :] = (pw_rr_ref[g, s + 1:s + 2, :] * c1
                                     + pw_ii_ref[g, s + 1:s + 2, :] * c2).astype(BF16)


def _s5_tables(lam_re, lam_im, log_dt, b_re, b_im, c_re, c_im, nblk):
    q, ng = S5_Q, S5_TILE_GROUPS
    hp = lax.Precision.HIGHEST
    lam = lax.complex(lam_re.astype(F32), lam_im.astype(F32))
    ldt = lam * jnp.exp(log_dt.astype(F32))[:, None]
    lam_bar = jnp.exp(ldt)
    b_bar = ((lam_bar - 1.0) / lam)[..., None] * lax.complex(b_re.astype(F32), b_im.astype(F32))
    c = lax.complex(c_re.astype(F32), c_im.astype(F32))
    tau = jnp.arange(q + 1, dtype=F32)
    pw = jnp.exp(ldt[:, None, :] * tau[None, :, None])
    kt = jnp.real(jnp.einsum('gpn,gtn,gnq->gqtp', c, pw[:, :q], b_bar, precision=hp)).reshape(
        S5_GROUPS, S5_GROUP, S5_ROW)
    halves = lambda lo, hi: jnp.concatenate([lo, hi], axis=-1)
    bt = jnp.transpose(b_bar, (0, 2, 1))
    small = [halves(jnp.real(pw), jnp.real(pw)), halves(jnp.imag(pw), jnp.imag(pw)),
             halves(jnp.real(bt), jnp.imag(bt)), halves(-jnp.imag(bt), jnp.real(bt)),
             halves(jnp.real(c), -jnp.imag(c)), halves(-jnp.imag(c), -jnp.real(c))]
    grp = lambda shape: pl.BlockSpec((ng,) + shape, lambda j: (j, 0, 0))
    shapes = [(S5_ROW, S5_ROW), (S5_ROW, 2 * S5_STATE), (S5_ROW, 2 * S5_STATE)]
    m, wst, wofft = pl.pallas_call(
        _s5_table_body,
        grid=(S5_GROUPS // ng,),
        in_specs=[grp(kt.shape[1:])] + [grp(a.shape[1:]) for a in small],
        out_specs=[grp(sh) for sh in shapes],
        out_shape=[jax.ShapeDtypeStruct((S5_GROUPS,) + sh, BF16) for sh in shapes],
        compiler_params=_params("parallel"),
        name="s5_tables",
    )(kt, *small)
    nlev = int(math.log2(nblk))
    step = jnp.exp(ldt[:, None, :] * (q * 2.0 ** jnp.arange(nlev, dtype=F32))[None, :, None])
    a1 = halves(jnp.real(step), jnp.real(step))
    a2 = halves(-jnp.imag(step), jnp.imag(step))
    return m, wst, wofft, a1, a2


def _s5_post_body(yc_ref, u_ref, d_ref, w_ref, b_ref, o_ref):
    y = _gelu_tanh(yc_ref[...] + d_ref[...] * u_ref[...])
    gate = _sigmoid(_dot(y.astype(BF16), w_ref[...]) + b_ref[...])
    o_ref[...] = (y * gate).astype(o_ref.dtype)


def _s5_post(y_conv, h_ab, d_skip, w_glu, b_glu):
    t = y_conv.shape[0]
    consts = [d_skip, w_glu, b_glu]
    return pl.pallas_call(
        _s5_post_body,
        grid=(t // TM,),
        in_specs=[pl.BlockSpec((TM, S5_WIDTH), lambda i: (i, 0)),
                  pl.BlockSpec((TM, S5_WIDTH), lambda i: (i, AB_U // S5_WIDTH))]
        + [_resident(a.shape) for a in consts],
        out_specs=pl.BlockSpec((TM, S5_WIDTH), lambda i: (i, 0)),
        out_shape=jax.ShapeDtypeStruct((t, S5_WIDTH), BF16),
        compiler_params=_params("parallel"),
        name="s5_post",
    )(y_conv, h_ab, *consts)


def _lin_levels(c):
    return [c >> (i + 1) for i in range(int(math.log2(c)))]


def _lin_tables(c):
    t = np.arange(c)[:, None]
    j = np.arange(c)[None, :]
    lvl = np.full((c, c), -1, np.int32)
    lvl[np.arange(c), np.arange(c)] = 0
    sgn = []
    for i, b in enumerate(_lin_levels(c)):
        same = (t // (2 * b)) == (j // (2 * b))
        second = (t % (2 * b)) >= b
        lvl[same & second & ((j % (2 * b)) < b)] = i + 1
        sgn.append(np.broadcast_to(np.where(second, 1.0, -1.0) * math.log2(math.e), (c, LANES)))
    return (jnp.asarray((j <= t).astype(np.float32), BF16), jnp.asarray(lvl),
            jnp.asarray(np.stack(sgn), F32))


def _gate_factors(g, tril, sgn_ref, gcum_ref):
    c = LIN_CHUNK
    gcum = _sel_dot(tril, g)
    gcum_ref[...] = gcum
    pos = lax.broadcasted_iota(jnp.int32, (c, LANES), 0)
    factors = []
    for i, b in enumerate(_lin_levels(c)):
        if 2 * b >= SUBLANES:
            mids = [jnp.broadcast_to(gcum_ref[blk * 2 * b + b - 1:blk * 2 * b + b, :], (2 * b, LANES))
                    for blk in range(c // (2 * b))]
            gmid = mids[0] if len(mids) == 1 else jnp.concatenate(mids, axis=0)
        elif b == 2:
            p4 = pos & 3
            gmid = jnp.where(p4 == 0, pltpu.roll(gcum, c - 1, axis=0),
                             jnp.where(p4 == 1, gcum,
                                       jnp.where(p4 == 2, pltpu.roll(gcum, 1, axis=0),
                                                 pltpu.roll(gcum, 2, axis=0))))
        else:
            gmid = jnp.where((pos & 1) == 1, pltpu.roll(gcum, 1, axis=0), gcum)
        factors.append(jnp.exp2((gcum - gmid) * sgn_ref[i]).astype(BF16))
    e_cum = jnp.exp(gcum)
    return e_cum, e_cum.astype(BF16), jnp.exp(gcum[c - 1:c, :] - gcum).astype(BF16), factors


def _lin_body(h_ref, gqk_ref, gv_ref, gr_ref, glr_ref, loglb_ref, log1mlb_ref, onemlb_ref, hnorm_ref,
              wup_ref, bgate_ref, gnorm_ref, tril_ref, lvl_ref, sgn_ref, o_ref, state_ref, gcum_ref):
    c = LIN_CHUNK

    @pl.when(pl.program_id(1) == 0)
    def _():
        state_ref[...] = jnp.zeros(state_ref.shape, F32)

    hq = h_ref[:, 0:HGRN_WIDTH]
    hf = h_ref[:, HGRN_WIDTH:2 * HGRN_WIDTH]
    lo = loglb_ref[...]
    hi_ = log1mlb_ref[...] + _log_sigmoid(hf)
    log_f = jnp.maximum(lo, hi_) + jnp.log1p(jnp.exp(-jnp.abs(lo - hi_)))
    q_c = _silu(hq)
    k_c = onemlb_ref[...] * _sigmoid(-hf)
    pre = _dot(glr_ref[...].astype(BF16), wup_ref[...]) + bgate_ref[...]
    log_a = _log_sigmoid(pre) * (1.0 / GLA_TAU)
    q_d = gqk_ref[:, 0:GLA_QK] * (GLA_DK ** -0.5)
    k_d = gqk_ref[:, GLA_QK:2 * GLA_QK]

    lvl = lvl_ref[...]
    lane = lax.broadcasted_iota(jnp.int32, (c, LANES), 1)
    tril = tril_ref[...]

    heads = []
    for hd in range(LIN_HEADS):
        ls = slice(hd * HEAD_W, (hd + 1) * HEAD_W)
        heads.append((hd, q_c[:, ls], k_c[:, ls], hd,
                      h_ref[:, 2 * HGRN_WIDTH + hd * HEAD_W:2 * HGRN_WIDTH + (hd + 1) * HEAD_W], hnorm_ref[:, ls],
                      h_ref[:, 3 * HGRN_WIDTH + hd * HEAD_W:3 * HGRN_WIDTH + (hd + 1) * HEAD_W], ls))
    for hd in range(LIN_HEADS):
        tile, half = divmod(hd, LANES // GLA_DK)
        ts = slice(tile * LANES, (tile + 1) * LANES)
        ls = slice(hd * HEAD_W, (hd + 1) * HEAD_W)
        mine = (lane >> int(math.log2(GLA_DK))) == half
        heads.append((LIN_HEADS + hd, jnp.where(mine, q_d[:, ts], 0.0), jnp.where(mine, k_d[:, ts], 0.0),
                      LIN_HEADS + tile, gv_ref[:, ls], gnorm_ref[:, ls], gr_ref[:, ls],
                      slice(HGRN_WIDTH + hd * HEAD_W, HGRN_WIDTH + (hd + 1) * HEAD_W)))

    gates = [log_f[:, t * LANES:(t + 1) * LANES] for t in range(LIN_HEADS)]
    gates += [log_a[:, t * LANES:(t + 1) * LANES] for t in range(GLA_QK // LANES)]
    factors = [_gate_factors(g, tril, sgn_ref, gcum_ref.at[t]) for t, g in enumerate(gates)]
    qkb = [(q.astype(BF16), k.astype(BF16)) for _, q, k, *_ in heads]
    attns = []
    for (qb, kb), head in zip(qkb, heads):
        e_lvl = factors[head[3]][3]
        attn = jnp.where(lvl == 0, _dot_nt(qb, kb), 0.0)
        for i, e in enumerate(e_lvl):
            attn = jnp.where(lvl == i + 1, _dot_nt(qb * e, kb * e), attn)
        attns.append(attn.astype(BF16))
    for (qb, kb), attn, (idx, _, _, tile, v, norm_w, gate, cols) in zip(qkb, attns, heads):
        e_cum, e_cum_b, e_end_b, _ = factors[tile]
        vb = v.astype(BF16)
        state_t = state_ref[idx]
        o = _dot(attn, vb) + _dot_nt(qb * e_cum_b, state_t.astype(BF16))
        state_ref[idx] = state_t * e_cum[c - 1:c, :] + _dot_tn(vb, kb * e_end_b)
        o_ref[:, cols] = (_rms_norm(o, norm_w) * _silu(gate)).astype(o_ref.dtype)


def _lin_mixer(h_cd, batch, seq, lb, hgrn_norm_w, gla_w_gate_up, gla_b_gate, gla_norm_w):
    c = LIN_CHUNK
    nc = seq // c
    tok = lambda cb: (lambda b, i: (b * nc + i, cb))
    lb = lb.astype(F32)[None, :]
    wup = jnp.zeros((LANES, GLA_QK), F32).at[:GLA_RANK].set(gla_w_gate_up).astype(BF16)
    tril, lvl, sgn = _lin_tables(c)
    consts = [jnp.log(lb), jnp.log1p(-lb), 1.0 - lb, hgrn_norm_w[None, :], wup, gla_b_gate[None, :],
              gla_norm_w[None, :], tril, lvl, sgn]
    in_specs = [
        pl.BlockSpec((c, 4 * HGRN_WIDTH), tok(CD_H // (4 * HGRN_WIDTH))),
        pl.BlockSpec((c, 2 * GLA_QK), tok(CD_GQK // (2 * GLA_QK))),
        pl.BlockSpec((c, GLA_WIDTH), tok(CD_GV // GLA_WIDTH)),
        pl.BlockSpec((c, GLA_WIDTH), tok(CD_GR // GLA_WIDTH)),
        pl.BlockSpec((c, LANES), tok(CD_GLR // LANES)),
    ] + [_resident(a.shape) for a in consts]
    return pl.pallas_call(
        _lin_body,
        grid=(batch, nc),
        in_specs=in_specs,
        out_specs=pl.BlockSpec((c, HGRN_WIDTH + GLA_WIDTH), tok(0)),
        out_shape=jax.ShapeDtypeStruct((batch * seq, HGRN_WIDTH + GLA_WIDTH), BF16),
        scratch_shapes=[pltpu.VMEM((2 * LIN_HEADS, HEAD_W, HEAD_W), F32),
                        pltpu.VMEM((LIN_HEADS + GLA_QK // LANES, c, LANES), F32)],
        compiler_params=_params("parallel", "arbitrary"),
        name="lin_mixer",
    )(h_cd, h_cd, h_cd, h_cd, h_cd, *consts)


def _pack_cols(w, splits, order, total):
    bounds = np.concatenate([[0], np.cumsum(splits)])
    parts = [w[:, bounds[f]:bounds[f + 1]] for f in order]
    parts.append(jnp.zeros((w.shape[0], total - sum(splits[f] for f in order)), w.dtype))
    return jnp.concatenate(parts, axis=1).astype(BF16)


def _hgrn_lower_bound(lb_logits, layer):
    cum = jnp.cumsum(jax.nn.softmax(lb_logits.astype(F32), axis=0), axis=0)
    return cum[layer] - cum[0]


def kernel(x, p, ln_g, ln_b, ffn_w_gate, ffn_w_up, ffn_w_down, ple_w_gate, ple_w_proj, ab_w_in, ab_w_out,
           ssd_conv_w, ssd_conv_b, ssd_dt_bias, ssd_a_log, ssd_d, ssd_norm_w, s5_lambda_re, s5_lambda_im,
           s5_log_dt, s5_b_re, s5_b_im, s5_c_re, s5_c_im, s5_d, s5_w_glu, s5_b_glu, cd_w_in, cd_w_out,
           hgrn_lb_logits, hgrn_norm_w, gla_w_gate_up, gla_b_gate, gla_norm_w):
    batch, seq, _ = x.shape
    t = batch * seq
    x = x.reshape(t, D_MODEL)
    bf = lambda w: w.astype(BF16)
    ab_splits = (SSD_INNER, SSD_CONV_DIM, SSD_HEADS, S5_WIDTH)
    cd_splits = (HGRN_WIDTH, HGRN_WIDTH, HGRN_WIDTH, HGRN_WIDTH, GLA_QK, GLA_QK, GLA_WIDTH, GLA_RANK,
                 GLA_WIDTH)
    wg, wu, wd = bf(ffn_w_gate), bf(ffn_w_up), bf(ffn_w_down)
    pg, pp = bf(ple_w_gate), bf(ple_w_proj)
    p = p.reshape(DEPTH, t, PLE_DIM)
    ln_g = ln_g.reshape(DEPTH * 3, 1, D_MODEL)
    ln_b = ln_b.reshape(DEPTH * 3, 1, D_MODEL)
    w_ab_out, w_cd_out = bf(ab_w_out), bf(cd_w_out)
    for i in range(DEPTH):
        j = i // 2
        x = _ffn_ln(x, wg, wu, wd, ln_g, ln_b, i, 0, 3 * i)
        if i % 2 == 0:
            w_in = _pack_cols(ab_w_in[j], ab_splits, [1, 0, 3, 2], AB_PACK)
            h_ab = _proj_in(x, w_in)
            y_a = _ssd_mixer(h_ab, batch, seq, ssd_conv_w[j], ssd_conv_b[j], ssd_dt_bias[j], ssd_a_log[j],
                             ssd_d[j], ssd_norm_w[j])
            tables = _s5_tables(s5_lambda_re[j], s5_lambda_im[j], s5_log_dt[j], s5_b_re[j], s5_b_im[j],
                                s5_c_re[j], s5_c_im[j], seq // S5_Q)
            y_conv = _s5_core(h_ab, batch, seq, tables)
            y_b = _s5_post(y_conv, h_ab, s5_d[j].reshape(1, S5_WIDTH).astype(F32), bf(s5_w_glu[j]),
                           s5_b_glu[j][None, :])
            x = _proj_ln(x, [y_a, y_b], w_ab_out, j, ln_g, ln_b, 3 * i + 1)
        else:
            w_in = _pack_cols(cd_w_in[j], cd_splits, [0, 1, 2, 3, 4, 5, 6, 8, 7], CD_PACK)
            h_cd = _proj_in(x, w_in)
            o_cd = _lin_mixer(h_cd, batch, seq, _hgrn_lower_bound(hgrn_lb_logits, i), hgrn_norm_w[j],
                              gla_w_gate_up[j], gla_b_gate[j], gla_norm_w[j])
            x = _proj_ln(x, [o_cd], w_cd_out, j, ln_g, ln_b, 3 * i + 1)
        x = _ffn_ln(x, wg, wu, wd, ln_g, ln_b, i, 1, 3 * i + 2, ple=(p, pg, pp))
    return x.reshape(batch, seq, D_MODEL)
```

```python
import functools
import math

import jax
import jax.numpy as jnp
import numpy as np
from jax import lax
from jax.experimental import pallas as pl
from jax.experimental.pallas import tpu as pltpu

F32 = jnp.float32
BF16 = jnp.bfloat16

D_MODEL = 1024
D_FF = 2816
PLE_DIM = 256
DEPTH = 2
DN_ALPHA = (2.0 * DEPTH) ** 0.25
LN_EPS = 1e-5
NEG_BIG = -1e30
SSD_HEADS = 16
SSD_HEAD_DIM = 64
SSD_GROUPS = 4
SSD_STATE = 128
SSD_CONV = 4
SSD_INNER = 1024
SSD_BC = SSD_GROUPS * SSD_STATE
SSD_CONV_DIM = SSD_INNER + 2 * SSD_BC
SSD_GROUP_WIDTH = SSD_INNER // SSD_GROUPS
SSD_HEADS_PER_GROUP = SSD_HEADS // SSD_GROUPS
S5_WIDTH = 1024
S5_GROUPS = 64
S5_GROUP = 16
S5_STATE = 64
LIN_HEADS = 4
HGRN_WIDTH = 512
GLA_DK = 64
GLA_QK = LIN_HEADS * GLA_DK
GLA_WIDTH = 512
GLA_RANK = 16
GLA_TAU = 16.0
HEAD_W = 128

LANES = 128
SUBLANES = 8
VMEM_LIMIT = 56 * 1024 * 1024

TM = 512
TM_FFN = 1024
SUB_FFN = 512
SUB_ROWS = 256
FF_CHUNK = 256
SSD_CHUNK = 128
MIX_STEP_CHUNKS = 4
LIN_CHUNK = 128
S5_Q = 32
S5_ROW = S5_Q * S5_GROUP
S5_TILE_GROUPS = LANES // S5_GROUP
S5_PITCH = 40

AB_XBC, AB_Z, AB_U, AB_DT, AB_PACK = 0, 2048, 3072, 4096, 4224
CD_H, CD_GQK, CD_GV, CD_GR, CD_GLR, CD_PACK = 0, 2048, 2560, 3072, 3584, 3712


def _resident(shape):
    n = len(shape)
    return pl.BlockSpec(shape, lambda *_: (0,) * n, pipeline_mode=pl.Buffered(1))


def _dot(a, b):
    return jnp.dot(a, b, preferred_element_type=F32)


def _dot_nt(a, b):
    return lax.dot_general(a, b, (((1,), (1,)), ((), ())), preferred_element_type=F32)


def _dot_tn(a, b):
    return lax.dot_general(a, b, (((0,), (0,)), ((), ())), preferred_element_type=F32)


def _split3(v):
    hi = v.astype(BF16)
    r = v - hi.astype(F32)
    mid = r.astype(BF16)
    lo = (r - mid.astype(F32)).astype(BF16)
    return hi, mid, lo


def _sel_dot(sel, v):
    hi, mid, lo = _split3(v)
    return _dot(sel, hi) + _dot(sel, mid) + _dot(sel, lo)


def _dot_sel(v, sel):
    hi, mid, lo = _split3(v)
    return _dot(hi, sel) + _dot(mid, sel) + _dot(lo, sel)


def _sigmoid(x):
    return 1.0 / (1.0 + jnp.exp(-x))


def _silu(x):
    return x * _sigmoid(x)


def _log_sigmoid(x):
    return jnp.minimum(x, 0.0) - jnp.log1p(jnp.exp(-jnp.abs(x)))


def _softplus(x):
    return jnp.maximum(x, 0.0) + jnp.log1p(jnp.exp(-jnp.abs(x)))


def _gelu_tanh(x):
    return 0.5 * x * (1.0 + jnp.tanh(math.sqrt(2.0 / math.pi) * (x + 0.044715 * (x * x * x))))


def _layer_norm(y, g, b):
    mu = jnp.mean(y, axis=-1, keepdims=True)
    yc = y - mu
    var = jnp.mean(yc * yc, axis=-1, keepdims=True)
    return yc * lax.rsqrt(var + LN_EPS) * g + b


def _rms_norm(y, w):
    return y * lax.rsqrt(jnp.mean(y * y, axis=-1, keepdims=True) + LN_EPS) * w


def _params(*sem):
    return pltpu.CompilerParams(dimension_semantics=sem, vmem_limit_bytes=VMEM_LIMIT)


def _ffn_body(with_ple, x_ref, wg_ref, wu_ref, wd_ref, g_ref, b_ref, *rest):
    if with_ple:
        p_ref, pg_ref, pp_ref, o_ref, acc_ref = rest
    else:
        o_ref, acc_ref = rest
    for r in range(x_ref.shape[0] // SUB_FFN):
        rows = slice(r * SUB_FFN, (r + 1) * SUB_FFN)
        x = x_ref[rows, :]
        xb = x.astype(BF16)
        for c in range(D_FF // FF_CHUNK):
            sl = slice(c * FF_CHUNK, (c + 1) * FF_CHUNK)
            gate = _dot(xb, wg_ref[:, sl])
            up = _dot(xb, wu_ref[:, sl])
            h = (_silu(gate) * up).astype(BF16)
            d = _dot(h, wd_ref[sl, :])
            if c == 0:
                acc_ref[rows, :] = d
            else:
                acc_ref[rows, :] += d
        y = _layer_norm(DN_ALPHA * x + 0.5 * acc_ref[rows, :], g_ref[...], b_ref[...])
        if with_ple:
            gate = _sigmoid(_dot(y.astype(BF16), pg_ref[...]))
            y = y + gate * _dot(p_ref[rows, :].astype(BF16), pp_ref[...])
        o_ref[rows, :] = y


def _pick(a, *idx):
    rest = a.shape[len(idx):]
    return pl.BlockSpec((None,) * len(idx) + rest, lambda *_: idx + (0,) * len(rest),
                        pipeline_mode=pl.Buffered(1))


def _ffn_ln(x, wg, wu, wd, ln_g, ln_b, layer, pos, ln_idx, ple=None):
    t = x.shape[0]
    row = lambda i: (i, 0)
    in_specs = [pl.BlockSpec((TM_FFN, D_MODEL), row), _pick(wg, layer, pos), _pick(wu, layer, pos),
                _pick(wd, layer, pos), _pick(ln_g, ln_idx), _pick(ln_b, ln_idx)]
    args = [x, wg, wu, wd, ln_g, ln_b]
    if ple is not None:
        p, pg, pp = ple
        in_specs += [pl.BlockSpec((None, TM_FFN, PLE_DIM), lambda i: (layer, i, 0)), _pick(pg, layer), _pick(pp, layer)]
        args += [p, pg, pp]
    return pl.pallas_call(
        functools.partial(_ffn_body, ple is not None),
        grid=(t // TM_FFN,),
        in_specs=in_specs,
        out_specs=pl.BlockSpec((TM_FFN, D_MODEL), row),
        out_shape=jax.ShapeDtypeStruct((t, D_MODEL), F32),
        scratch_shapes=[pltpu.VMEM((TM_FFN, D_MODEL), F32)],
        compiler_params=_params("parallel"),
        name="ffn_ln_ple" if ple is not None else "ffn_ln",
    )(*args)


def _proj_in_body(n_out, x_ref, w_ref, o_ref):
    xb = x_ref[...].astype(BF16)
    step = 512
    for c0 in range(0, n_out, step):
        c1 = min(c0 + step, n_out)
        o_ref[:, c0:c1] = _dot(xb, w_ref[:, c0:c1])


def _proj_in(x, w):
    t, n_out = x.shape[0], w.shape[1]
    return pl.pallas_call(
        functools.partial(_proj_in_body, n_out),
        grid=(t // TM,),
        in_specs=[pl.BlockSpec((TM, D_MODEL), lambda i: (i, 0)), _resident(w.shape)],
        out_specs=pl.BlockSpec((TM, n_out), lambda i: (i, 0)),
        out_shape=jax.ShapeDtypeStruct((t, n_out), F32),
        compiler_params=_params("parallel"),
        name="proj_in",
    )(x, w)


def _proj_ln_body(widths, x_ref, *rest):
    a_refs = rest[:len(widths)]
    w_ref, g_ref, b_ref, o_ref = rest[len(widths):]
    for r in range(x_ref.shape[0] // SUB_ROWS):
        rows = slice(r * SUB_ROWS, (r + 1) * SUB_ROWS)
        mix, off = None, 0
        for a_ref, k in zip(a_refs, widths):
            d = _dot(a_ref[rows, :], w_ref[off:off + k, :])
            mix = d if mix is None else mix + d
            off += k
        o_ref[rows, :] = _layer_norm(DN_ALPHA * x_ref[rows, :] + mix, g_ref[...], b_ref[...])


def _proj_ln(x, acts, w, w_idx, ln_g, ln_b, ln_idx):
    t = x.shape[0]
    row = lambda i: (i, 0)
    widths = tuple(a.shape[1] for a in acts)
    in_specs = [pl.BlockSpec((TM, D_MODEL), row)]
    in_specs += [pl.BlockSpec((TM, k), row) for k in widths]
    in_specs += [_pick(w, w_idx), _pick(ln_g, ln_idx), _pick(ln_b, ln_idx)]
    return pl.pallas_call(
        functools.partial(_proj_ln_body, widths),
        grid=(t // TM,),
        in_specs=in_specs,
        out_specs=pl.BlockSpec((TM, D_MODEL), row),
        out_shape=jax.ShapeDtypeStruct((t, D_MODEL), F32),
        compiler_params=_params("parallel"),
        name="proj_ln",
    )(x, *acts, w, ln_g, ln_b)


def _ssd_body(xbc_ref, z_ref, dt_ref, convw_ref, convb_ref, dtb_ref, alog_ref, dskip_ref,
              normw_ref, tril_ref, expand_ref, shift_ref, o_ref, xb_ref, state_ref):
    ch = SSD_CHUNK
    gw = SSD_GROUP_WIDTH

    @pl.when(pl.program_id(1) == 0)
    def _():
        xb_ref[0:ch, :] = jnp.zeros((ch, SSD_CONV_DIM), BF16)
        state_ref[...] = jnp.zeros(state_ref.shape, F32)

    def chunk(i, carry):
        rows = pl.ds(pl.multiple_of(i * ch, ch), ch)

        def conv_silu(cols):
            cur = xbc_ref[rows, cols]
            cur_b = cur.astype(BF16)
            xb_ref[ch:2 * ch, cols] = cur_b
            both = xb_ref[:, cols]
            acc = cur * convw_ref[SSD_CONV - 1:SSD_CONV, cols] + convb_ref[:, cols]
            for j in range(1, SSD_CONV):
                acc = acc + _dot(shift_ref[j - 1], both) * convw_ref[SSD_CONV - 1 - j:SSD_CONV - j, cols]
            xb_ref[0:ch, cols] = cur_b
            return _silu(acc)

        dt = _softplus(dt_ref[rows, :] + dtb_ref[...])
        da = dt * (-jnp.exp(alog_ref[...]))
        a_cum = _sel_dot(tril_ref[...], da)
        a_cum_t = a_cum.T
        dt_hi, dt_mid, _ = _split3(dt)
        dt_e = _dot(dt_hi, expand_ref[...]) + _dot(dt_mid, expand_ref[...])
        ac_e = _dot_sel(a_cum, expand_ref[...])
        ac_last = ac_e[ch - 1:ch, :]
        to_end = jnp.exp(ac_last - ac_e)
        carry_scale = jnp.exp(ac_e)
        chunk_decay = jnp.exp(ac_last)

        row = lax.broadcasted_iota(jnp.int32, (ch, ch), 0)
        col = lax.broadcasted_iota(jnp.int32, (ch, ch), 1)
        causal = col <= row
        lane_head = lax.broadcasted_iota(jnp.int32, (ch, gw), 1) >> int(math.log2(SSD_HEAD_DIM))

        groups = range(SSD_GROUPS)
        cols = [slice(g * gw, (g + 1) * gw) for g in groups]
        xs = [conv_silu(cols[g]) for g in groups]
        bc = [conv_silu(slice(SSD_INNER + k * gw, SSD_INNER + (k + 1) * gw)).astype(BF16)
              for k in range(2 * SSD_BC // gw)]
        bc = [v[:, half * SSD_STATE:(half + 1) * SSD_STATE] for v in bc for half in range(gw // SSD_STATE)]
        bs, cs = bc[:SSD_GROUPS], bc[SSD_GROUPS:]
        xdt = [xs[g] * dt_e[:, cols[g]] for g in groups]
        xdt_b = [v.astype(BF16) for v in xdt]
        scores = [_dot_nt(cs[g], bs[g]).astype(BF16) for g in groups]
        ys = []
        for g in groups:
            y_g = jnp.zeros((ch, gw), F32)
            for hh in range(SSD_HEADS_PER_GROUP):
                h = g * SSD_HEADS_PER_GROUP + hh
                seg = a_cum[:, h:h + 1] - a_cum_t[h:h + 1, :]
                decay = jnp.exp(jnp.where(causal, seg, NEG_BIG)).astype(BF16)
                full = _dot(scores[g] * decay, xdt_b[g])
                y_g = jnp.where(lane_head == hh, full, y_g)
            ys.append(y_g)
        for g in groups:
            gs = cols[g]
            state = state_ref[g]
            y_g = ys[g] + _dot(cs[g], state.astype(BF16)) * carry_scale[:, gs]
            state_ref[g] = state * chunk_decay[:, gs] + _dot_tn(bs[g], (xdt[g] * to_end[:, gs]).astype(BF16))
            y_g = y_g + xs[g] * dskip_ref[:, gs]
            y_g = y_g * _silu(z_ref[rows, gs])
            o_ref[rows, gs] = _rms_norm(y_g, normw_ref[:, gs]).astype(o_ref.dtype)
        return carry

    lax.fori_loop(0, xbc_ref.shape[0] // ch, chunk, 0)


def _ssd_mixer(h_ab, batch, seq, conv_w, conv_b, dt_bias, a_log, d_skip, norm_w):
    ch = SSD_CHUNK
    step = ch * MIX_STEP_CHUNKS
    nc = seq // step
    tok = lambda cb: (lambda b, c: (b * nc + c, cb))
    pad_heads = lambda v: jnp.zeros((1, LANES), F32).at[0, :SSD_HEADS].set(v.astype(F32))
    tril = jnp.asarray(np.tril(np.ones((ch, ch), np.float32)), BF16)
    expand = np.zeros((LANES, SSD_INNER), np.float32)
    for h in range(SSD_HEADS):
        expand[h, h * SSD_HEAD_DIM:(h + 1) * SSD_HEAD_DIM] = 1.0
    expand = jnp.asarray(expand, BF16)
    shift = np.zeros((SSD_CONV - 1, ch, 2 * ch), np.float32)
    for j in range(1, SSD_CONV):
        shift[j - 1, np.arange(ch), ch + np.arange(ch) - j] = 1.0
    shift = jnp.asarray(shift, BF16)
    dskip_e = jnp.repeat(d_skip.astype(F32), SSD_HEAD_DIM)[None, :]
    consts = [conv_w, conv_b[None, :], pad_heads(dt_bias), pad_heads(a_log), dskip_e,
              norm_w[None, :], tril, expand, shift]
    in_specs = [
        pl.BlockSpec((step, SSD_CONV_DIM), tok(AB_XBC // SSD_CONV_DIM)),
        pl.BlockSpec((step, SSD_INNER), tok(AB_Z // SSD_INNER)),
        pl.BlockSpec((step, LANES), tok(AB_DT // LANES)),
    ] + [_resident(a.shape) for a in consts]
    return pl.pallas_call(
        _ssd_body,
        grid=(batch, nc),
        in_specs=in_specs,
        out_specs=pl.BlockSpec((step, SSD_INNER), tok(0)),
        out_shape=jax.ShapeDtypeStruct((batch * seq, SSD_INNER), BF16),
        scratch_shapes=[pltpu.VMEM((2 * ch, SSD_CONV_DIM), BF16),
                        pltpu.VMEM((SSD_GROUPS, SSD_STATE, SSD_GROUP_WIDTH), F32)],
        compiler_params=_params("parallel", "arbitrary"),
        name="ssd_mixer",
    )(h_ab, h_ab, h_ab, *consts)


def _s5_slot(g, s):
    return (s + g) % S5_TILE_GROUPS


def _s5_body(nblk, u_ref, m_ref, wst_ref, wofft_ref, a1_ref, a2_ref, o_ref, uscr_ref, yscr_ref, ug_ref, yg_ref):
    ng = S5_TILE_GROUPS
    ncol = S5_Q // ng
    nstrip = nblk // SUBLANES
    slot = lax.broadcasted_iota(jnp.int32, (SUBLANES, LANES), 1) >> int(math.log2(S5_GROUP))
    blk = lax.broadcasted_iota(jnp.int32, (nblk, 2 * S5_STATE), 0)

    def pitch_in(c, carry):
        src = pl.multiple_of(c * S5_Q, S5_Q)
        dst = pl.multiple_of(c * S5_PITCH, SUBLANES)
        uscr_ref[pl.ds(dst, S5_Q), :] = u_ref[pl.ds(src, S5_Q), :]
        return carry
    lax.fori_loop(0, nblk, pitch_in, 0, unroll=8)

    def gather(i, carry):
        base = pl.multiple_of(i * (SUBLANES * S5_PITCH), SUBLANES)
        row = pl.multiple_of(i * SUBLANES, SUBLANES)
        for m in range(ncol):
            rolled = []
            for s in range(ng):
                us = uscr_ref[pl.ds(base + ng * m + s, SUBLANES, stride=S5_PITCH), :]
                rolled.append(pltpu.roll(us, S5_GROUP * s, axis=1) if s else us)
            for g in range(ng):
                v = rolled[0]
                for s in range(1, ng):
                    v = jnp.where(slot == _s5_slot(g, s), rolled[s], v)
                ug_ref[pl.ds(row, SUBLANES), (g * ncol + m) * LANES:(g * ncol + m + 1) * LANES] = v
        return carry
    lax.fori_loop(0, nstrip, gather, 0, unroll=2)

    us = [ug_ref[:, g * S5_ROW:(g + 1) * S5_ROW].astype(BF16) for g in range(ng)]
    xs = [_dot(us[g], wst_ref[g]) for g in range(ng)]
    for k in range(int(math.log2(nblk))):
        sh = 1 << k
        for g in range(ng):
            prev = jnp.where(blk >= sh, pltpu.roll(xs[g], sh, axis=0), 0.0)
            xs[g] = (xs[g] + prev * a1_ref[g, k:k + 1, :]
                     + pltpu.roll(prev, S5_STATE, axis=1) * a2_ref[g, k:k + 1, :])
    for g in range(ng):
        x_in = jnp.where(blk >= 1, pltpu.roll(xs[g], 1, axis=0), 0.0)
        yg_ref[:, g * S5_ROW:(g + 1) * S5_ROW] = (_dot(us[g], m_ref[g])
                                                   + _dot_nt(x_in.astype(BF16), wofft_ref[g]))

    def scatter(i, carry):
        base = pl.multiple_of(i * (SUBLANES * S5_PITCH), SUBLANES)
        row = pl.multiple_of(i * SUBLANES, SUBLANES)
        for m in range(ncol):
            cols = [yg_ref[pl.ds(row, SUBLANES), (g * ncol + m) * LANES:(g * ncol + m + 1) * LANES]
                    for g in range(ng)]
            for tt in range(ng):
                z = cols[0]
                for g in range(1, ng):
                    z = jnp.where(slot == _s5_slot(g, tt), cols[g], z)
                sh = (LANES - S5_GROUP * tt) % LANES
                yscr_ref[pl.ds(base + ng * m + tt, SUBLANES, stride=S5_PITCH), :] = (
                    pltpu.roll(z, sh, axis=1) if sh else z)
        return carry
    lax.fori_loop(0, nstrip, scatter, 0, unroll=2)

    def pitch_out(c, carry):
        src = pl.multiple_of(c * S5_PITCH, SUBLANES)
        dst = pl.multiple_of(c * S5_Q, S5_Q)
        o_ref[pl.ds(dst, S5_Q), :] = yscr_ref[pl.ds(src, S5_Q), :]
        return carry
    lax.fori_loop(0, nblk, pitch_out, 0, unroll=8)


def _s5_core(h_ab, batch, seq, tables):
    nblk = seq // S5_Q
    ng = S5_TILE_GROUPS
    grp = lambda a: pl.BlockSpec((ng,) + a.shape[1:], lambda j, b: (j, 0, 0))
    return pl.pallas_call(
        functools.partial(_s5_body, nblk),
        grid=(S5_GROUPS // ng, batch),
        in_specs=[pl.BlockSpec((seq, LANES), lambda j, b: (b, AB_U // LANES + j))]
        + [grp(a) for a in tables],
        out_specs=pl.BlockSpec((seq, LANES), lambda j, b: (b, j)),
        out_shape=jax.ShapeDtypeStruct((batch * seq, S5_WIDTH), F32),
        scratch_shapes=[pltpu.VMEM((nblk * S5_PITCH, LANES), F32), pltpu.VMEM((nblk * S5_PITCH, LANES), F32),
                        pltpu.VMEM((nblk, ng * S5_ROW), F32), pltpu.VMEM((nblk, ng * S5_ROW), F32)],
        compiler_params=_params("parallel", "parallel"),
        name="s5_core",
    )(h_ab, *tables)


def _s5_table_body(kt_ref, pw_rr_ref, pw_ii_ref, b1_ref, b2_ref, c1_ref, c2_ref, m_ref, wst_ref, wofft_ref):
    ng = S5_TILE_GROUPS
    ncol = S5_Q // ng
    lane = lax.broadcasted_iota(jnp.int32, (S5_GROUP, LANES), 1)
    for g in range(ng):
        strip = [jnp.zeros((S5_GROUP, LANES), F32)] * ncol + [kt_ref[g, :, c * LANES:(c + 1) * LANES]
                                                            for c in range(ncol)]
        b1, b2, c1, c2 = b1_ref[g], b2_ref[g], c1_ref[g], c2_ref[g]
        for s in range(S5_Q):
            start = ncol * LANES - S5_GROUP * s
            a, sh = start // LANES, start % LANES
            k = ng * (s // ng) + _s5_slot(g, s % ng)
            rows = slice(S5_GROUP * k, S5_GROUP * (k + 1))
            for c in range(ncol):
                if sh:
                    w = jnp.where(lane < LANES - sh, pltpu.roll(strip[a + c], LANES - sh, axis=1),
                                  pltpu.roll(strip[a + c + 1], LANES - sh, axis=1))
                else:
                    w = strip[a + c]
                if g:
                    w = pltpu.roll(w, S5_GROUP * g, axis=1)
                m_ref[g, rows, c * LANES:(c + 1) * LANES] = w.astype(BF16)
            e = S5_Q - 1 - s
            wst_ref[g, rows, :] = (pw_rr_ref[g, e:e + 1, :] * b1 + pw_ii_ref[g, e:e + 1, :] * b2).astype(BF16)
            wofft_ref[g, rows, :] = (pw_rr_ref[g, s + 1:s + 2, :] * c1
                                     + pw_ii_ref[g, s + 1:s + 2, :] * c2).astype(BF16)


def _s5_tables(lam_re, lam_im, log_dt, b_re, b_im, c_re, c_im, nblk):
    q, ng = S5_Q, S5_TILE_GROUPS
    hp = lax.Precision.HIGHEST
    lam = lax.complex(lam_re.astype(F32), lam_im.astype(F32))
    ldt = lam * jnp.exp(log_dt.astype(F32))[:, None]
    lam_bar = jnp.exp(ldt)
    b_bar = ((lam_bar - 1.0) / lam)[..., None] * lax.complex(b_re.astype(F32), b_im.astype(F32))
    c = lax.complex(c_re.astype(F32), c_im.astype(F32))
    tau = jnp.arange(q + 1, dtype=F32)
    pw = jnp.exp(ldt[:, None, :] * tau[None, :, None])
    kt = jnp.real(jnp.einsum('gpn,gtn,gnq->gqtp', c, pw[:, :q], b_bar, precision=hp)).reshape(
        S5_GROUPS, S5_GROUP, S5_ROW)
    halves = lambda lo, hi: jnp.concatenate([lo, hi], axis=-1)
    bt = jnp.transpose(b_bar, (0, 2, 1))
    small = [halves(jnp.real(pw), jnp.real(pw)), halves(jnp.imag(pw), jnp.imag(pw)),
             halves(jnp.real(bt), jnp.imag(bt)), halves(-jnp.imag(bt), jnp.real(bt)),
             halves(jnp.real(c), -jnp.imag(c)), halves(-jnp.imag(c), -jnp.real(c))]
    grp = lambda shape: pl.BlockSpec((ng,) + shape, lambda j: (j, 0, 0))
    shapes = [(S5_ROW, S5_ROW), (S5_ROW, 2 * S5_STATE), (S5_ROW, 2 * S5_STATE)]
    m, wst, wofft = pl.pallas_call(
        _s5_table_body,
        grid=(S5_GROUPS // ng,),
        in_specs=[grp(kt.shape[1:])] + [grp(a.shape[1:]) for a in small],
        out_specs=[grp(sh) for sh in shapes],
        out_shape=[jax.ShapeDtypeStruct((S5_GROUPS,) + sh, BF16) for sh in shapes],
        compiler_params=_params("parallel"),
        name="s5_tables",
    )(kt, *small)
    nlev = int(math.log2(nblk))
    step = jnp.exp(ldt[:, None, :] * (q * 2.0 ** jnp.arange(nlev, dtype=F32))[None, :, None])
    a1 = halves(jnp.real(step), jnp.real(step))
    a2 = halves(-jnp.imag(step), jnp.imag(step))
    return m, wst, wofft, a1, a2


def _s5_post_body(yc_ref, u_ref, d_ref, w_ref, b_ref, o_ref):
    for r in range(yc_ref.shape[0] // SUB_ROWS):
        rows = slice(r * SUB_ROWS, (r + 1) * SUB_ROWS)
        y = _gelu_tanh(yc_ref[rows, :] + d_ref[...] * u_ref[rows, :])
        gate = _sigmoid(_dot(y.astype(BF16), w_ref[...]) + b_ref[...])
        o_ref[rows, :] = (y * gate).astype(o_ref.dtype)


def _s5_post(y_conv, h_ab, d_skip, w_glu, b_glu):
    t = y_conv.shape[0]
    consts = [d_skip, w_glu, b_glu]
    return pl.pallas_call(
        _s5_post_body,
        grid=(t // TM,),
        in_specs=[pl.BlockSpec((TM, S5_WIDTH), lambda i: (i, 0)),
                  pl.BlockSpec((TM, S5_WIDTH), lambda i: (i, AB_U // S5_WIDTH))]
        + [_resident(a.shape) for a in consts],
        out_specs=pl.BlockSpec((TM, S5_WIDTH), lambda i: (i, 0)),
        out_shape=jax.ShapeDtypeStruct((t, S5_WIDTH), BF16),
        compiler_params=_params("parallel"),
        name="s5_post",
    )(y_conv, h_ab, *consts)


def _lin_levels(c):
    return [c >> (i + 1) for i in range(int(math.log2(c)))]


def _lin_tables(c):
    t = np.arange(c)[:, None]
    j = np.arange(c)[None, :]
    lvl = np.full((c, c), -1, np.int32)
    lvl[np.arange(c), np.arange(c)] = 0
    sgn = []
    for i, b in enumerate(_lin_levels(c)):
        same = (t // (2 * b)) == (j // (2 * b))
        second = (t % (2 * b)) >= b
        lvl[same & second & ((j % (2 * b)) < b)] = i + 1
        sgn.append(np.broadcast_to(np.where(second, 1.0, -1.0) * math.log2(math.e), (c, LANES)))
    return (jnp.asarray((j <= t).astype(np.float32), BF16), jnp.asarray(lvl),
            jnp.asarray(np.stack(sgn), F32))


def _gate_factors(g, tril, sgn_ref, gcum_ref):
    c = LIN_CHUNK
    gcum = _sel_dot(tril, g)
    gcum_ref[...] = gcum
    pos = lax.broadcasted_iota(jnp.int32, (c, LANES), 0)
    factors = []
    for i, b in enumerate(_lin_levels(c)):
        if 2 * b >= SUBLANES:
            mids = [jnp.broadcast_to(gcum_ref[blk * 2 * b + b - 1:blk * 2 * b + b, :], (2 * b, LANES))
                    for blk in range(c // (2 * b))]
            gmid = mids[0] if len(mids) == 1 else jnp.concatenate(mids, axis=0)
        elif b == 2:
            p4 = pos & 3
            gmid = jnp.where(p4 == 0, pltpu.roll(gcum, c - 1, axis=0),
                             jnp.where(p4 == 1, gcum,
                                       jnp.where(p4 == 2, pltpu.roll(gcum, 1, axis=0),
                                                 pltpu.roll(gcum, 2, axis=0))))
        else:
            gmid = jnp.where((pos & 1) == 1, pltpu.roll(gcum, 1, axis=0), gcum)
        factors.append(jnp.exp2((gcum - gmid) * sgn_ref[i]).astype(BF16))
    e_cum = jnp.exp(gcum)
    return e_cum, e_cum.astype(BF16), jnp.exp(gcum[c - 1:c, :] - gcum).astype(BF16), factors


def _lin_body(h_ref, gqk_ref, gv_ref, gr_ref, glr_ref, loglb_ref, log1mlb_ref, onemlb_ref, hnorm_ref,
              wup_ref, bgate_ref, gnorm_ref, tril_ref, lvl_ref, sgn_ref, o_ref, state_ref, gcum_ref):
    c = LIN_CHUNK

    @pl.when(pl.program_id(1) == 0)
    def _():
        state_ref[...] = jnp.zeros(state_ref.shape, F32)

    def chunk(ci, carry):
        rows = pl.ds(pl.multiple_of(ci * c, c), c)

        hq = h_ref[rows, 0:HGRN_WIDTH]
        hf = h_ref[rows, HGRN_WIDTH:2 * HGRN_WIDTH]
        lo = loglb_ref[...]
        hi_ = log1mlb_ref[...] + _log_sigmoid(hf)
        log_f = jnp.maximum(lo, hi_) + jnp.log1p(jnp.exp(-jnp.abs(lo - hi_)))
        q_c = _silu(hq)
        k_c = onemlb_ref[...] * _sigmoid(-hf)
        pre = _dot(glr_ref[rows, :].astype(BF16), wup_ref[...]) + bgate_ref[...]
        log_a = _log_sigmoid(pre) * (1.0 / GLA_TAU)
        q_d = gqk_ref[rows, 0:GLA_QK] * (GLA_DK ** -0.5)
        k_d = gqk_ref[rows, GLA_QK:2 * GLA_QK]

        lvl = lvl_ref[...]
        lane = lax.broadcasted_iota(jnp.int32, (c, LANES), 1)
        tril = tril_ref[...]

        heads = []
        for hd in range(LIN_HEADS):
            ls = slice(hd * HEAD_W, (hd + 1) * HEAD_W)
            heads.append((hd, q_c[:, ls], k_c[:, ls], hd,
                          h_ref[rows, 2 * HGRN_WIDTH + hd * HEAD_W:2 * HGRN_WIDTH + (hd + 1) * HEAD_W],
                          hnorm_ref[:, ls],
                          h_ref[rows, 3 * HGRN_WIDTH + hd * HEAD_W:3 * HGRN_WIDTH + (hd + 1) * HEAD_W], ls))
        for hd in range(LIN_HEADS):
            tile, half = divmod(hd, LANES // GLA_DK)
            ts = slice(tile * LANES, (tile + 1) * LANES)
            ls = slice(hd * HEAD_W, (hd + 1) * HEAD_W)
            mine = (lane >> int(math.log2(GLA_DK))) == half
            heads.append((LIN_HEADS + hd, jnp.where(mine, q_d[:, ts], 0.0), jnp.where(mine, k_d[:, ts], 0.0),
                          LIN_HEADS + tile, gv_ref[rows, ls], gnorm_ref[:, ls], gr_ref[rows, ls],
                          slice(HGRN_WIDTH + hd * HEAD_W, HGRN_WIDTH + (hd + 1) * HEAD_W)))

        gates = [log_f[:, t * LANES:(t + 1) * LANES] for t in range(LIN_HEADS)]
        gates += [log_a[:, t * LANES:(t + 1) * LANES] for t in range(GLA_QK // LANES)]
        factors = [_gate_factors(g, tril, sgn_ref, gcum_ref.at[t]) for t, g in enumerate(gates)]
        qkb = [(q.astype(BF16), k.astype(BF16)) for _, q, k, *_ in heads]
        attns = []
        for (qb, kb), head in zip(qkb, heads):
            e_lvl = factors[head[3]][3]
            attn = jnp.where(lvl == 0, _dot_nt(qb, kb), 0.0)
            for i, e in enumerate(e_lvl):
                attn = jnp.where(lvl == i + 1, _dot_nt(qb * e, kb * e), attn)
            attns.append(attn.astype(BF16))
        for (qb, kb), attn, (idx, _, _, tile, v, norm_w, gate, cols) in zip(qkb, attns, heads):
            e_cum, e_cum_b, e_end_b, _ = factors[tile]
            vb = v.astype(BF16)
            state_t = state_ref[idx]
            o = _dot(attn, vb) + _dot_nt(qb * e_cum_b, state_t.astype(BF16))
            state_ref[idx] = state_t * e_cum[c - 1:c, :] + _dot_tn(vb, kb * e_end_b)
            o_ref[rows, cols] = (_rms_norm(o, norm_w) * _silu(gate)).astype(o_ref.dtype)
        return carry

    lax.fori_loop(0, h_ref.shape[0] // c, chunk, 0)


def _lin_mixer(h_cd, batch, seq, lb, hgrn_norm_w, gla_w_gate_up, gla_b_gate, gla_norm_w):
    c = LIN_CHUNK
    step = c * MIX_STEP_CHUNKS
    nc = seq // step
    tok = lambda cb: (lambda b, i: (b * nc + i, cb))
    lb = lb.astype(F32)[None, :]
    wup = jnp.zeros((LANES, GLA_QK), F32).at[:GLA_RANK].set(gla_w_gate_up).astype(BF16)
    tril, lvl, sgn = _lin_tables(c)
    consts = [jnp.log(lb), jnp.log1p(-lb), 1.0 - lb, hgrn_norm_w[None, :], wup, gla_b_gate[None, :],
              gla_norm_w[None, :], tril, lvl, sgn]
    in_specs = [
        pl.BlockSpec((step, 4 * HGRN_WIDTH), tok(CD_H // (4 * HGRN_WIDTH))),
        pl.BlockSpec((step, 2 * GLA_QK), tok(CD_GQK // (2 * GLA_QK))),
        pl.BlockSpec((step, GLA_WIDTH), tok(CD_GV // GLA_WIDTH)),
        pl.BlockSpec((step, GLA_WIDTH), tok(CD_GR // GLA_WIDTH)),
        pl.BlockSpec((step, LANES), tok(CD_GLR // LANES)),
    ] + [_resident(a.shape) for a in consts]
    return pl.pallas_call(
        _lin_body,
        grid=(batch, nc),
        in_specs=in_specs,
        out_specs=pl.BlockSpec((step, HGRN_WIDTH + GLA_WIDTH), tok(0)),
        out_shape=jax.ShapeDtypeStruct((batch * seq, HGRN_WIDTH + GLA_WIDTH), BF16),
        scratch_shapes=[pltpu.VMEM((2 * LIN_HEADS, HEAD_W, HEAD_W), F32),
                        pltpu.VMEM((LIN_HEADS + GLA_QK // LANES, c, LANES), F32)],
        compiler_params=_params("parallel", "arbitrary"),
        name="lin_mixer",
    )(h_cd, h_cd, h_cd, h_cd, h_cd, *consts)


def _pack_cols(w, splits, order, total):
    bounds = np.concatenate([[0], np.cumsum(splits)])
    parts = [w[:, bounds[f]:bounds[f + 1]] for f in order]
    parts.append(jnp.zeros((w.shape[0], total - sum(splits[f] for f in order)), w.dtype))
    return jnp.concatenate(parts, axis=1).astype(BF16)


def _hgrn_lower_bound(lb_logits, layer):
    cum = jnp.cumsum(jax.nn.softmax(lb_logits.astype(F32), axis=0), axis=0)
    return cum[layer] - cum[0]


def kernel(x, p, ln_g, ln_b, ffn_w_gate, ffn_w_up, ffn_w_down, ple_w_gate, ple_w_proj, ab_w_in, ab_w_out,
           ssd_conv_w, ssd_conv_b, ssd_dt_bias, ssd_a_log, ssd_d, ssd_norm_w, s5_lambda_re, s5_lambda_im,
           s5_log_dt, s5_b_re, s5_b_im, s5_c_re, s5_c_im, s5_d, s5_w_glu, s5_b_glu, cd_w_in, cd_w_out,
           hgrn_lb_logits, hgrn_norm_w, gla_w_gate_up, gla_b_gate, gla_norm_w):
    batch, seq, _ = x.shape
    t = batch * seq
    x = x.reshape(t, D_MODEL)
    bf = lambda w: w.astype(BF16)
    ab_splits = (SSD_INNER, SSD_CONV_DIM, SSD_HEADS, S5_WIDTH)
    cd_splits = (HGRN_WIDTH, HGRN_WIDTH, HGRN_WIDTH, HGRN_WIDTH, GLA_QK, GLA_QK, GLA_WIDTH, GLA_RANK,
                 GLA_WIDTH)
    wg, wu, wd = bf(ffn_w_gate), bf(ffn_w_up), bf(ffn_w_down)
    pg, pp = bf(ple_w_gate), bf(ple_w_proj)
    p = p.reshape(DEPTH, t, PLE_DIM)
    ln_g = ln_g.reshape(DEPTH * 3, 1, D_MODEL)
    ln_b = ln_b.reshape(DEPTH * 3, 1, D_MODEL)
    w_ab_out, w_cd_out = bf(ab_w_out), bf(cd_w_out)
    for i in range(DEPTH):
        j = i // 2
        x = _ffn_ln(x, wg, wu, wd, ln_g, ln_b, i, 0, 3 * i)
        if i % 2 == 0:
            w_in = _pack_cols(ab_w_in[j], ab_splits, [1, 0, 3, 2], AB_PACK)
            h_ab = _proj_in(x, w_in)
            y_a = _ssd_mixer(h_ab, batch, seq, ssd_conv_w[j], ssd_conv_b[j], ssd_dt_bias[j], ssd_a_log[j],
                             ssd_d[j], ssd_norm_w[j])
            tables = _s5_tables(s5_lambda_re[j], s5_lambda_im[j], s5_log_dt[j], s5_b_re[j], s5_b_im[j],
                                s5_c_re[j], s5_c_im[j], seq // S5_Q)
            y_conv = _s5_core(h_ab, batch, seq, tables)
            y_b = _s5_post(y_conv, h_ab, s5_d[j].reshape(1, S5_WIDTH).astype(F32), bf(s5_w_glu[j]),
                           s5_b_glu[j][None, :])
            x = _proj_ln(x, [y_a, y_b], w_ab_out, j, ln_g, ln_b, 3 * i + 1)
        else:
            w_in = _pack_cols(cd_w_in[j], cd_splits, [0, 1, 2, 3, 4, 5, 6, 8, 7], CD_PACK)
            h_cd = _proj_in(x, w_in)
            o_cd = _lin_mixer(h_cd, batch, seq, _hgrn_lower_bound(hgrn_lb_logits, i), hgrn_norm_w[j],
                              gla_w_gate_up[j], gla_b_gate[j], gla_norm_w[j])
            x = _proj_ln(x, [o_cd], w_cd_out, j, ln_g, ln_b, 3 * i + 1)
        x = _ffn_ln(x, wg, wu, wd, ln_g, ln_b, i, 1, 3 * i + 2, ple=(p, pg, pp))
    return x.reshape(batch, seq, D_MODEL)
```

```python
import functools
import math

import jax
import jax.numpy as jnp
import numpy as np
from jax import lax
from jax.experimental import pallas as pl
from jax.experimental.pallas import tpu as pltpu

F32 = jnp.float32
BF16 = jnp.bfloat16

D_MODEL = 1024
D_FF = 2816
PLE_DIM = 256
DEPTH = 2
DN_ALPHA = (2.0 * DEPTH) ** 0.25
LN_EPS = 1e-5
NEG_BIG = -1e30
SSD_HEADS = 16
SSD_HEAD_DIM = 64
SSD_GROUPS = 4
SSD_STATE = 128
SSD_CONV = 4
SSD_INNER = 1024
SSD_BC = SSD_GROUPS * SSD_STATE
SSD_CONV_DIM = SSD_INNER + 2 * SSD_BC
SSD_GROUP_WIDTH = SSD_INNER // SSD_GROUPS
SSD_HEADS_PER_GROUP = SSD_HEADS // SSD_GROUPS
S5_WIDTH = 1024
S5_GROUPS = 64
S5_GROUP = 16
S5_STATE = 64
LIN_HEADS = 4
HGRN_WIDTH = 512
GLA_DK = 64
GLA_QK = LIN_HEADS * GLA_DK
GLA_WIDTH = 512
GLA_RANK = 16
GLA_TAU = 16.0
HEAD_W = 128

LANES = 128
SUBLANES = 8
VMEM_LIMIT = 56 * 1024 * 1024

TM = 512
TM_FFN = 1024
SUB_FFN = 512
FF_CHUNK = 256
SSD_CHUNK = 128
MIX_STEP_CHUNKS = 4
LIN_CHUNK = 128
S5_Q = 32
S5_ROW = S5_Q * S5_GROUP
S5_TILE_GROUPS = LANES // S5_GROUP
S5_PITCH = 40

AB_XBC, AB_Z, AB_U, AB_DT, AB_PACK = 0, 2048, 3072, 4096, 4224
CD_H, CD_GQK, CD_GV, CD_GR, CD_GLR, CD_PACK = 0, 2048, 2560, 3072, 3584, 3712


def _resident(shape):
    n = len(shape)
    return pl.BlockSpec(shape, lambda *_: (0,) * n, pipeline_mode=pl.Buffered(1))


def _dot(a, b):
    return jnp.dot(a, b, preferred_element_type=F32)


def _dot_nt(a, b):
    return lax.dot_general(a, b, (((1,), (1,)), ((), ())), preferred_element_type=F32)


def _dot_tn(a, b):
    return lax.dot_general(a, b, (((0,), (0,)), ((), ())), preferred_element_type=F32)


def _split3(v):
    hi = v.astype(BF16)
    r = v - hi.astype(F32)
    mid = r.astype(BF16)
    lo = (r - mid.astype(F32)).astype(BF16)
    return hi, mid, lo


def _sel_dot(sel, v):
    hi, mid, lo = _split3(v)
    return _dot(sel, hi) + _dot(sel, mid) + _dot(sel, lo)


def _dot_sel(v, sel):
    hi, mid, lo = _split3(v)
    return _dot(hi, sel) + _dot(mid, sel) + _dot(lo, sel)


def _sigmoid(x):
    return 1.0 / (1.0 + jnp.exp(-x))


def _silu(x):
    return x * _sigmoid(x)


def _log_sigmoid(x):
    return jnp.minimum(x, 0.0) - jnp.log1p(jnp.exp(-jnp.abs(x)))


def _softplus(x):
    return jnp.maximum(x, 0.0) + jnp.log1p(jnp.exp(-jnp.abs(x)))


def _gelu_tanh(x):
    return 0.5 * x * (1.0 + jnp.tanh(math.sqrt(2.0 / math.pi) * (x + 0.044715 * (x * x * x))))


def _layer_norm(y, g, b):
    mu = jnp.mean(y, axis=-1, keepdims=True)
    yc = y - mu
    var = jnp.mean(yc * yc, axis=-1, keepdims=True)
    return yc * lax.rsqrt(var + LN_EPS) * g + b


def _rms_norm(y, w):
    return y * lax.rsqrt(jnp.mean(y * y, axis=-1, keepdims=True) + LN_EPS) * w


def _params(*sem):
    return pltpu.CompilerParams(dimension_semantics=sem, vmem_limit_bytes=VMEM_LIMIT)


def _ffn_body(mix, with_ple, x_ref, *refs):
    refs = list(refs)
    take = lambda n: [refs.pop(0) for _ in range(n)]
    if mix == "ab":
        ya_ref, ys_ref, wout_ref, wglu_ref, bglu_ref, g1_ref, b1_ref = take(7)
    elif mix == "cd":
        oc_ref, wout_ref, g1_ref, b1_ref = take(4)
    wg_ref, wu_ref, wd_ref, g_ref, b_ref = take(5)
    if with_ple:
        p_ref, pg_ref, pp_ref = take(3)
    o_ref, acc_ref = refs
    for r in range(x_ref.shape[0] // SUB_FFN):
        rows = slice(r * SUB_FFN, (r + 1) * SUB_FFN)
        x = x_ref[rows, :]
        if mix == "ab":
            y = _gelu_tanh(ys_ref[rows, :].astype(F32))
            gate = _sigmoid(_dot(y.astype(BF16), wglu_ref[...]) + bglu_ref[...])
            m = _dot(ya_ref[rows, :], wout_ref[0:SSD_INNER, :]) + _dot((y * gate).astype(BF16), wout_ref[SSD_INNER:, :])
            x = _layer_norm(DN_ALPHA * x + m, g1_ref[...], b1_ref[...])
        elif mix == "cd":
            x = _layer_norm(DN_ALPHA * x + _dot(oc_ref[rows, :], wout_ref[...]), g1_ref[...], b1_ref[...])
        xb = x.astype(BF16)
        for c in range(D_FF // FF_CHUNK):
            sl = slice(c * FF_CHUNK, (c + 1) * FF_CHUNK)
            gate = _dot(xb, wg_ref[:, sl])
            up = _dot(xb, wu_ref[:, sl])
            h = (_silu(gate) * up).astype(BF16)
            d = _dot(h, wd_ref[sl, :])
            if c == 0:
                acc_ref[rows, :] = d
            else:
                acc_ref[rows, :] += d
        y = _layer_norm(DN_ALPHA * x + 0.5 * acc_ref[rows, :], g_ref[...], b_ref[...])
        if with_ple:
            gate = _sigmoid(_dot(y.astype(BF16), pg_ref[...]))
            y = y + gate * _dot(p_ref[rows, :].astype(BF16), pp_ref[...])
        o_ref[rows, :] = y


def _pick(a, *idx):
    rest = a.shape[len(idx):]
    return pl.BlockSpec((None,) * len(idx) + rest, lambda *_: idx + (0,) * len(rest),
                        pipeline_mode=pl.Buffered(1))


def _ffn_ln(x, wg, wu, wd, ln_g, ln_b, layer, pos, ln_idx, mix=None, ple=None):
    t = x.shape[0]
    row = lambda i: (i, 0)
    in_specs = [pl.BlockSpec((TM_FFN, D_MODEL), row)]
    args = [x]
    if mix is not None:
        kind, acts, consts = mix
        in_specs += [pl.BlockSpec((TM_FFN, a.shape[1]), row) for a in acts]
        in_specs += [_pick(a, *idx) for a, idx in consts]
        in_specs += [_pick(ln_g, ln_idx - 1), _pick(ln_b, ln_idx - 1)]
        args += list(acts) + [a for a, _ in consts] + [ln_g, ln_b]
    in_specs += [_pick(wg, layer, pos), _pick(wu, layer, pos), _pick(wd, layer, pos),
                 _pick(ln_g, ln_idx), _pick(ln_b, ln_idx)]
    args += [wg, wu, wd, ln_g, ln_b]
    if ple is not None:
        p, pg, pp = ple
        in_specs += [pl.BlockSpec((None, TM_FFN, PLE_DIM), lambda i: (layer, i, 0)), _pick(pg, layer), _pick(pp, layer)]
        args += [p, pg, pp]
    return pl.pallas_call(
        functools.partial(_ffn_body, None if mix is None else mix[0], ple is not None),
        grid=(t // TM_FFN,),
        in_specs=in_specs,
        out_specs=pl.BlockSpec((TM_FFN, D_MODEL), row),
        out_shape=jax.ShapeDtypeStruct((t, D_MODEL), F32),
        scratch_shapes=[pltpu.VMEM((TM_FFN, D_MODEL), F32)],
        compiler_params=_params("parallel"),
        name="ffn_ln" if mix is None else "mix_ffn_ln_ple",
    )(*args)


def _proj_in_body(n_out, x_ref, w_ref, o_ref):
    xb = x_ref[...].astype(BF16)
    step = 512
    for c0 in range(0, n_out, step):
        c1 = min(c0 + step, n_out)
        o_ref[:, c0:c1] = _dot(xb, w_ref[:, c0:c1])


def _proj_in(x, w):
    t, n_out = x.shape[0], w.shape[1]
    return pl.pallas_call(
        functools.partial(_proj_in_body, n_out),
        grid=(t // TM,),
        in_specs=[pl.BlockSpec((TM, D_MODEL), lambda i: (i, 0)), _resident(w.shape)],
        out_specs=pl.BlockSpec((TM, n_out), lambda i: (i, 0)),
        out_shape=jax.ShapeDtypeStruct((t, n_out), F32),
        compiler_params=_params("parallel"),
        name="proj_in",
    )(x, w)


def _ssd_body(xbc_ref, z_ref, dt_ref, convw_ref, convb_ref, dtb_ref, alog_ref, dskip_ref,
              normw_ref, tril_ref, expand_ref, shift_ref, o_ref, xb_ref, state_ref):
    ch = SSD_CHUNK
    gw = SSD_GROUP_WIDTH

    @pl.when(pl.program_id(1) == 0)
    def _():
        xb_ref[0:ch, :] = jnp.zeros((ch, SSD_CONV_DIM), BF16)
        state_ref[...] = jnp.zeros(state_ref.shape, F32)

    def chunk(i, carry):
        rows = pl.ds(pl.multiple_of(i * ch, ch), ch)

        def conv_silu(cols):
            cur = xbc_ref[rows, cols]
            cur_b = cur.astype(BF16)
            xb_ref[ch:2 * ch, cols] = cur_b
            both = xb_ref[:, cols]
            acc = cur * convw_ref[SSD_CONV - 1:SSD_CONV, cols] + convb_ref[:, cols]
            for j in range(1, SSD_CONV):
                acc = acc + _dot(shift_ref[j - 1], both) * convw_ref[SSD_CONV - 1 - j:SSD_CONV - j, cols]
            xb_ref[0:ch, cols] = cur_b
            return _silu(acc)

        dt = _softplus(dt_ref[rows, :] + dtb_ref[...])
        da = dt * (-jnp.exp(alog_ref[...]))
        a_cum = _sel_dot(tril_ref[...], da)
        a_cum_t = a_cum.T
        dt_hi, dt_mid, _ = _split3(dt)
        dt_e = _dot(dt_hi, expand_ref[...]) + _dot(dt_mid, expand_ref[...])
        ac_e = _dot_sel(a_cum, expand_ref[...])
        ac_last = ac_e[ch - 1:ch, :]
        to_end = jnp.exp(ac_last - ac_e)
        carry_scale = jnp.exp(ac_e)
        chunk_decay = jnp.exp(ac_last)

        row = lax.broadcasted_iota(jnp.int32, (ch, ch), 0)
        col = lax.broadcasted_iota(jnp.int32, (ch, ch), 1)
        causal = col <= row
        lane_head = lax.broadcasted_iota(jnp.int32, (ch, gw), 1) >> int(math.log2(SSD_HEAD_DIM))

        groups = range(SSD_GROUPS)
        cols = [slice(g * gw, (g + 1) * gw) for g in groups]
        xs = [conv_silu(cols[g]) for g in groups]
        bc = [conv_silu(slice(SSD_INNER + k * gw, SSD_INNER + (k + 1) * gw)).astype(BF16)
              for k in range(2 * SSD_BC // gw)]
        bc = [v[:, half * SSD_STATE:(half + 1) * SSD_STATE] for v in bc for half in range(gw // SSD_STATE)]
        bs, cs = bc[:SSD_GROUPS], bc[SSD_GROUPS:]
        xdt = [xs[g] * dt_e[:, cols[g]] for g in groups]
        xdt_b = [v.astype(BF16) for v in xdt]
        scores = [_dot_nt(cs[g], bs[g]).astype(BF16) for g in groups]
        ys = []
        for g in groups:
            y_g = jnp.zeros((ch, gw), F32)
            for hh in range(SSD_HEADS_PER_GROUP):
                h = g * SSD_HEADS_PER_GROUP + hh
                seg = a_cum[:, h:h + 1] - a_cum_t[h:h + 1, :]
                decay = jnp.exp(jnp.where(causal, seg, NEG_BIG)).astype(BF16)
                full = _dot(scores[g] * decay, xdt_b[g])
                y_g = jnp.where(lane_head == hh, full, y_g)
            ys.append(y_g)
        for g in groups:
            gs = cols[g]
            state = state_ref[g]
            y_g = ys[g] + _dot(cs[g], state.astype(BF16)) * carry_scale[:, gs]
            state_ref[g] = state * chunk_decay[:, gs] + _dot_tn(bs[g], (xdt[g] * to_end[:, gs]).astype(BF16))
            y_g = y_g + xs[g] * dskip_ref[:, gs]
            y_g = y_g * _silu(z_ref[rows, gs])
            o_ref[rows, gs] = _rms_norm(y_g, normw_ref[:, gs]).astype(o_ref.dtype)
        return carry

    lax.fori_loop(0, xbc_ref.shape[0] // ch, chunk, 0)


def _ssd_mixer(h_ab, batch, seq, conv_w, conv_b, dt_bias, a_log, d_skip, norm_w):
    ch = SSD_CHUNK
    step = ch * MIX_STEP_CHUNKS
    nc = seq // step
    tok = lambda cb: (lambda b, c: (b * nc + c, cb))
    pad_heads = lambda v: jnp.zeros((1, LANES), F32).at[0, :SSD_HEADS].set(v.astype(F32))
    tril = jnp.asarray(np.tril(np.ones((ch, ch), np.float32)), BF16)
    expand = np.zeros((LANES, SSD_INNER), np.float32)
    for h in range(SSD_HEADS):
        expand[h, h * SSD_HEAD_DIM:(h + 1) * SSD_HEAD_DIM] = 1.0
    expand = jnp.asarray(expand, BF16)
    shift = np.zeros((SSD_CONV - 1, ch, 2 * ch), np.float32)
    for j in range(1, SSD_CONV):
        shift[j - 1, np.arange(ch), ch + np.arange(ch) - j] = 1.0
    shift = jnp.asarray(shift, BF16)
    dskip_e = jnp.repeat(d_skip.astype(F32), SSD_HEAD_DIM)[None, :]
    consts = [conv_w, conv_b[None, :], pad_heads(dt_bias), pad_heads(a_log), dskip_e,
              norm_w[None, :], tril, expand, shift]
    in_specs = [
        pl.BlockSpec((step, SSD_CONV_DIM), tok(AB_XBC // SSD_CONV_DIM)),
        pl.BlockSpec((step, SSD_INNER), tok(AB_Z // SSD_INNER)),
        pl.BlockSpec((step, LANES), tok(AB_DT // LANES)),
    ] + [_resident(a.shape) for a in consts]
    return pl.pallas_call(
        _ssd_body,
        grid=(batch, nc),
        in_specs=in_specs,
        out_specs=pl.BlockSpec((step, SSD_INNER), tok(0)),
        out_shape=jax.ShapeDtypeStruct((batch * seq, SSD_INNER), BF16),
        scratch_shapes=[pltpu.VMEM((2 * ch, SSD_CONV_DIM), BF16),
                        pltpu.VMEM((SSD_GROUPS, SSD_STATE, SSD_GROUP_WIDTH), F32)],
        compiler_params=_params("parallel", "arbitrary"),
        name="ssd_mixer",
    )(h_ab, h_ab, h_ab, *consts)


def _s5_slot(g, s):
    return (s + g) % S5_TILE_GROUPS


def _s5_body(nblk, u_ref, d_ref, m_ref, wst_ref, wofft_ref, a1_ref, a2_ref, o_ref, uscr_ref, yscr_ref, ug_ref,
             yg_ref):
    ng = S5_TILE_GROUPS
    ncol = S5_Q // ng
    nstrip = nblk // SUBLANES
    slot = lax.broadcasted_iota(jnp.int32, (SUBLANES, LANES), 1) >> int(math.log2(S5_GROUP))
    blk = lax.broadcasted_iota(jnp.int32, (nblk, 2 * S5_STATE), 0)

    def pitch_in(c, carry):
        src = pl.multiple_of(c * S5_Q, S5_Q)
        dst = pl.multiple_of(c * S5_PITCH, SUBLANES)
        uscr_ref[pl.ds(dst, S5_Q), :] = u_ref[pl.ds(src, S5_Q), :]
        return carry
    lax.fori_loop(0, nblk, pitch_in, 0, unroll=8)

    def gather(i, carry):
        base = pl.multiple_of(i * (SUBLANES * S5_PITCH), SUBLANES)
        row = pl.multiple_of(i * SUBLANES, SUBLANES)
        for m in range(ncol):
            rolled = []
            for s in range(ng):
                us = uscr_ref[pl.ds(base + ng * m + s, SUBLANES, stride=S5_PITCH), :]
                rolled.append(pltpu.roll(us, S5_GROUP * s, axis=1) if s else us)
            for g in range(ng):
                v = rolled[0]
                for s in range(1, ng):
                    v = jnp.where(slot == _s5_slot(g, s), rolled[s], v)
                ug_ref[pl.ds(row, SUBLANES), (g * ncol + m) * LANES:(g * ncol + m + 1) * LANES] = v
        return carry
    lax.fori_loop(0, nstrip, gather, 0, unroll=2)

    us = [ug_ref[:, g * S5_ROW:(g + 1) * S5_ROW].astype(BF16) for g in range(ng)]
    xs = [_dot(us[g], wst_ref[g]) for g in range(ng)]
    for k in range(int(math.log2(nblk))):
        sh = 1 << k
        for g in range(ng):
            prev = jnp.where(blk >= sh, pltpu.roll(xs[g], sh, axis=0), 0.0)
            xs[g] = (xs[g] + prev * a1_ref[g, k:k + 1, :]
                     + pltpu.roll(prev, S5_STATE, axis=1) * a2_ref[g, k:k + 1, :])
    for g in range(ng):
        x_in = jnp.where(blk >= 1, pltpu.roll(xs[g], 1, axis=0), 0.0)
        yg_ref[:, g * S5_ROW:(g + 1) * S5_ROW] = (_dot(us[g], m_ref[g])
                                                   + _dot_nt(x_in.astype(BF16), wofft_ref[g]))

    def scatter(i, carry):
        base = pl.multiple_of(i * (SUBLANES * S5_PITCH), SUBLANES)
        row = pl.multiple_of(i * SUBLANES, SUBLANES)
        for m in range(ncol):
            cols = [yg_ref[pl.ds(row, SUBLANES), (g * ncol + m) * LANES:(g * ncol + m + 1) * LANES]
                    for g in range(ng)]
            for tt in range(ng):
                z = cols[0]
                for g in range(1, ng):
                    z = jnp.where(slot == _s5_slot(g, tt), cols[g], z)
                sh = (LANES - S5_GROUP * tt) % LANES
                yscr_ref[pl.ds(base + ng * m + tt, SUBLANES, stride=S5_PITCH), :] = (
                    pltpu.roll(z, sh, axis=1) if sh else z)
        return carry
    lax.fori_loop(0, nstrip, scatter, 0, unroll=2)

    def pitch_out(c, carry):
        src = pl.multiple_of(c * S5_PITCH, SUBLANES)
        dst = pl.multiple_of(c * S5_Q, S5_Q)
        y = yscr_ref[pl.ds(src, S5_Q), :] + d_ref[...] * u_ref[pl.ds(dst, S5_Q), :]
        o_ref[pl.ds(dst, S5_Q), :] = y.astype(o_ref.dtype)
        return carry
    lax.fori_loop(0, nblk, pitch_out, 0, unroll=8)


def _s5_core(h_ab, d_skip, batch, seq, tables):
    nblk = seq // S5_Q
    ng = S5_TILE_GROUPS
    grp = lambda a: pl.BlockSpec((ng,) + a.shape[1:], lambda j, b: (j, 0, 0))
    return pl.pallas_call(
        functools.partial(_s5_body, nblk),
        grid=(S5_GROUPS // ng, batch),
        in_specs=[pl.BlockSpec((seq, LANES), lambda j, b: (b, AB_U // LANES + j)),
                  pl.BlockSpec((1, LANES), lambda j, b: (0, j))]
        + [grp(a) for a in tables],
        out_specs=pl.BlockSpec((seq, LANES), lambda j, b: (b, j)),
        out_shape=jax.ShapeDtypeStruct((batch * seq, S5_WIDTH), BF16),
        scratch_shapes=[pltpu.VMEM((nblk * S5_PITCH, LANES), F32), pltpu.VMEM((nblk * S5_PITCH, LANES), F32),
                        pltpu.VMEM((nblk, ng * S5_ROW), F32), pltpu.VMEM((nblk, ng * S5_ROW), F32)],
        compiler_params=_params("parallel", "parallel"),
        name="s5_core",
    )(h_ab, d_skip, *tables)


def _s5_table_body(kt_ref, pw_rr_ref, pw_ii_ref, b1_ref, b2_ref, c1_ref, c2_ref, m_ref, wst_ref, wofft_ref):
    ng = S5_TILE_GROUPS
    ncol = S5_Q // ng
    lane = lax.broadcasted_iota(jnp.int32, (S5_GROUP, LANES), 1)
    for g in range(ng):
        strip = [jnp.zeros((S5_GROUP, LANES), F32)] * ncol + [kt_ref[g, :, c * LANES:(c + 1) * LANES]
                                                            for c in range(ncol)]
        b1, b2, c1, c2 = b1_ref[g], b2_ref[g], c1_ref[g], c2_ref[g]
        for s in range(S5_Q):
            start = ncol * LANES - S5_GROUP * s
            a, sh = start // LANES, start % LANES
            k = ng * (s // ng) + _s5_slot(g, s % ng)
            rows = slice(S5_GROUP * k, S5_GROUP * (k + 1))
            for c in range(ncol):
                if sh:
                    w = jnp.where(lane < LANES - sh, pltpu.roll(strip[a + c], LANES - sh, axis=1),
                                  pltpu.roll(strip[a + c + 1], LANES - sh, axis=1))
                else:
                    w = strip[a + c]
                if g:
                    w = pltpu.roll(w, S5_GROUP * g, axis=1)
                m_ref[g, rows, c * LANES:(c + 1) * LANES] = w.astype(BF16)
            e = S5_Q - 1 - s
            wst_ref[g, rows, :] = (pw_rr_ref[g, e:e + 1, :] * b1 + pw_ii_ref[g, e:e + 1, :] * b2).astype(BF16)
            wofft_ref[g, rows, :] = (pw_rr_ref[g, s + 1:s + 2, :] * c1
                                     + pw_ii_ref[g, s + 1:s + 2, :] * c2).astype(BF16)


def _s5_tables(lam_re, lam_im, log_dt, b_re, b_im, c_re, c_im, nblk):
    q, ng = S5_Q, S5_TILE_GROUPS
    hp = lax.Precision.HIGHEST
    lam = lax.complex(lam_re.astype(F32), lam_im.astype(F32))
    ldt = lam * jnp.exp(log_dt.astype(F32))[:, None]
    lam_bar = jnp.exp(ldt)
    b_bar = ((lam_bar - 1.0) / lam)[..., None] * lax.complex(b_re.astype(F32), b_im.astype(F32))
    c = lax.complex(c_re.astype(F32), c_im.astype(F32))
    tau = jnp.arange(q + 1, dtype=F32)
    pw = jnp.exp(ldt[:, None, :] * tau[None, :, None])
    kt = jnp.real(jnp.einsum('gpn,gtn,gnq->gqtp', c, pw[:, :q], b_bar, precision=hp)).reshape(
        S5_GROUPS, S5_GROUP, S5_ROW)
    halves = lambda lo, hi: jnp.concatenate([lo, hi], axis=-1)
    bt = jnp.transpose(b_bar, (0, 2, 1))
    small = [halves(jnp.real(pw), jnp.real(pw)), halves(jnp.imag(pw), jnp.imag(pw)),
             halves(jnp.real(bt), jnp.imag(bt)), halves(-jnp.imag(bt), jnp.real(bt)),
             halves(jnp.real(c), -jnp.imag(c)), halves(-jnp.imag(c), -jnp.real(c))]
    grp = lambda shape: pl.BlockSpec((ng,) + shape, lambda j: (j, 0, 0))
    shapes = [(S5_ROW, S5_ROW), (S5_ROW, 2 * S5_STATE), (S5_ROW, 2 * S5_STATE)]
    m, wst, wofft = pl.pallas_call(
        _s5_table_body,
        grid=(S5_GROUPS // ng,),
        in_specs=[grp(kt.shape[1:])] + [grp(a.shape[1:]) for a in small],
        out_specs=[grp(sh) for sh in shapes],
        out_shape=[jax.ShapeDtypeStruct((S5_GROUPS,) + sh, BF16) for sh in shapes],
        compiler_params=_params("parallel"),
        name="s5_tables",
    )(kt, *small)
    nlev = int(math.log2(nblk))
    step = jnp.exp(ldt[:, None, :] * (q * 2.0 ** jnp.arange(nlev, dtype=F32))[None, :, None])
    a1 = halves(jnp.real(step), jnp.real(step))
    a2 = halves(-jnp.imag(step), jnp.imag(step))
    return m, wst, wofft, a1, a2


def _lin_levels(c):
    return [c >> (i + 1) for i in range(int(math.log2(c)))]


def _lin_tables(c):
    t = np.arange(c)[:, None]
    j = np.arange(c)[None, :]
    lvl = np.full((c, c), -1, np.int32)
    lvl[np.arange(c), np.arange(c)] = 0
    sgn = []
    for i, b in enumerate(_lin_levels(c)):
        same = (t // (2 * b)) == (j // (2 * b))
        second = (t % (2 * b)) >= b
        lvl[same & second & ((j % (2 * b)) < b)] = i + 1
        sgn.append(np.broadcast_to(np.where(second, 1.0, -1.0) * math.log2(math.e), (c, LANES)))
    return (jnp.asarray((j <= t).astype(np.float32), BF16), jnp.asarray(lvl),
            jnp.asarray(np.stack(sgn), F32))


def _gate_factors(g, tril, sgn_ref, gcum_ref):
    c = LIN_CHUNK
    gcum = _sel_dot(tril, g)
    gcum_ref[...] = gcum
    pos = lax.broadcasted_iota(jnp.int32, (c, LANES), 0)
    factors = []
    for i, b in enumerate(_lin_levels(c)):
        if 2 * b >= SUBLANES:
            mids = [jnp.broadcast_to(gcum_ref[blk * 2 * b + b - 1:blk * 2 * b + b, :], (2 * b, LANES))
                    for blk in range(c // (2 * b))]
            gmid = mids[0] if len(mids) == 1 else jnp.concatenate(mids, axis=0)
        elif b == 2:
            p4 = pos & 3
            gmid = jnp.where(p4 == 0, pltpu.roll(gcum, c - 1, axis=0),
                             jnp.where(p4 == 1, gcum,
                                       jnp.where(p4 == 2, pltpu.roll(gcum, 1, axis=0),
                                                 pltpu.roll(gcum, 2, axis=0))))
        else:
            gmid = jnp.where((pos & 1) == 1, pltpu.roll(gcum, 1, axis=0), gcum)
        factors.append(jnp.exp2((gcum - gmid) * sgn_ref[i]).astype(BF16))
    e_cum = jnp.exp(gcum)
    return e_cum, e_cum.astype(BF16), jnp.exp(gcum[c - 1:c, :] - gcum).astype(BF16), factors


def _lin_body(h_ref, gqk_ref, gv_ref, gr_ref, glr_ref, loglb_ref, log1mlb_ref, onemlb_ref, hnorm_ref,
              wup_ref, bgate_ref, gnorm_ref, tril_ref, lvl_ref, sgn_ref, o_ref, state_ref, gcum_ref):
    c = LIN_CHUNK

    @pl.when(pl.program_id(1) == 0)
    def _():
        state_ref[...] = jnp.zeros(state_ref.shape, F32)

    def chunk(ci, carry):
        rows = pl.ds(pl.multiple_of(ci * c, c), c)

        hq = h_ref[rows, 0:HGRN_WIDTH]
        hf = h_ref[rows, HGRN_WIDTH:2 * HGRN_WIDTH]
        lo = loglb_ref[...]
        hi_ = log1mlb_ref[...] + _log_sigmoid(hf)
        log_f = jnp.maximum(lo, hi_) + jnp.log1p(jnp.exp(-jnp.abs(lo - hi_)))
        q_c = _silu(hq)
        k_c = onemlb_ref[...] * _sigmoid(-hf)
        pre = _dot(glr_ref[rows, :].astype(BF16), wup_ref[...]) + bgate_ref[...]
        log_a = _log_sigmoid(pre) * (1.0 / GLA_TAU)
        q_d = gqk_ref[rows, 0:GLA_QK] * (GLA_DK ** -0.5)
        k_d = gqk_ref[rows, GLA_QK:2 * GLA_QK]

        lvl = lvl_ref[...]
        lane = lax.broadcasted_iota(jnp.int32, (c, LANES), 1)
        tril = tril_ref[...]

        heads = []
        for hd in range(LIN_HEADS):
            ls = slice(hd * HEAD_W, (hd + 1) * HEAD_W)
            heads.append((hd, q_c[:, ls], k_c[:, ls], hd,
                          h_ref[rows, 2 * HGRN_WIDTH + hd * HEAD_W:2 * HGRN_WIDTH + (hd + 1) * HEAD_W],
                          hnorm_ref[:, ls],
                          h_ref[rows, 3 * HGRN_WIDTH + hd * HEAD_W:3 * HGRN_WIDTH + (hd + 1) * HEAD_W], ls))
        for hd in range(LIN_HEADS):
            tile, half = divmod(hd, LANES // GLA_DK)
            ts = slice(tile * LANES, (tile + 1) * LANES)
            ls = slice(hd * HEAD_W, (hd + 1) * HEAD_W)
            mine = (lane >> int(math.log2(GLA_DK))) == half
            heads.append((LIN_HEADS + hd, jnp.where(mine, q_d[:, ts], 0.0), jnp.where(mine, k_d[:, ts], 0.0),
                          LIN_HEADS + tile, gv_ref[rows, ls], gnorm_ref[:, ls], gr_ref[rows, ls],
                          slice(HGRN_WIDTH + hd * HEAD_W, HGRN_WIDTH + (hd + 1) * HEAD_W)))

        gates = [log_f[:, t * LANES:(t + 1) * LANES] for t in range(LIN_HEADS)]
        gates += [log_a[:, t * LANES:(t + 1) * LANES] for t in range(GLA_QK // LANES)]
        factors = [_gate_factors(g, tril, sgn_ref, gcum_ref.at[t]) for t, g in enumerate(gates)]
        qkb = [(q.astype(BF16), k.astype(BF16)) for _, q, k, *_ in heads]
        attns = []
        for (qb, kb), head in zip(qkb, heads):
            e_lvl = factors[head[3]][3]
            attn = jnp.where(lvl == 0, _dot_nt(qb, kb), 0.0)
            for i, e in enumerate(e_lvl):
                attn = jnp.where(lvl == i + 1, _dot_nt(qb * e, kb * e), attn)
            attns.append(attn.astype(BF16))
        for (qb, kb), attn, (idx, _, _, tile, v, norm_w, gate, cols) in zip(qkb, attns, heads):
            e_cum, e_cum_b, e_end_b, _ = factors[tile]
            vb = v.astype(BF16)
            state_t = state_ref[idx]
            o = _dot(attn, vb) + _dot_nt(qb * e_cum_b, state_t.astype(BF16))
            state_ref[idx] = state_t * e_cum[c - 1:c, :] + _dot_tn(vb, kb * e_end_b)
            o_ref[rows, cols] = (_rms_norm(o, norm_w) * _silu(gate)).astype(o_ref.dtype)
        return carry

    lax.fori_loop(0, h_ref.shape[0] // c, chunk, 0)


def _lin_mixer(h_cd, batch, seq, lb, hgrn_norm_w, gla_w_gate_up, gla_b_gate, gla_norm_w):
    c = LIN_CHUNK
    step = c * MIX_STEP_CHUNKS
    nc = seq // step
    tok = lambda cb: (lambda b, i: (b * nc + i, cb))
    lb = lb.astype(F32)[None, :]
    wup = jnp.zeros((LANES, GLA_QK), F32).at[:GLA_RANK].set(gla_w_gate_up).astype(BF16)
    tril, lvl, sgn = _lin_tables(c)
    consts = [jnp.log(lb), jnp.log1p(-lb), 1.0 - lb, hgrn_norm_w[None, :], wup, gla_b_gate[None, :],
              gla_norm_w[None, :], tril, lvl, sgn]
    in_specs = [
        pl.BlockSpec((step, 4 * HGRN_WIDTH), tok(CD_H // (4 * HGRN_WIDTH))),
        pl.BlockSpec((step, 2 * GLA_QK), tok(CD_GQK // (2 * GLA_QK))),
        pl.BlockSpec((step, GLA_WIDTH), tok(CD_GV // GLA_WIDTH)),
        pl.BlockSpec((step, GLA_WIDTH), tok(CD_GR // GLA_WIDTH)),
        pl.BlockSpec((step, LANES), tok(CD_GLR // LANES)),
    ] + [_resident(a.shape) for a in consts]
    return pl.pallas_call(
        _lin_body,
        grid=(batch, nc),
        in_specs=in_specs,
        out_specs=pl.BlockSpec((step, HGRN_WIDTH + GLA_WIDTH), tok(0)),
        out_shape=jax.ShapeDtypeStruct((batch * seq, HGRN_WIDTH + GLA_WIDTH), BF16),
        scratch_shapes=[pltpu.VMEM((2 * LIN_HEADS, HEAD_W, HEAD_W), F32),
                        pltpu.VMEM((LIN_HEADS + GLA_QK // LANES, c, LANES), F32)],
        compiler_params=_params("parallel", "arbitrary"),
        name="lin_mixer",
    )(h_cd, h_cd, h_cd, h_cd, h_cd, *consts)


def _pack_cols(w, splits, order, total):
    bounds = np.concatenate([[0], np.cumsum(splits)])
    parts = [w[:, bounds[f]:bounds[f + 1]] for f in order]
    parts.append(jnp.zeros((w.shape[0], total - sum(splits[f] for f in order)), w.dtype))
    return jnp.concatenate(parts, axis=1).astype(BF16)


def _hgrn_lower_bound(lb_logits, layer):
    cum = jnp.cumsum(jax.nn.softmax(lb_logits.astype(F32), axis=0), axis=0)
    return cum[layer] - cum[0]


def kernel(x, p, ln_g, ln_b, ffn_w_gate, ffn_w_up, ffn_w_down, ple_w_gate, ple_w_proj, ab_w_in, ab_w_out,
           ssd_conv_w, ssd_conv_b, ssd_dt_bias, ssd_a_log, ssd_d, ssd_norm_w, s5_lambda_re, s5_lambda_im,
           s5_log_dt, s5_b_re, s5_b_im, s5_c_re, s5_c_im, s5_d, s5_w_glu, s5_b_glu, cd_w_in, cd_w_out,
           hgrn_lb_logits, hgrn_norm_w, gla_w_gate_up, gla_b_gate, gla_norm_w):
    batch, seq, _ = x.shape
    t = batch * seq
    x = x.reshape(t, D_MODEL)
    bf = lambda w: w.astype(BF16)
    ab_splits = (SSD_INNER, SSD_CONV_DIM, SSD_HEADS, S5_WIDTH)
    cd_splits = (HGRN_WIDTH, HGRN_WIDTH, HGRN_WIDTH, HGRN_WIDTH, GLA_QK, GLA_QK, GLA_WIDTH, GLA_RANK,
                 GLA_WIDTH)
    wg, wu, wd = bf(ffn_w_gate), bf(ffn_w_up), bf(ffn_w_down)
    pg, pp = bf(ple_w_gate), bf(ple_w_proj)
    p = p.reshape(DEPTH, t, PLE_DIM)
    ln_g = ln_g.reshape(DEPTH * 3, 1, D_MODEL)
    ln_b = ln_b.reshape(DEPTH * 3, 1, D_MODEL)
    w_ab_out, w_cd_out = bf(ab_w_out), bf(cd_w_out)
    w_glu, b_glu = bf(s5_w_glu), s5_b_glu[:, None, :]
    for i in range(DEPTH):
        j = i // 2
        x = _ffn_ln(x, wg, wu, wd, ln_g, ln_b, i, 0, 3 * i)
        if i % 2 == 0:
            w_in = _pack_cols(ab_w_in[j], ab_splits, [1, 0, 3, 2], AB_PACK)
            h_ab = _proj_in(x, w_in)
            y_a = _ssd_mixer(h_ab, batch, seq, ssd_conv_w[j], ssd_conv_b[j], ssd_dt_bias[j], ssd_a_log[j],
                             ssd_d[j], ssd_norm_w[j])
            tables = _s5_tables(s5_lambda_re[j], s5_lambda_im[j], s5_log_dt[j], s5_b_re[j], s5_b_im[j],
                                s5_c_re[j], s5_c_im[j], seq // S5_Q)
            y_s = _s5_core(h_ab, s5_d[j].reshape(1, S5_WIDTH).astype(F32), batch, seq, tables)
            mix = ("ab", (y_a, y_s), [(w_ab_out, (j,)), (w_glu, (j,)), (b_glu, (j,))])
        else:
            w_in = _pack_cols(cd_w_in[j], cd_splits, [0, 1, 2, 3, 4, 5, 6, 8, 7], CD_PACK)
            h_cd = _proj_in(x, w_in)
            o_cd = _lin_mixer(h_cd, batch, seq, _hgrn_lower_bound(hgrn_lb_logits, i), hgrn_norm_w[j],
                              gla_w_gate_up[j], gla_b_gate[j], gla_norm_w[j])
            mix = ("cd", (o_cd,), [(w_cd_out, (j,))])
        x = _ffn_ln(x, wg, wu, wd, ln_g, ln_b, i, 1, 3 * i + 2, mix=mix, ple=(p, pg, pp))
    return x.reshape(batch, seq, D_MODEL)
```

```python
import functools
import math

import jax
import jax.numpy as jnp
import numpy as np
from jax import lax
from jax.experimental import pallas as pl
from jax.experimental.pallas import tpu as pltpu

F32 = jnp.float32
BF16 = jnp.bfloat16

D_MODEL = 1024
D_FF = 2816
PLE_DIM = 256
DEPTH = 2
DN_ALPHA = (2.0 * DEPTH) ** 0.25
LN_EPS = 1e-5
NEG_BIG = -1e30
SSD_HEADS = 16
SSD_HEAD_DIM = 64
SSD_GROUPS = 4
SSD_STATE = 128
SSD_CONV = 4
SSD_INNER = 1024
SSD_BC = SSD_GROUPS * SSD_STATE
SSD_CONV_DIM = SSD_INNER + 2 * SSD_BC
SSD_GROUP_WIDTH = SSD_INNER // SSD_GROUPS
SSD_HEADS_PER_GROUP = SSD_HEADS // SSD_GROUPS
S5_WIDTH = 1024
S5_GROUPS = 64
S5_GROUP = 16
S5_STATE = 64
LIN_HEADS = 4
HGRN_WIDTH = 512
GLA_DK = 64
GLA_QK = LIN_HEADS * GLA_DK
GLA_WIDTH = 512
GLA_RANK = 16
GLA_TAU = 16.0
HEAD_W = 128

LANES = 128
SUBLANES = 8
VMEM_LIMIT = 56 * 1024 * 1024

TM = 512
TM_FFN = 1024
SUB_FFN = 512
FF_CHUNK = 256
SSD_CHUNK = 128
MIX_STEP_CHUNKS = 4
LIN_CHUNK = 128
S5_Q = 32
S5_ROW = S5_Q * S5_GROUP
S5_TILE_GROUPS = LANES // S5_GROUP
S5_PITCH = 40

AB_XBC, AB_Z, AB_U, AB_DT, AB_PACK = 0, 2048, 3072, 4096, 4224
CD_H, CD_GQK, CD_GV, CD_GR, CD_GLR, CD_PACK = 0, 2048, 2560, 3072, 3584, 3712


def _resident(shape):
    n = len(shape)
    return pl.BlockSpec(shape, lambda *_: (0,) * n, pipeline_mode=pl.Buffered(1))


def _dot(a, b):
    return jnp.dot(a, b, preferred_element_type=F32)


def _dot_nt(a, b):
    return lax.dot_general(a, b, (((1,), (1,)), ((), ())), preferred_element_type=F32)


def _dot_tn(a, b):
    return lax.dot_general(a, b, (((0,), (0,)), ((), ())), preferred_element_type=F32)


def _split3(v):
    hi = v.astype(BF16)
    r = v - hi.astype(F32)
    mid = r.astype(BF16)
    lo = (r - mid.astype(F32)).astype(BF16)
    return hi, mid, lo


def _sel_dot(sel, v):
    hi, mid, lo = _split3(v)
    return _dot(sel, hi) + _dot(sel, mid) + _dot(sel, lo)


def _dot_sel(v, sel):
    hi, mid, lo = _split3(v)
    return _dot(hi, sel) + _dot(mid, sel) + _dot(lo, sel)


def _sigmoid(x):
    return 1.0 / (1.0 + jnp.exp(-x))


def _silu(x):
    return x * _sigmoid(x)


def _log_sigmoid(x):
    return jnp.minimum(x, 0.0) - jnp.log1p(jnp.exp(-jnp.abs(x)))


def _softplus(x):
    return jnp.maximum(x, 0.0) + jnp.log1p(jnp.exp(-jnp.abs(x)))


def _gelu_tanh(x):
    return 0.5 * x * (1.0 + jnp.tanh(math.sqrt(2.0 / math.pi) * (x + 0.044715 * (x * x * x))))


def _layer_norm(y, g, b):
    mu = jnp.mean(y, axis=-1, keepdims=True)
    yc = y - mu
    var = jnp.mean(yc * yc, axis=-1, keepdims=True)
    return yc * lax.rsqrt(var + LN_EPS) * g + b


def _rms_norm(y, w):
    return y * lax.rsqrt(jnp.mean(y * y, axis=-1, keepdims=True) + LN_EPS) * w


def _params(*sem):
    return pltpu.CompilerParams(dimension_semantics=sem, vmem_limit_bytes=VMEM_LIMIT)


def _ffn_body(mix, with_ple, x_ref, *refs):
    refs = list(refs)
    take = lambda n: [refs.pop(0) for _ in range(n)]
    if mix == "ab":
        ya_ref, ys_ref, wout_ref, wglu_ref, bglu_ref, g1_ref, b1_ref = take(7)
    elif mix == "cd":
        oc_ref, wout_ref, g1_ref, b1_ref = take(4)
    wg_ref, wu_ref, wd_ref, g_ref, b_ref = take(5)
    if with_ple:
        p_ref, pg_ref, pp_ref = take(3)
    o_ref, acc_ref = refs
    for r in range(x_ref.shape[0] // SUB_FFN):
        rows = slice(r * SUB_FFN, (r + 1) * SUB_FFN)
        x = x_ref[rows, :]
        if mix == "ab":
            y = _gelu_tanh(ys_ref[rows, :].astype(F32))
            gate = _sigmoid(_dot(y.astype(BF16), wglu_ref[...]) + bglu_ref[...])
            m = _dot(ya_ref[rows, :], wout_ref[0:SSD_INNER, :]) + _dot((y * gate).astype(BF16), wout_ref[SSD_INNER:, :])
            x = _layer_norm(DN_ALPHA * x + m, g1_ref[...], b1_ref[...])
        elif mix == "cd":
            x = _layer_norm(DN_ALPHA * x + _dot(oc_ref[rows, :], wout_ref[...]), g1_ref[...], b1_ref[...])
        xb = x.astype(BF16)
        for c in range(D_FF // FF_CHUNK):
            sl = slice(c * FF_CHUNK, (c + 1) * FF_CHUNK)
            gate = _dot(xb, wg_ref[:, sl])
            up = _dot(xb, wu_ref[:, sl])
            h = (_silu(gate) * up).astype(BF16)
            d = _dot(h, wd_ref[sl, :])
            if c == 0:
                acc_ref[rows, :] = d
            else:
                acc_ref[rows, :] += d
        y = _layer_norm(DN_ALPHA * x + 0.5 * acc_ref[rows, :], g_ref[...], b_ref[...])
        if with_ple:
            gate = _sigmoid(_dot(y.astype(BF16), pg_ref[...]))
            y = y + gate * _dot(p_ref[rows, :].astype(BF16), pp_ref[...])
        o_ref[rows, :] = y


def _pick(a, *idx):
    rest = a.shape[len(idx):]
    return pl.BlockSpec((None,) * len(idx) + rest, lambda *_: idx + (0,) * len(rest),
                        pipeline_mode=pl.Buffered(1))


def _ffn_ln(x, wg, wu, wd, ln_g, ln_b, layer, pos, ln_idx, mix=None, ple=None):
    t = x.shape[0]
    row = lambda i: (i, 0)
    in_specs = [pl.BlockSpec((TM_FFN, D_MODEL), row)]
    args = [x]
    if mix is not None:
        kind, acts, consts = mix
        in_specs += [pl.BlockSpec((TM_FFN, a.shape[1]), row) for a in acts]
        in_specs += [_pick(a, *idx) for a, idx in consts]
        in_specs += [_pick(ln_g, ln_idx - 1), _pick(ln_b, ln_idx - 1)]
        args += list(acts) + [a for a, _ in consts] + [ln_g, ln_b]
    in_specs += [_pick(wg, layer, pos), _pick(wu, layer, pos), _pick(wd, layer, pos),
                 _pick(ln_g, ln_idx), _pick(ln_b, ln_idx)]
    args += [wg, wu, wd, ln_g, ln_b]
    if ple is not None:
        p, pg, pp = ple
        in_specs += [pl.BlockSpec((None, TM_FFN, PLE_DIM), lambda i: (layer, i, 0)), _pick(pg, layer), _pick(pp, layer)]
        args += [p, pg, pp]
    return pl.pallas_call(
        functools.partial(_ffn_body, None if mix is None else mix[0], ple is not None),
        grid=(t // TM_FFN,),
        in_specs=in_specs,
        out_specs=pl.BlockSpec((TM_FFN, D_MODEL), row),
        out_shape=jax.ShapeDtypeStruct((t, D_MODEL), F32),
        scratch_shapes=[pltpu.VMEM((TM_FFN, D_MODEL), F32)],
        compiler_params=_params("parallel"),
        name="ffn_ln" if mix is None else "mix_ffn_ln_ple",
    )(*args)


def _proj_in_body(starts, x_ref, *refs):
    w_refs, o_ref = refs[:-1], refs[-1]
    xb = x_ref[...].astype(BF16)
    step = 512
    for w_ref, c0 in zip(w_refs, starts):
        n = w_ref.shape[1]
        for s0 in range(0, n, step):
            s1 = min(s0 + step, n)
            o_ref[:, c0 + s0:c0 + s1] = _dot(xb, w_ref[:, s0:s1])


def _proj_in(x, parts, n_out):
    t = x.shape[0]
    weights = [w for w, _ in parts]
    spans = sorted((c0, c0 + w.shape[1]) for w, c0 in parts)
    assert spans[0][0] == 0 and spans[-1][1] == n_out and all(a[1] == b[0] for a, b in zip(spans, spans[1:]))
    return pl.pallas_call(
        functools.partial(_proj_in_body, tuple(c0 for _, c0 in parts)),
        grid=(t // TM,),
        in_specs=[pl.BlockSpec((TM, D_MODEL), lambda i: (i, 0))] + [_resident(w.shape) for w in weights],
        out_specs=pl.BlockSpec((TM, n_out), lambda i: (i, 0)),
        out_shape=jax.ShapeDtypeStruct((t, n_out), F32),
        compiler_params=_params("parallel"),
        name="proj_in",
    )(x, *weights)


def _ssd_body(xbc_ref, z_ref, dt_ref, convw_ref, convb_ref, dtb_ref, alog_ref, dskip_ref,
              normw_ref, tril_ref, expand_ref, shift_ref, o_ref, xb_ref, state_ref):
    ch = SSD_CHUNK
    gw = SSD_GROUP_WIDTH

    @pl.when(pl.program_id(1) == 0)
    def _():
        xb_ref[0:ch, :] = jnp.zeros((ch, SSD_CONV_DIM), BF16)
        state_ref[...] = jnp.zeros(state_ref.shape, F32)

    def chunk(i, carry):
        rows = pl.ds(pl.multiple_of(i * ch, ch), ch)

        def conv_silu(cols):
            cur = xbc_ref[rows, cols]
            cur_b = cur.astype(BF16)
            xb_ref[ch:2 * ch, cols] = cur_b
            both = xb_ref[:, cols]
            acc = cur * convw_ref[SSD_CONV - 1:SSD_CONV, cols] + convb_ref[:, cols]
            for j in range(1, SSD_CONV):
                acc = acc + _dot(shift_ref[j - 1], both) * convw_ref[SSD_CONV - 1 - j:SSD_CONV - j, cols]
            xb_ref[0:ch, cols] = cur_b
            return _silu(acc)

        dt = _softplus(dt_ref[rows, :] + dtb_ref[...])
        da = dt * (-jnp.exp(alog_ref[...]))
        a_cum = _sel_dot(tril_ref[...], da)
        a_cum_t = a_cum.T
        dt_hi, dt_mid, _ = _split3(dt)
        dt_e = _dot(dt_hi, expand_ref[...]) + _dot(dt_mid, expand_ref[...])
        ac_e = _dot_sel(a_cum, expand_ref[...])
        ac_last = ac_e[ch - 1:ch, :]
        to_end = jnp.exp(ac_last - ac_e)
        carry_scale = jnp.exp(ac_e)
        chunk_decay = jnp.exp(ac_last)

        row = lax.broadcasted_iota(jnp.int32, (ch, ch), 0)
        col = lax.broadcasted_iota(jnp.int32, (ch, ch), 1)
        causal = col <= row
        lane_head = lax.broadcasted_iota(jnp.int32, (ch, gw), 1) >> int(math.log2(SSD_HEAD_DIM))

        groups = range(SSD_GROUPS)
        cols = [slice(g * gw, (g + 1) * gw) for g in groups]
        xs = [conv_silu(cols[g]) for g in groups]
        bc = [conv_silu(slice(SSD_INNER + k * gw, SSD_INNER + (k + 1) * gw)).astype(BF16)
              for k in range(2 * SSD_BC // gw)]
        bc = [v[:, half * SSD_STATE:(half + 1) * SSD_STATE] for v in bc for half in range(gw // SSD_STATE)]
        bs, cs = bc[:SSD_GROUPS], bc[SSD_GROUPS:]
        xdt = [xs[g] * dt_e[:, cols[g]] for g in groups]
        xdt_b = [v.astype(BF16) for v in xdt]
        scores = [_dot_nt(cs[g], bs[g]).astype(BF16) for g in groups]
        ys = []
        for g in groups:
            y_g = jnp.zeros((ch, gw), F32)
            for hh in range(SSD_HEADS_PER_GROUP):
                h = g * SSD_HEADS_PER_GROUP + hh
                seg = a_cum[:, h:h + 1] - a_cum_t[h:h + 1, :]
                decay = jnp.exp(jnp.where(causal, seg, NEG_BIG)).astype(BF16)
                full = _dot(scores[g] * decay, xdt_b[g])
                y_g = jnp.where(lane_head == hh, full, y_g)
            ys.append(y_g)
        for g in groups:
            gs = cols[g]
            state = state_ref[g]
            y_g = ys[g] + _dot(cs[g], state.astype(BF16)) * carry_scale[:, gs]
            state_ref[g] = state * chunk_decay[:, gs] + _dot_tn(bs[g], (xdt[g] * to_end[:, gs]).astype(BF16))
            y_g = y_g + xs[g] * dskip_ref[:, gs]
            y_g = y_g * _silu(z_ref[rows, gs])
            o_ref[rows, gs] = _rms_norm(y_g, normw_ref[:, gs]).astype(o_ref.dtype)
        return carry

    lax.fori_loop(0, xbc_ref.shape[0] // ch, chunk, 0)


def _ssd_mixer(h_ab, batch, seq, conv_w, conv_b, dt_bias, a_log, d_skip, norm_w):
    ch = SSD_CHUNK
    step = ch * MIX_STEP_CHUNKS
    nc = seq // step
    tok = lambda cb: (lambda b, c: (b * nc + c, cb))
    pad_heads = lambda v: jnp.zeros((1, LANES), F32).at[0, :SSD_HEADS].set(v.astype(F32))
    tril = jnp.asarray(np.tril(np.ones((ch, ch), np.float32)), BF16)
    expand = np.zeros((LANES, SSD_INNER), np.float32)
    for h in range(SSD_HEADS):
        expand[h, h * SSD_HEAD_DIM:(h + 1) * SSD_HEAD_DIM] = 1.0
    expand = jnp.asarray(expand, BF16)
    shift = np.zeros((SSD_CONV - 1, ch, 2 * ch), np.float32)
    for j in range(1, SSD_CONV):
        shift[j - 1, np.arange(ch), ch + np.arange(ch) - j] = 1.0
    shift = jnp.asarray(shift, BF16)
    dskip_e = jnp.repeat(d_skip.astype(F32), SSD_HEAD_DIM)[None, :]
    consts = [conv_w, conv_b[None, :], pad_heads(dt_bias), pad_heads(a_log), dskip_e,
              norm_w[None, :], tril, expand, shift]
    in_specs = [
        pl.BlockSpec((step, SSD_CONV_DIM), tok(AB_XBC // SSD_CONV_DIM)),
        pl.BlockSpec((step, SSD_INNER), tok(AB_Z // SSD_INNER)),
        pl.BlockSpec((step, LANES), tok(AB_DT // LANES)),
    ] + [_resident(a.shape) for a in consts]
    return pl.pallas_call(
        _ssd_body,
        grid=(batch, nc),
        in_specs=in_specs,
        out_specs=pl.BlockSpec((step, SSD_INNER), tok(0)),
        out_shape=jax.ShapeDtypeStruct((batch * seq, SSD_INNER), BF16),
        scratch_shapes=[pltpu.VMEM((2 * ch, SSD_CONV_DIM), BF16),
                        pltpu.VMEM((SSD_GROUPS, SSD_STATE, SSD_GROUP_WIDTH), F32)],
        compiler_params=_params("parallel", "arbitrary"),
        name="ssd_mixer",
    )(h_ab, h_ab, h_ab, *consts)


def _s5_slot(g, s):
    return (s + g) % S5_TILE_GROUPS


def _s5_body(nblk, u_ref, d_ref, m_ref, wst_ref, wofft_ref, a1_ref, a2_ref, o_ref, uscr_ref, yscr_ref, ug_ref,
             yg_ref):
    ng = S5_TILE_GROUPS
    ncol = S5_Q // ng
    nstrip = nblk // SUBLANES
    slot = lax.broadcasted_iota(jnp.int32, (SUBLANES, LANES), 1) >> int(math.log2(S5_GROUP))
    blk = lax.broadcasted_iota(jnp.int32, (nblk, 2 * S5_STATE), 0)

    def pitch_in(c, carry):
        src = pl.multiple_of(c * S5_Q, S5_Q)
        dst = pl.multiple_of(c * S5_PITCH, SUBLANES)
        uscr_ref[pl.ds(dst, S5_Q), :] = u_ref[pl.ds(src, S5_Q), :]
        return carry
    lax.fori_loop(0, nblk, pitch_in, 0, unroll=8)

    def gather(i, carry):
        base = pl.multiple_of(i * (SUBLANES * S5_PITCH), SUBLANES)
        row = pl.multiple_of(i * SUBLANES, SUBLANES)
        for m in range(ncol):
            rolled = []
            for s in range(ng):
                us = uscr_ref[pl.ds(base + ng * m + s, SUBLANES, stride=S5_PITCH), :]
                rolled.append(pltpu.roll(us, S5_GROUP * s, axis=1) if s else us)
            for g in range(ng):
                v = rolled[0]
                for s in range(1, ng):
                    v = jnp.where(slot == _s5_slot(g, s), rolled[s], v)
                ug_ref[pl.ds(row, SUBLANES), (g * ncol + m) * LANES:(g * ncol + m + 1) * LANES] = v
        return carry
    lax.fori_loop(0, nstrip, gather, 0, unroll=2)

    us = [ug_ref[:, g * S5_ROW:(g + 1) * S5_ROW].astype(BF16) for g in range(ng)]
    xs = [_dot(us[g], wst_ref[g]) for g in range(ng)]
    for k in range(int(math.log2(nblk))):
        sh = 1 << k
        for g in range(ng):
            prev = jnp.where(blk >= sh, pltpu.roll(xs[g], sh, axis=0), 0.0)
            xs[g] = (xs[g] + prev * a1_ref[g, k:k + 1, :]
                     + pltpu.roll(prev, S5_STATE, axis=1) * a2_ref[g, k:k + 1, :])
    for g in range(ng):
        x_in = jnp.where(blk >= 1, pltpu.roll(xs[g], 1, axis=0), 0.0)
        yg_ref[:, g * S5_ROW:(g + 1) * S5_ROW] = (_dot(us[g], m_ref[g])
                                                   + _dot_nt(x_in.astype(BF16), wofft_ref[g]))

    def scatter(i, carry):
        base = pl.multiple_of(i * (SUBLANES * S5_PITCH), SUBLANES)
        row = pl.multiple_of(i * SUBLANES, SUBLANES)
        for m in range(ncol):
            cols = [yg_ref[pl.ds(row, SUBLANES), (g * ncol + m) * LANES:(g * ncol + m + 1) * LANES]
                    for g in range(ng)]
            for tt in range(ng):
                z = cols[0]
                for g in range(1, ng):
                    z = jnp.where(slot == _s5_slot(g, tt), cols[g], z)
                sh = (LANES - S5_GROUP * tt) % LANES
                yscr_ref[pl.ds(base + ng * m + tt, SUBLANES, stride=S5_PITCH), :] = (
                    pltpu.roll(z, sh, axis=1) if sh else z)
        return carry
    lax.fori_loop(0, nstrip, scatter, 0, unroll=2)

    def pitch_out(c, carry):
        src = pl.multiple_of(c * S5_PITCH, SUBLANES)
        dst = pl.multiple_of(c * S5_Q, S5_Q)
        y = yscr_ref[pl.ds(src, S5_Q), :] + d_ref[...] * u_ref[pl.ds(dst, S5_Q), :]
        o_ref[pl.ds(dst, S5_Q), :] = y.astype(o_ref.dtype)
        return carry
    lax.fori_loop(0, nblk, pitch_out, 0, unroll=8)


def _s5_core(h_ab, d_skip, batch, seq, tables):
    nblk = seq // S5_Q
    ng = S5_TILE_GROUPS
    grp = lambda a: pl.BlockSpec((ng,) + a.shape[1:], lambda j, b: (j, 0, 0))
    return pl.pallas_call(
        functools.partial(_s5_body, nblk),
        grid=(S5_GROUPS // ng, batch),
        in_specs=[pl.BlockSpec((seq, LANES), lambda j, b: (b, AB_U // LANES + j)),
                  pl.BlockSpec((1, LANES), lambda j, b: (0, j))]
        + [grp(a) for a in tables],
        out_specs=pl.BlockSpec((seq, LANES), lambda j, b: (b, j)),
        out_shape=jax.ShapeDtypeStruct((batch * seq, S5_WIDTH), BF16),
        scratch_shapes=[pltpu.VMEM((nblk * S5_PITCH, LANES), F32), pltpu.VMEM((nblk * S5_PITCH, LANES), F32),
                        pltpu.VMEM((nblk, ng * S5_ROW), F32), pltpu.VMEM((nblk, ng * S5_ROW), F32)],
        compiler_params=_params("parallel", "parallel"),
        name="s5_core",
    )(h_ab, d_skip, *tables)


def _s5_table_body(kt_ref, pw_rr_ref, pw_ii_ref, b1_ref, b2_ref, c1_ref, c2_ref, m_ref, wst_ref, wofft_ref):
    ng = S5_TILE_GROUPS
    ncol = S5_Q // ng
    lane = lax.broadcasted_iota(jnp.int32, (S5_GROUP, LANES), 1)
    for g in range(ng):
        strip = [jnp.zeros((S5_GROUP, LANES), F32)] * ncol + [kt_ref[g, :, c * LANES:(c + 1) * LANES]
                                                            for c in range(ncol)]
        b1, b2, c1, c2 = b1_ref[g], b2_ref[g], c1_ref[g], c2_ref[g]
        for s in range(S5_Q):
            start = ncol * LANES - S5_GROUP * s
            a, sh = start // LANES, start % LANES
            k = ng * (s // ng) + _s5_slot(g, s % ng)
            rows = slice(S5_GROUP * k, S5_GROUP * (k + 1))
            for c in range(ncol):
                if sh:
                    w = jnp.where(lane < LANES - sh, pltpu.roll(strip[a + c], LANES - sh, axis=1),
                                  pltpu.roll(strip[a + c + 1], LANES - sh, axis=1))
                else:
                    w = strip[a + c]
                if g:
                    w = pltpu.roll(w, S5_GROUP * g, axis=1)
                m_ref[g, rows, c * LANES:(c + 1) * LANES] = w.astype(BF16)
            e = S5_Q - 1 - s
            wst_ref[g, rows, :] = (pw_rr_ref[g, e:e + 1, :] * b1 + pw_ii_ref[g, e:e + 1, :] * b2).astype(BF16)
            wofft_ref[g, rows, :] = (pw_rr_ref[g, s + 1:s + 2, :] * c1
                                     + pw_ii_ref[g, s + 1:s + 2, :] * c2).astype(BF16)


def _s5_tables(lam_re, lam_im, log_dt, b_re, b_im, c_re, c_im, nblk):
    q, ng = S5_Q, S5_TILE_GROUPS
    hp = lax.Precision.HIGHEST
    lam = lax.complex(lam_re.astype(F32), lam_im.astype(F32))
    ldt = lam * jnp.exp(log_dt.astype(F32))[:, None]
    lam_bar = jnp.exp(ldt)
    b_bar = ((lam_bar - 1.0) / lam)[..., None] * lax.complex(b_re.astype(F32), b_im.astype(F32))
    c = lax.complex(c_re.astype(F32), c_im.astype(F32))
    tau = jnp.arange(q + 1, dtype=F32)
    pw = jnp.exp(ldt[:, None, :] * tau[None, :, None])
    kt = jnp.real(jnp.einsum('gpn,gtn,gnq->gqtp', c, pw[:, :q], b_bar, precision=hp)).reshape(
        S5_GROUPS, S5_GROUP, S5_ROW)
    halves = lambda lo, hi: jnp.concatenate([lo, hi], axis=-1)
    bt = jnp.transpose(b_bar, (0, 2, 1))
    small = [halves(jnp.real(pw), jnp.real(pw)), halves(jnp.imag(pw), jnp.imag(pw)),
             halves(jnp.real(bt), jnp.imag(bt)), halves(-jnp.imag(bt), jnp.real(bt)),
             halves(jnp.real(c), -jnp.imag(c)), halves(-jnp.imag(c), -jnp.real(c))]
    grp = lambda shape: pl.BlockSpec((ng,) + shape, lambda j: (j, 0, 0))
    shapes = [(S5_ROW, S5_ROW), (S5_ROW, 2 * S5_STATE), (S5_ROW, 2 * S5_STATE)]
    m, wst, wofft = pl.pallas_call(
        _s5_table_body,
        grid=(S5_GROUPS // ng,),
        in_specs=[grp(kt.shape[1:])] + [grp(a.shape[1:]) for a in small],
        out_specs=[grp(sh) for sh in shapes],
        out_shape=[jax.ShapeDtypeStruct((S5_GROUPS,) + sh, BF16) for sh in shapes],
        compiler_params=_params("parallel"),
        name="s5_tables",
    )(kt, *small)
    nlev = int(math.log2(nblk))
    step = jnp.exp(ldt[:, None, :] * (q * 2.0 ** jnp.arange(nlev, dtype=F32))[None, :, None])
    a1 = halves(jnp.real(step), jnp.real(step))
    a2 = halves(-jnp.imag(step), jnp.imag(step))
    return m, wst, wofft, a1, a2


def _lin_levels(c):
    return [c >> (i + 1) for i in range(int(math.log2(c)))]


def _lin_tables(c):
    t = np.arange(c)[:, None]
    j = np.arange(c)[None, :]
    lvl = np.full((c, c), -1, np.int32)
    lvl[np.arange(c), np.arange(c)] = 0
    sgn = []
    for i, b in enumerate(_lin_levels(c)):
        same = (t // (2 * b)) == (j // (2 * b))
        second = (t % (2 * b)) >= b
        lvl[same & second & ((j % (2 * b)) < b)] = i + 1
        sgn.append(np.broadcast_to(np.where(second, 1.0, -1.0) * math.log2(math.e), (c, LANES)))
    return (jnp.asarray((j <= t).astype(np.float32), BF16), jnp.asarray(lvl),
            jnp.asarray(np.stack(sgn), F32))


def _gate_factors(g, tril, sgn_ref, gcum_ref):
    c = LIN_CHUNK
    gcum = _sel_dot(tril, g)
    gcum_ref[...] = gcum
    pos = lax.broadcasted_iota(jnp.int32, (c, LANES), 0)
    factors = []
    for i, b in enumerate(_lin_levels(c)):
        if 2 * b >= SUBLANES:
            mids = [jnp.broadcast_to(gcum_ref[blk * 2 * b + b - 1:blk * 2 * b + b, :], (2 * b, LANES))
                    for blk in range(c // (2 * b))]
            gmid = mids[0] if len(mids) == 1 else jnp.concatenate(mids, axis=0)
        elif b == 2:
            p4 = pos & 3
            gmid = jnp.where(p4 == 0, pltpu.roll(gcum, c - 1, axis=0),
                             jnp.where(p4 == 1, gcum,
                                       jnp.where(p4 == 2, pltpu.roll(gcum, 1, axis=0),
                                                 pltpu.roll(gcum, 2, axis=0))))
        else:
            gmid = jnp.where((pos & 1) == 1, pltpu.roll(gcum, 1, axis=0), gcum)
        factors.append(jnp.exp2((gcum - gmid) * sgn_ref[i]).astype(BF16))
    e_cum = jnp.exp(gcum)
    return e_cum, e_cum.astype(BF16), jnp.exp(gcum[c - 1:c, :] - gcum).astype(BF16), factors


def _lin_body(h_ref, gqk_ref, gv_ref, gr_ref, glr_ref, loglb_ref, log1mlb_ref, onemlb_ref, hnorm_ref,
              wup_ref, bgate_ref, gnorm_ref, tril_ref, lvl_ref, sgn_ref, o_ref, state_ref, gcum_ref):
    c = LIN_CHUNK

    @pl.when(pl.program_id(1) == 0)
    def _():
        state_ref[...] = jnp.zeros(state_ref.shape, F32)

    def chunk(ci, carry):
        rows = pl.ds(pl.multiple_of(ci * c, c), c)

        hq = h_ref[rows, 0:HGRN_WIDTH]
        hf = h_ref[rows, HGRN_WIDTH:2 * HGRN_WIDTH]
        lo = loglb_ref[...]
        hi_ = log1mlb_ref[...] + _log_sigmoid(hf)
        log_f = jnp.maximum(lo, hi_) + jnp.log1p(jnp.exp(-jnp.abs(lo - hi_)))
        q_c = _silu(hq)
        k_c = onemlb_ref[...] * _sigmoid(-hf)
        pre = _dot(glr_ref[rows, :].astype(BF16), wup_ref[...]) + bgate_ref[...]
        log_a = _log_sigmoid(pre) * (1.0 / GLA_TAU)
        q_d = gqk_ref[rows, 0:GLA_QK] * (GLA_DK ** -0.5)
        k_d = gqk_ref[rows, GLA_QK:2 * GLA_QK]

        lvl = lvl_ref[...]
        lane = lax.broadcasted_iota(jnp.int32, (c, LANES), 1)
        tril = tril_ref[...]

        heads = []
        for hd in range(LIN_HEADS):
            ls = slice(hd * HEAD_W, (hd + 1) * HEAD_W)
            heads.append((hd, q_c[:, ls], k_c[:, ls], hd,
                          h_ref[rows, 2 * HGRN_WIDTH + hd * HEAD_W:2 * HGRN_WIDTH + (hd + 1) * HEAD_W],
                          hnorm_ref[:, ls],
                          h_ref[rows, 3 * HGRN_WIDTH + hd * HEAD_W:3 * HGRN_WIDTH + (hd + 1) * HEAD_W], ls))
        for hd in range(LIN_HEADS):
            tile, half = divmod(hd, LANES // GLA_DK)
            ts = slice(tile * LANES, (tile + 1) * LANES)
            ls = slice(hd * HEAD_W, (hd + 1) * HEAD_W)
            mine = (lane >> int(math.log2(GLA_DK))) == half
            heads.append((LIN_HEADS + hd, jnp.where(mine, q_d[:, ts], 0.0), jnp.where(mine, k_d[:, ts], 0.0),
                          LIN_HEADS + tile, gv_ref[rows, ls], gnorm_ref[:, ls], gr_ref[rows, ls],
                          slice(HGRN_WIDTH + hd * HEAD_W, HGRN_WIDTH + (hd + 1) * HEAD_W)))

        gates = [log_f[:, t * LANES:(t + 1) * LANES] for t in range(LIN_HEADS)]
        gates += [log_a[:, t * LANES:(t + 1) * LANES] for t in range(GLA_QK // LANES)]
        factors = [_gate_factors(g, tril, sgn_ref, gcum_ref.at[t]) for t, g in enumerate(gates)]
        qkb = [(q.astype(BF16), k.astype(BF16)) for _, q, k, *_ in heads]
        attns = [None] * len(heads)
        for i in range(1 + len(_lin_levels(c))):
            mask = lvl == i
            for n, ((qb, kb), head) in enumerate(zip(qkb, heads)):
                if i == 0:
                    attns[n] = jnp.where(mask, _dot_nt(qb, kb), 0.0)
                else:
                    e = factors[head[3]][3][i - 1]
                    attns[n] = jnp.where(mask, _dot_nt(qb * e, kb * e), attns[n])
        attns = [a.astype(BF16) for a in attns]
        for (qb, kb), attn, (idx, _, _, tile, v, norm_w, gate, cols) in zip(qkb, attns, heads):
            e_cum, e_cum_b, e_end_b, _ = factors[tile]
            vb = v.astype(BF16)
            state_t = state_ref[idx]
            o = _dot(attn, vb) + _dot_nt(qb * e_cum_b, state_t.astype(BF16))
            state_ref[idx] = state_t * e_cum[c - 1:c, :] + _dot_tn(vb, kb * e_end_b)
            o_ref[rows, cols] = (_rms_norm(o, norm_w) * _silu(gate)).astype(o_ref.dtype)
        return carry

    lax.fori_loop(0, h_ref.shape[0] // c, chunk, 0)


def _lin_mixer(h_cd, batch, seq, lb, hgrn_norm_w, gla_w_gate_up, gla_b_gate, gla_norm_w):
    c = LIN_CHUNK
    step = c * MIX_STEP_CHUNKS
    nc = seq // step
    tok = lambda cb: (lambda b, i: (b * nc + i, cb))
    lb = lb.astype(F32)[None, :]
    wup = jnp.zeros((LANES, GLA_QK), F32).at[:GLA_RANK].set(gla_w_gate_up).astype(BF16)
    tril, lvl, sgn = _lin_tables(c)
    consts = [jnp.log(lb), jnp.log1p(-lb), 1.0 - lb, hgrn_norm_w[None, :], wup, gla_b_gate[None, :],
              gla_norm_w[None, :], tril, lvl, sgn]
    in_specs = [
        pl.BlockSpec((step, 4 * HGRN_WIDTH), tok(CD_H // (4 * HGRN_WIDTH))),
        pl.BlockSpec((step, 2 * GLA_QK), tok(CD_GQK // (2 * GLA_QK))),
        pl.BlockSpec((step, GLA_WIDTH), tok(CD_GV // GLA_WIDTH)),
        pl.BlockSpec((step, GLA_WIDTH), tok(CD_GR // GLA_WIDTH)),
        pl.BlockSpec((step, LANES), tok(CD_GLR // LANES)),
    ] + [_resident(a.shape) for a in consts]
    return pl.pallas_call(
        _lin_body,
        grid=(batch, nc),
        in_specs=in_specs,
        out_specs=pl.BlockSpec((step, HGRN_WIDTH + GLA_WIDTH), tok(0)),
        out_shape=jax.ShapeDtypeStruct((batch * seq, HGRN_WIDTH + GLA_WIDTH), BF16),
        scratch_shapes=[pltpu.VMEM((2 * LIN_HEADS, HEAD_W, HEAD_W), F32),
                        pltpu.VMEM((LIN_HEADS + GLA_QK // LANES, c, LANES), F32)],
        compiler_params=_params("parallel", "arbitrary"),
        name="lin_mixer",
    )(h_cd, h_cd, h_cd, h_cd, h_cd, *consts)


def _cols(w, start, stop, width=None):
    part = w[:, start:stop].astype(BF16)
    if width is not None and width > stop - start:
        part = jnp.pad(part, ((0, 0), (0, width - (stop - start))))
    return part


def _hgrn_lower_bound(lb_logits, layer):
    cum = jnp.cumsum(jax.nn.softmax(lb_logits.astype(F32), axis=0), axis=0)
    return cum[layer] - cum[0]


def kernel(x, p, ln_g, ln_b, ffn_w_gate, ffn_w_up, ffn_w_down, ple_w_gate, ple_w_proj, ab_w_in, ab_w_out,
           ssd_conv_w, ssd_conv_b, ssd_dt_bias, ssd_a_log, ssd_d, ssd_norm_w, s5_lambda_re, s5_lambda_im,
           s5_log_dt, s5_b_re, s5_b_im, s5_c_re, s5_c_im, s5_d, s5_w_glu, s5_b_glu, cd_w_in, cd_w_out,
           hgrn_lb_logits, hgrn_norm_w, gla_w_gate_up, gla_b_gate, gla_norm_w):
    batch, seq, _ = x.shape
    t = batch * seq
    x = x.reshape(t, D_MODEL)
    bf = lambda w: w.astype(BF16)
    ab_splits = (SSD_INNER, SSD_CONV_DIM, SSD_HEADS, S5_WIDTH)
    cd_splits = (HGRN_WIDTH, HGRN_WIDTH, HGRN_WIDTH, HGRN_WIDTH, GLA_QK, GLA_QK, GLA_WIDTH, GLA_RANK,
                 GLA_WIDTH)
    wg, wu, wd = bf(ffn_w_gate), bf(ffn_w_up), bf(ffn_w_down)
    pg, pp = bf(ple_w_gate), bf(ple_w_proj)
    p = p.reshape(DEPTH, t, PLE_DIM)
    ln_g = ln_g.reshape(DEPTH * 3, 1, D_MODEL)
    ln_b = ln_b.reshape(DEPTH * 3, 1, D_MODEL)
    w_ab_out, w_cd_out = bf(ab_w_out), bf(cd_w_out)
    w_glu, b_glu = bf(s5_w_glu), s5_b_glu[:, None, :]
    for i in range(DEPTH):
        j = i // 2
        x = _ffn_ln(x, wg, wu, wd, ln_g, ln_b, i, 0, 3 * i)
        if i % 2 == 0:
            w, e = ab_w_in[j], np.cumsum((0,) + ab_splits)
            h_ab = _proj_in(x, [(_cols(w, e[1], e[2]), AB_XBC), (_cols(w, e[0], e[1]), AB_Z),
                                (_cols(w, e[3], e[4]), AB_U), (_cols(w, e[2], e[3], AB_PACK - AB_DT), AB_DT)],
                            AB_PACK)
            y_a = _ssd_mixer(h_ab, batch, seq, ssd_conv_w[j], ssd_conv_b[j], ssd_dt_bias[j], ssd_a_log[j],
                             ssd_d[j], ssd_norm_w[j])
            tables = _s5_tables(s5_lambda_re[j], s5_lambda_im[j], s5_log_dt[j], s5_b_re[j], s5_b_im[j],
                                s5_c_re[j], s5_c_im[j], seq // S5_Q)
            y_s = _s5_core(h_ab, s5_d[j].reshape(1, S5_WIDTH).astype(F32), batch, seq, tables)
            mix = ("ab", (y_a, y_s), [(w_ab_out, (j,)), (w_glu, (j,)), (b_glu, (j,))])
        else:
            w, e = cd_w_in[j], np.cumsum((0,) + cd_splits)
            h_cd = _proj_in(x, [(_cols(w, e[0], e[7]), CD_H), (_cols(w, e[8], e[9]), CD_GR),
                                (_cols(w, e[7], e[8], CD_PACK - CD_GLR), CD_GLR)], CD_PACK)
            o_cd = _lin_mixer(h_cd, batch, seq, _hgrn_lower_bound(hgrn_lb_logits, i), hgrn_norm_w[j],
                              gla_w_gate_up[j], gla_b_gate[j], gla_norm_w[j])
            mix = ("cd", (o_cd,), [(w_cd_out, (j,))])
        x = _ffn_ln(x, wg, wu, wd, ln_g, ln_b, i, 1, 3 * i + 2, mix=mix, ple=(p, pg, pp))
    return x.reshape(batch, seq, D_MODEL)
```

```python
import functools
import math

import jax
import jax.numpy as jnp
import numpy as np
from jax import lax
from jax.experimental import pallas as pl
from jax.experimental.pallas import tpu as pltpu

F32 = jnp.float32
BF16 = jnp.bfloat16

D_MODEL = 1024
D_FF = 2816
PLE_DIM = 256
DEPTH = 2
DN_ALPHA = (2.0 * DEPTH) ** 0.25
LN_EPS = 1e-5
NEG_BIG = -1e30
SSD_HEADS = 16
SSD_HEAD_DIM = 64
SSD_GROUPS = 4
SSD_STATE = 128
SSD_CONV = 4
SSD_INNER = 1024
SSD_BC = SSD_GROUPS * SSD_STATE
SSD_CONV_DIM = SSD_INNER + 2 * SSD_BC
SSD_GROUP_WIDTH = SSD_INNER // SSD_GROUPS
SSD_HEADS_PER_GROUP = SSD_HEADS // SSD_GROUPS
S5_WIDTH = 1024
S5_GROUPS = 64
S5_GROUP = 16
S5_STATE = 64
LIN_HEADS = 4
HGRN_WIDTH = 512
GLA_DK = 64
GLA_QK = LIN_HEADS * GLA_DK
GLA_WIDTH = 512
GLA_RANK = 16
GLA_TAU = 16.0
HEAD_W = 128

LANES = 128
SUBLANES = 8
VMEM_LIMIT = 58 * 1024 * 1024

TM = 512
TM_FFN = 1024
SUB_FFN = 512
FF_CHUNK = 256
SSD_CHUNK = 128
MIX_STEP_CHUNKS = 4
LIN_CHUNK = 128
S5_Q = 32
S5_ROW = S5_Q * S5_GROUP
S5_TILE_GROUPS = LANES // S5_GROUP
S5_PITCH = 40

AB_XBC, AB_Z, AB_U, AB_DT, AB_PACK = 0, 2048, 3072, 4096, 4224
CD_H, CD_GQK, CD_GV, CD_GR, CD_GLR, CD_PACK = 0, 2048, 2560, 3072, 3584, 3712


def _resident(shape):
    n = len(shape)
    return pl.BlockSpec(shape, lambda *_: (0,) * n, pipeline_mode=pl.Buffered(1))


def _dot(a, b):
    return jnp.dot(a, b, preferred_element_type=F32)


def _dot_nt(a, b):
    return lax.dot_general(a, b, (((1,), (1,)), ((), ())), preferred_element_type=F32)


def _dot_tn(a, b):
    return lax.dot_general(a, b, (((0,), (0,)), ((), ())), preferred_element_type=F32)


def _split3(v):
    hi = v.astype(BF16)
    r = v - hi.astype(F32)
    mid = r.astype(BF16)
    lo = (r - mid.astype(F32)).astype(BF16)
    return hi, mid, lo


def _sel_dot(sel, v):
    hi, mid, lo = _split3(v)
    return _dot(sel, hi) + _dot(sel, mid) + _dot(sel, lo)


def _dot_sel(v, sel):
    hi, mid, lo = _split3(v)
    return _dot(hi, sel) + _dot(mid, sel) + _dot(lo, sel)


def _sigmoid(x):
    return 1.0 / (1.0 + jnp.exp(-x))


def _silu(x):
    return x * _sigmoid(x)


def _log_sigmoid(x):
    return jnp.minimum(x, 0.0) - jnp.log1p(jnp.exp(-jnp.abs(x)))


def _softplus(x):
    return jnp.maximum(x, 0.0) + jnp.log1p(jnp.exp(-jnp.abs(x)))


def _gelu_tanh(x):
    return 0.5 * x * (1.0 + jnp.tanh(math.sqrt(2.0 / math.pi) * (x + 0.044715 * (x * x * x))))


def _layer_norm(y, g, b):
    mu = jnp.mean(y, axis=-1, keepdims=True)
    yc = y - mu
    var = jnp.mean(yc * yc, axis=-1, keepdims=True)
    return yc * lax.rsqrt(var + LN_EPS) * g + b


def _rms_norm(y, w):
    return y * lax.rsqrt(jnp.mean(y * y, axis=-1, keepdims=True) + LN_EPS) * w


def _params(*sem):
    return pltpu.CompilerParams(dimension_semantics=sem, vmem_limit_bytes=VMEM_LIMIT)


def _ffn_body(mix, with_ple, x_ref, *refs):
    refs = list(refs)
    take = lambda n: [refs.pop(0) for _ in range(n)]
    if mix == "ab":
        ya_ref, ys_ref, wout_ref, wglu_ref, bglu_ref, g1_ref, b1_ref = take(7)
    elif mix == "cd":
        oc_ref, wout_ref, g1_ref, b1_ref = take(4)
    wg_ref, wu_ref, wd_ref, g_ref, b_ref = take(5)
    if with_ple:
        p_ref, pg_ref, pp_ref = take(3)
    o_ref, h_ref = refs
    for r in range(x_ref.shape[0] // SUB_FFN):
        rows = slice(r * SUB_FFN, (r + 1) * SUB_FFN)
        x = x_ref[rows, :]
        if mix == "ab":
            y = _gelu_tanh(ys_ref[rows, :].astype(F32))
            gate = _sigmoid(_dot(y.astype(BF16), wglu_ref[...]) + bglu_ref[...])
            m = _dot(ya_ref[rows, :], wout_ref[0:SSD_INNER, :]) + _dot((y * gate).astype(BF16), wout_ref[SSD_INNER:, :])
            x = _layer_norm(DN_ALPHA * x + m, g1_ref[...], b1_ref[...])
        elif mix == "cd":
            x = _layer_norm(DN_ALPHA * x + _dot(oc_ref[rows, :], wout_ref[...]), g1_ref[...], b1_ref[...])
        xb = x.astype(BF16)
        for c in range(D_FF // FF_CHUNK):
            sl = slice(c * FF_CHUNK, (c + 1) * FF_CHUNK)
            gate = _dot(xb, wg_ref[:, sl])
            up = _dot(xb, wu_ref[:, sl])
            h_ref[:, sl] = (_silu(gate) * up).astype(BF16)
        y = _layer_norm(DN_ALPHA * x + 0.5 * _dot(h_ref[...], wd_ref[...]), g_ref[...], b_ref[...])
        if with_ple:
            gate = _sigmoid(_dot(y.astype(BF16), pg_ref[...]))
            y = y + gate * _dot(p_ref[rows, :].astype(BF16), pp_ref[...])
        o_ref[rows, :] = y


def _pick(a, *idx):
    rest = a.shape[len(idx):]
    return pl.BlockSpec((None,) * len(idx) + rest, lambda *_: idx + (0,) * len(rest),
                        pipeline_mode=pl.Buffered(1))


def _ffn_ln(x, wg, wu, wd, ln_g, ln_b, layer, pos, ln_idx, mix=None, ple=None):
    t = x.shape[0]
    row = lambda i: (i, 0)
    in_specs = [pl.BlockSpec((TM_FFN, D_MODEL), row)]
    args = [x]
    if mix is not None:
        kind, acts, consts = mix
        in_specs += [pl.BlockSpec((TM_FFN, a.shape[1]), row) for a in acts]
        in_specs += [_pick(a, *idx) for a, idx in consts]
        in_specs += [_pick(ln_g, ln_idx - 1), _pick(ln_b, ln_idx - 1)]
        args += list(acts) + [a for a, _ in consts] + [ln_g, ln_b]
    in_specs += [_pick(wg, layer, pos), _pick(wu, layer, pos), _pick(wd, layer, pos),
                 _pick(ln_g, ln_idx), _pick(ln_b, ln_idx)]
    args += [wg, wu, wd, ln_g, ln_b]
    if ple is not None:
        p, pg, pp = ple
        in_specs += [pl.BlockSpec((None, TM_FFN, PLE_DIM), lambda i: (layer, i, 0)), _pick(pg, layer), _pick(pp, layer)]
        args += [p, pg, pp]
    return pl.pallas_call(
        functools.partial(_ffn_body, None if mix is None else mix[0], ple is not None),
        grid=(t // TM_FFN,),
        in_specs=in_specs,
        out_specs=pl.BlockSpec((TM_FFN, D_MODEL), row),
        out_shape=jax.ShapeDtypeStruct((t, D_MODEL), F32),
        scratch_shapes=[pltpu.VMEM((SUB_FFN, D_FF), BF16)],
        compiler_params=_params("parallel"),
        name="ffn_ln" if mix is None else "mix_ffn_ln_ple",
    )(*args)


def _proj_in_body(starts, x_ref, *refs):
    w_refs, o_ref = refs[:-1], refs[-1]
    xb = x_ref[...].astype(BF16)
    step = 512
    for w_ref, c0 in zip(w_refs, starts):
        n = w_ref.shape[1]
        for s0 in range(0, n, step):
            s1 = min(s0 + step, n)
            o_ref[:, c0 + s0:c0 + s1] = _dot(xb, w_ref[:, s0:s1])


def _proj_in(x, parts, n_out):
    t = x.shape[0]
    weights = [w for w, _ in parts]
    spans = sorted((c0, c0 + w.shape[1]) for w, c0 in parts)
    assert spans[0][0] == 0 and spans[-1][1] == n_out and all(a[1] == b[0] for a, b in zip(spans, spans[1:]))
    return pl.pallas_call(
        functools.partial(_proj_in_body, tuple(c0 for _, c0 in parts)),
        grid=(t // TM,),
        in_specs=[pl.BlockSpec((TM, D_MODEL), lambda i: (i, 0))] + [_resident(w.shape) for w in weights],
        out_specs=pl.BlockSpec((TM, n_out), lambda i: (i, 0)),
        out_shape=jax.ShapeDtypeStruct((t, n_out), F32),
        compiler_params=_params("parallel"),
        name="proj_in",
    )(x, *weights)


def _ssd_body(xbc_ref, z_ref, dt_ref, convw_ref, convb_ref, dtb_ref, alog_ref, dskip_ref,
              normw_ref, tril_ref, expand_ref, shift_ref, o_ref, xb_ref, state_ref):
    ch = SSD_CHUNK
    gw = SSD_GROUP_WIDTH

    @pl.when(pl.program_id(1) == 0)
    def _():
        xb_ref[0:ch, :] = jnp.zeros((ch, SSD_CONV_DIM), BF16)
        state_ref[...] = jnp.zeros(state_ref.shape, F32)

    def chunk(i, carry):
        rows = pl.ds(pl.multiple_of(i * ch, ch), ch)

        def conv_silu(cols):
            cur = xbc_ref[rows, cols]
            cur_b = cur.astype(BF16)
            xb_ref[ch:2 * ch, cols] = cur_b
            both = xb_ref[:, cols]
            acc = cur * convw_ref[SSD_CONV - 1:SSD_CONV, cols] + convb_ref[:, cols]
            for j in range(1, SSD_CONV):
                acc = acc + _dot(shift_ref[j - 1], both) * convw_ref[SSD_CONV - 1 - j:SSD_CONV - j, cols]
            xb_ref[0:ch, cols] = cur_b
            return _silu(acc)

        dt = _softplus(dt_ref[rows, :] + dtb_ref[...])
        da = dt * (-jnp.exp(alog_ref[...]))
        a_cum = _sel_dot(tril_ref[...], da)
        a_cum_t = a_cum.T
        dt_hi, dt_mid, _ = _split3(dt)
        dt_e = _dot(dt_hi, expand_ref[...]) + _dot(dt_mid, expand_ref[...])
        ac_e = _dot_sel(a_cum, expand_ref[...])
        ac_last = ac_e[ch - 1:ch, :]
        to_end = jnp.exp(ac_last - ac_e)
        carry_scale = jnp.exp(ac_e)
        chunk_decay = jnp.exp(ac_last)

        row = lax.broadcasted_iota(jnp.int32, (ch, ch), 0)
        col = lax.broadcasted_iota(jnp.int32, (ch, ch), 1)
        causal = col <= row
        lane_head = lax.broadcasted_iota(jnp.int32, (ch, gw), 1) >> int(math.log2(SSD_HEAD_DIM))

        groups = range(SSD_GROUPS)
        cols = [slice(g * gw, (g + 1) * gw) for g in groups]
        xs = [conv_silu(cols[g]) for g in groups]
        bc = [conv_silu(slice(SSD_INNER + k * gw, SSD_INNER + (k + 1) * gw)).astype(BF16)
              for k in range(2 * SSD_BC // gw)]
        bc = [v[:, half * SSD_STATE:(half + 1) * SSD_STATE] for v in bc for half in range(gw // SSD_STATE)]
        bs, cs = bc[:SSD_GROUPS], bc[SSD_GROUPS:]
        xdt = [xs[g] * dt_e[:, cols[g]] for g in groups]
        xdt_b = [v.astype(BF16) for v in xdt]
        scores = [_dot_nt(cs[g], bs[g]).astype(BF16) for g in groups]
        ys = []
        for g in groups:
            y_g = jnp.zeros((ch, gw), F32)
            for hh in range(SSD_HEADS_PER_GROUP):
                h = g * SSD_HEADS_PER_GROUP + hh
                seg = a_cum[:, h:h + 1] - a_cum_t[h:h + 1, :]
                decay = jnp.exp(jnp.where(causal, seg, NEG_BIG)).astype(BF16)
                full = _dot(scores[g] * decay, xdt_b[g])
                y_g = jnp.where(lane_head == hh, full, y_g)
            ys.append(y_g)
        for g in groups:
            gs = cols[g]
            state = state_ref[g]
            y_g = ys[g] + _dot(cs[g], state.astype(BF16)) * carry_scale[:, gs]
            state_ref[g] = state * chunk_decay[:, gs] + _dot_tn(bs[g], (xdt[g] * to_end[:, gs]).astype(BF16))
            y_g = y_g + xs[g] * dskip_ref[:, gs]
            y_g = y_g * _silu(z_ref[rows, gs])
            o_ref[rows, gs] = _rms_norm(y_g, normw_ref[:, gs]).astype(o_ref.dtype)
        return carry

    lax.fori_loop(0, xbc_ref.shape[0] // ch, chunk, 0)


def _ssd_mixer(h_ab, batch, seq, conv_w, conv_b, dt_bias, a_log, d_skip, norm_w):
    ch = SSD_CHUNK
    step = ch * MIX_STEP_CHUNKS
    nc = seq // step
    tok = lambda cb: (lambda b, c: (b * nc + c, cb))
    pad_heads = lambda v: jnp.zeros((1, LANES), F32).at[0, :SSD_HEADS].set(v.astype(F32))
    tril = jnp.asarray(np.tril(np.ones((ch, ch), np.float32)), BF16)
    expand = np.zeros((LANES, SSD_INNER), np.float32)
    for h in range(SSD_HEADS):
        expand[h, h * SSD_HEAD_DIM:(h + 1) * SSD_HEAD_DIM] = 1.0
    expand = jnp.asarray(expand, BF16)
    shift = np.zeros((SSD_CONV - 1, ch, 2 * ch), np.float32)
    for j in range(1, SSD_CONV):
        shift[j - 1, np.arange(ch), ch + np.arange(ch) - j] = 1.0
    shift = jnp.asarray(shift, BF16)
    dskip_e = jnp.repeat(d_skip.astype(F32), SSD_HEAD_DIM)[None, :]
    consts = [conv_w, conv_b[None, :], pad_heads(dt_bias), pad_heads(a_log), dskip_e,
              norm_w[None, :], tril, expand, shift]
    in_specs = [
        pl.BlockSpec((step, SSD_CONV_DIM), tok(AB_XBC // SSD_CONV_DIM)),
        pl.BlockSpec((step, SSD_INNER), tok(AB_Z // SSD_INNER)),
        pl.BlockSpec((step, LANES), tok(AB_DT // LANES)),
    ] + [_resident(a.shape) for a in consts]
    return pl.pallas_call(
        _ssd_body,
        grid=(batch, nc),
        in_specs=in_specs,
        out_specs=pl.BlockSpec((step, SSD_INNER), tok(0)),
        out_shape=jax.ShapeDtypeStruct((batch * seq, SSD_INNER), BF16),
        scratch_shapes=[pltpu.VMEM((2 * ch, SSD_CONV_DIM), BF16),
                        pltpu.VMEM((SSD_GROUPS, SSD_STATE, SSD_GROUP_WIDTH), F32)],
        compiler_params=_params("parallel", "arbitrary"),
        name="ssd_mixer",
    )(h_ab, h_ab, h_ab, *consts)


def _s5_slot(g, s):
    return (s + g) % S5_TILE_GROUPS


def _s5_body(nblk, u_ref, d_ref, m_ref, wst_ref, wofft_ref, a1_ref, a2_ref, o_ref, uscr_ref, yscr_ref, ug_ref,
             yg_ref):
    ng = S5_TILE_GROUPS
    ncol = S5_Q // ng
    nstrip = nblk // SUBLANES
    slot = lax.broadcasted_iota(jnp.int32, (SUBLANES, LANES), 1) >> int(math.log2(S5_GROUP))
    blk = lax.broadcasted_iota(jnp.int32, (nblk, 2 * S5_STATE), 0)

    def pitch_in(c, carry):
        src = pl.multiple_of(c * S5_Q, S5_Q)
        dst = pl.multiple_of(c * S5_PITCH, SUBLANES)
        uscr_ref[pl.ds(dst, S5_Q), :] = u_ref[pl.ds(src, S5_Q), :]
        return carry
    lax.fori_loop(0, nblk, pitch_in, 0, unroll=8)

    def gather(i, carry):
        base = pl.multiple_of(i * (SUBLANES * S5_PITCH), SUBLANES)
        row = pl.multiple_of(i * SUBLANES, SUBLANES)
        for m in range(ncol):
            rolled = []
            for s in range(ng):
                us = uscr_ref[pl.ds(base + ng * m + s, SUBLANES, stride=S5_PITCH), :]
                rolled.append(pltpu.roll(us, S5_GROUP * s, axis=1) if s else us)
            for g in range(ng):
                v = rolled[0]
                for s in range(1, ng):
                    v = jnp.where(slot == _s5_slot(g, s), rolled[s], v)
                ug_ref[pl.ds(row, SUBLANES), (g * ncol + m) * LANES:(g * ncol + m + 1) * LANES] = v
        return carry
    lax.fori_loop(0, nstrip, gather, 0, unroll=2)

    us = [ug_ref[:, g * S5_ROW:(g + 1) * S5_ROW].astype(BF16) for g in range(ng)]
    xs = [_dot(us[g], wst_ref[g]) for g in range(ng)]
    for k in range(int(math.log2(nblk))):
        sh = 1 << k
        for g in range(ng):
            prev = jnp.where(blk >= sh, pltpu.roll(xs[g], sh, axis=0), 0.0)
            xs[g] = (xs[g] + prev * a1_ref[g, k:k + 1, :]
                     + pltpu.roll(prev, S5_STATE, axis=1) * a2_ref[g, k:k + 1, :])
    for g in range(ng):
        x_in = jnp.where(blk >= 1, pltpu.roll(xs[g], 1, axis=0), 0.0)
        yg_ref[:, g * S5_ROW:(g + 1) * S5_ROW] = (_dot(us[g], m_ref[g])
                                                   + _dot_nt(x_in.astype(BF16), wofft_ref[g]))

    def scatter(i, carry):
        base = pl.multiple_of(i * (SUBLANES * S5_PITCH), SUBLANES)
        row = pl.multiple_of(i * SUBLANES, SUBLANES)
        for m in range(ncol):
            cols = [yg_ref[pl.ds(row, SUBLANES), (g * ncol + m) * LANES:(g * ncol + m + 1) * LANES]
                    for g in range(ng)]
            for tt in range(ng):
                z = cols[0]
                for g in range(1, ng):
                    z = jnp.where(slot == _s5_slot(g, tt), cols[g], z)
                sh = (LANES - S5_GROUP * tt) % LANES
                yscr_ref[pl.ds(base + ng * m + tt, SUBLANES, stride=S5_PITCH), :] = (
                    pltpu.roll(z, sh, axis=1) if sh else z)
        return carry
    lax.fori_loop(0, nstrip, scatter, 0, unroll=2)

    def pitch_out(c, carry):
        src = pl.multiple_of(c * S5_PITCH, SUBLANES)
        dst = pl.multiple_of(c * S5_Q, S5_Q)
        y = yscr_ref[pl.ds(src, S5_Q), :] + d_ref[...] * u_ref[pl.ds(dst, S5_Q), :]
        o_ref[pl.ds(dst, S5_Q), :] = y.astype(o_ref.dtype)
        return carry
    lax.fori_loop(0, nblk, pitch_out, 0, unroll=8)


def _s5_core(h_ab, d_skip, batch, seq, tables):
    nblk = seq // S5_Q
    ng = S5_TILE_GROUPS
    grp = lambda a: pl.BlockSpec((ng,) + a.shape[1:], lambda j, b: (j, 0, 0))
    return pl.pallas_call(
        functools.partial(_s5_body, nblk),
        grid=(S5_GROUPS // ng, batch),
        in_specs=[pl.BlockSpec((seq, LANES), lambda j, b: (b, AB_U // LANES + j)),
                  pl.BlockSpec((1, LANES), lambda j, b: (0, j))]
        + [grp(a) for a in tables],
        out_specs=pl.BlockSpec((seq, LANES), lambda j, b: (b, j)),
        out_shape=jax.ShapeDtypeStruct((batch * seq, S5_WIDTH), BF16),
        scratch_shapes=[pltpu.VMEM((nblk * S5_PITCH, LANES), F32), pltpu.VMEM((nblk * S5_PITCH, LANES), F32),
                        pltpu.VMEM((nblk, ng * S5_ROW), F32), pltpu.VMEM((nblk, ng * S5_ROW), F32)],
        compiler_params=_params("parallel", "parallel"),
        name="s5_core",
    )(h_ab, d_skip, *tables)


def _s5_table_body(kt_ref, pw_rr_ref, pw_ii_ref, b1_ref, b2_ref, c1_ref, c2_ref, m_ref, wst_ref, wofft_ref):
    ng = S5_TILE_GROUPS
    ncol = S5_Q // ng
    lane = lax.broadcasted_iota(jnp.int32, (S5_GROUP, LANES), 1)
    for g in range(ng):
        strip = [jnp.zeros((S5_GROUP, LANES), F32)] * ncol + [kt_ref[g, :, c * LANES:(c + 1) * LANES]
                                                            for c in range(ncol)]
        b1, b2, c1, c2 = b1_ref[g], b2_ref[g], c1_ref[g], c2_ref[g]
        for s in range(S5_Q):
            start = ncol * LANES - S5_GROUP * s
            a, sh = start // LANES, start % LANES
            k = ng * (s // ng) + _s5_slot(g, s % ng)
            rows = slice(S5_GROUP * k, S5_GROUP * (k + 1))
            for c in range(ncol):
                if sh:
                    w = jnp.where(lane < LANES - sh, pltpu.roll(strip[a + c], LANES - sh, axis=1),
                                  pltpu.roll(strip[a + c + 1], LANES - sh, axis=1))
                else:
                    w = strip[a + c]
                if g:
                    w = pltpu.roll(w, S5_GROUP * g, axis=1)
                m_ref[g, rows, c * LANES:(c + 1) * LANES] = w.astype(BF16)
            e = S5_Q - 1 - s
            wst_ref[g, rows, :] = (pw_rr_ref[g, e:e + 1, :] * b1 + pw_ii_ref[g, e:e + 1, :] * b2).astype(BF16)
            wofft_ref[g, rows, :] = (pw_rr_ref[g, s + 1:s + 2, :] * c1
                                     + pw_ii_ref[g, s + 1:s + 2, :] * c2).astype(BF16)


def _s5_tables(lam_re, lam_im, log_dt, b_re, b_im, c_re, c_im, nblk):
    q, ng = S5_Q, S5_TILE_GROUPS
    hp = lax.Precision.HIGHEST
    lam = lax.complex(lam_re.astype(F32), lam_im.astype(F32))
    ldt = lam * jnp.exp(log_dt.astype(F32))[:, None]
    lam_bar = jnp.exp(ldt)
    b_bar = ((lam_bar - 1.0) / lam)[..., None] * lax.complex(b_re.astype(F32), b_im.astype(F32))
    c = lax.complex(c_re.astype(F32), c_im.astype(F32))
    tau = jnp.arange(q + 1, dtype=F32)
    pw = jnp.exp(ldt[:, None, :] * tau[None, :, None])
    kt = jnp.real(jnp.einsum('gpn,gtn,gnq->gqtp', c, pw[:, :q], b_bar, precision=hp)).reshape(
        S5_GROUPS, S5_GROUP, S5_ROW)
    halves = lambda lo, hi: jnp.concatenate([lo, hi], axis=-1)
    bt = jnp.transpose(b_bar, (0, 2, 1))
    small = [halves(jnp.real(pw), jnp.real(pw)), halves(jnp.imag(pw), jnp.imag(pw)),
             halves(jnp.real(bt), jnp.imag(bt)), halves(-jnp.imag(bt), jnp.real(bt)),
             halves(jnp.real(c), -jnp.imag(c)), halves(-jnp.imag(c), -jnp.real(c))]
    grp = lambda shape: pl.BlockSpec((ng,) + shape, lambda j: (j, 0, 0))
    shapes = [(S5_ROW, S5_ROW), (S5_ROW, 2 * S5_STATE), (S5_ROW, 2 * S5_STATE)]
    m, wst, wofft = pl.pallas_call(
        _s5_table_body,
        grid=(S5_GROUPS // ng,),
        in_specs=[grp(kt.shape[1:])] + [grp(a.shape[1:]) for a in small],
        out_specs=[grp(sh) for sh in shapes],
        out_shape=[jax.ShapeDtypeStruct((S5_GROUPS,) + sh, BF16) for sh in shapes],
        compiler_params=_params("parallel"),
        name="s5_tables",
    )(kt, *small)
    nlev = int(math.log2(nblk))
    step = jnp.exp(ldt[:, None, :] * (q * 2.0 ** jnp.arange(nlev, dtype=F32))[None, :, None])
    a1 = halves(jnp.real(step), jnp.real(step))
    a2 = halves(-jnp.imag(step), jnp.imag(step))
    return m, wst, wofft, a1, a2


def _lin_levels(c):
    return [c >> (i + 1) for i in range(int(math.log2(c)))]


def _lin_tables(c):
    t = np.arange(c)[:, None]
    j = np.arange(c)[None, :]
    lvl = np.full((c, c), -1, np.int32)
    lvl[np.arange(c), np.arange(c)] = 0
    sgn = []
    for i, b in enumerate(_lin_levels(c)):
        same = (t // (2 * b)) == (j // (2 * b))
        second = (t % (2 * b)) >= b
        lvl[same & second & ((j % (2 * b)) < b)] = i + 1
        sgn.append(np.broadcast_to(np.where(second, 1.0, -1.0) * math.log2(math.e), (c, LANES)))
    return (jnp.asarray((j <= t).astype(np.float32), BF16), jnp.asarray(lvl),
            jnp.asarray(np.stack(sgn), F32))


def _gate_factors(g, tril, sgn_ref, gcum_ref):
    c = LIN_CHUNK
    gcum = _sel_dot(tril, g)
    gcum_ref[...] = gcum
    pos = lax.broadcasted_iota(jnp.int32, (c, LANES), 0)
    factors = []
    for i, b in enumerate(_lin_levels(c)):
        if 2 * b >= SUBLANES:
            mids = [jnp.broadcast_to(gcum_ref[blk * 2 * b + b - 1:blk * 2 * b + b, :], (2 * b, LANES))
                    for blk in range(c // (2 * b))]
            gmid = mids[0] if len(mids) == 1 else jnp.concatenate(mids, axis=0)
        elif b == 2:
            p4 = pos & 3
            gmid = jnp.where(p4 == 0, pltpu.roll(gcum, c - 1, axis=0),
                             jnp.where(p4 == 1, gcum,
                                       jnp.where(p4 == 2, pltpu.roll(gcum, 1, axis=0),
                                                 pltpu.roll(gcum, 2, axis=0))))
        else:
            gmid = jnp.where((pos & 1) == 1, pltpu.roll(gcum, 1, axis=0), gcum)
        factors.append(jnp.exp2((gcum - gmid) * sgn_ref[i]).astype(BF16))
    e_cum = jnp.exp(gcum)
    return e_cum, e_cum.astype(BF16), jnp.exp(gcum[c - 1:c, :] - gcum).astype(BF16), factors


def _lin_body(h_ref, gqk_ref, gv_ref, gr_ref, glr_ref, loglb_ref, log1mlb_ref, onemlb_ref, hnorm_ref,
              wup_ref, bgate_ref, gnorm_ref, tril_ref, lvl_ref, sgn_ref, o_ref, state_ref, gcum_ref):
    c = LIN_CHUNK

    @pl.when(pl.program_id(1) == 0)
    def _():
        state_ref[...] = jnp.zeros(state_ref.shape, F32)

    def chunk(ci, carry):
        rows = pl.ds(pl.multiple_of(ci * c, c), c)

        hq = h_ref[rows, 0:HGRN_WIDTH]
        hf = h_ref[rows, HGRN_WIDTH:2 * HGRN_WIDTH]
        lo = loglb_ref[...]
        hi_ = log1mlb_ref[...] + _log_sigmoid(hf)
        log_f = jnp.maximum(lo, hi_) + jnp.log1p(jnp.exp(-jnp.abs(lo - hi_)))
        q_c = _silu(hq)
        k_c = onemlb_ref[...] * _sigmoid(-hf)
        pre = _dot(glr_ref[rows, :].astype(BF16), wup_ref[...]) + bgate_ref[...]
        log_a = _log_sigmoid(pre) * (1.0 / GLA_TAU)
        q_d = gqk_ref[rows, 0:GLA_QK] * (GLA_DK ** -0.5)
        k_d = gqk_ref[rows, GLA_QK:2 * GLA_QK]

        lvl = lvl_ref[...]
        lane = lax.broadcasted_iota(jnp.int32, (c, LANES), 1)
        tril = tril_ref[...]

        heads = []
        for hd in range(LIN_HEADS):
            ls = slice(hd * HEAD_W, (hd + 1) * HEAD_W)
            heads.append((hd, q_c[:, ls], k_c[:, ls], hd,
                          h_ref[rows, 2 * HGRN_WIDTH + hd * HEAD_W:2 * HGRN_WIDTH + (hd + 1) * HEAD_W],
                          hnorm_ref[:, ls],
                          h_ref[rows, 3 * HGRN_WIDTH + hd * HEAD_W:3 * HGRN_WIDTH + (hd + 1) * HEAD_W], ls))
        for hd in range(LIN_HEADS):
            tile, half = divmod(hd, LANES // GLA_DK)
            ts = slice(tile * LANES, (tile + 1) * LANES)
            ls = slice(hd * HEAD_W, (hd + 1) * HEAD_W)
            mine = (lane >> int(math.log2(GLA_DK))) == half
            heads.append((LIN_HEADS + hd, jnp.where(mine, q_d[:, ts], 0.0), jnp.where(mine, k_d[:, ts], 0.0),
                          LIN_HEADS + tile, gv_ref[rows, ls], gnorm_ref[:, ls], gr_ref[rows, ls],
                          slice(HGRN_WIDTH + hd * HEAD_W, HGRN_WIDTH + (hd + 1) * HEAD_W)))

        gates = [log_f[:, t * LANES:(t + 1) * LANES] for t in range(LIN_HEADS)]
        gates += [log_a[:, t * LANES:(t + 1) * LANES] for t in range(GLA_QK // LANES)]
        factors = [_gate_factors(g, tril, sgn_ref, gcum_ref.at[t]) for t, g in enumerate(gates)]
        qkb = [(q.astype(BF16), k.astype(BF16)) for _, q, k, *_ in heads]
        attns = [None] * len(heads)
        for i in range(1 + len(_lin_levels(c))):
            mask = lvl == i
            for n, ((qb, kb), head) in enumerate(zip(qkb, heads)):
                if i == 0:
                    attns[n] = jnp.where(mask, _dot_nt(qb, kb), 0.0)
                else:
                    e = factors[head[3]][3][i - 1]
                    attns[n] = jnp.where(mask, _dot_nt(qb * e, kb * e), attns[n])
        attns = [a.astype(BF16) for a in attns]
        for (qb, kb), attn, (idx, _, _, tile, v, norm_w, gate, cols) in zip(qkb, attns, heads):
            e_cum, e_cum_b, e_end_b, _ = factors[tile]
            vb = v.astype(BF16)
            state_t = state_ref[idx]
            o = _dot(attn, vb) + _dot_nt(qb * e_cum_b, state_t.astype(BF16))
            state_ref[idx] = state_t * e_cum[c - 1:c, :] + _dot_tn(vb, kb * e_end_b)
            o_ref[rows, cols] = (_rms_norm(o, norm_w) * _silu(gate)).astype(o_ref.dtype)
        return carry

    lax.fori_loop(0, h_ref.shape[0] // c, chunk, 0)


def _lin_mixer(h_cd, batch, seq, lb, hgrn_norm_w, gla_w_gate_up, gla_b_gate, gla_norm_w):
    c = LIN_CHUNK
    step = c * MIX_STEP_CHUNKS
    nc = seq // step
    tok = lambda cb: (lambda b, i: (b * nc + i, cb))
    lb = lb.astype(F32)[None, :]
    wup = jnp.zeros((LANES, GLA_QK), F32).at[:GLA_RANK].set(gla_w_gate_up).astype(BF16)
    tril, lvl, sgn = _lin_tables(c)
    consts = [jnp.log(lb), jnp.log1p(-lb), 1.0 - lb, hgrn_norm_w[None, :], wup, gla_b_gate[None, :],
              gla_norm_w[None, :], tril, lvl, sgn]
    in_specs = [
        pl.BlockSpec((step, 4 * HGRN_WIDTH), tok(CD_H // (4 * HGRN_WIDTH))),
        pl.BlockSpec((step, 2 * GLA_QK), tok(CD_GQK // (2 * GLA_QK))),
        pl.BlockSpec((step, GLA_WIDTH), tok(CD_GV // GLA_WIDTH)),
        pl.BlockSpec((step, GLA_WIDTH), tok(CD_GR // GLA_WIDTH)),
        pl.BlockSpec((step, LANES), tok(CD_GLR // LANES)),
    ] + [_resident(a.shape) for a in consts]
    return pl.pallas_call(
        _lin_body,
        grid=(batch, nc),
        in_specs=in_specs,
        out_specs=pl.BlockSpec((step, HGRN_WIDTH + GLA_WIDTH), tok(0)),
        out_shape=jax.ShapeDtypeStruct((batch * seq, HGRN_WIDTH + GLA_WIDTH), BF16),
        scratch_shapes=[pltpu.VMEM((2 * LIN_HEADS, HEAD_W, HEAD_W), F32),
                        pltpu.VMEM((LIN_HEADS + GLA_QK // LANES, c, LANES), F32)],
        compiler_params=_params("parallel", "arbitrary"),
        name="lin_mixer",
    )(h_cd, h_cd, h_cd, h_cd, h_cd, *consts)


def _cols(w, start, stop, width=None):
    part = w[:, start:stop].astype(BF16)
    if width is not None and width > stop - start:
        part = jnp.pad(part, ((0, 0), (0, width - (stop - start))))
    return part


def _hgrn_lower_bound(lb_logits, layer):
    cum = jnp.cumsum(jax.nn.softmax(lb_logits.astype(F32), axis=0), axis=0)
    return cum[layer] - cum[0]


def kernel(x, p, ln_g, ln_b, ffn_w_gate, ffn_w_up, ffn_w_down, ple_w_gate, ple_w_proj, ab_w_in, ab_w_out,
           ssd_conv_w, ssd_conv_b, ssd_dt_bias, ssd_a_log, ssd_d, ssd_norm_w, s5_lambda_re, s5_lambda_im,
           s5_log_dt, s5_b_re, s5_b_im, s5_c_re, s5_c_im, s5_d, s5_w_glu, s5_b_glu, cd_w_in, cd_w_out,
           hgrn_lb_logits, hgrn_norm_w, gla_w_gate_up, gla_b_gate, gla_norm_w):
    batch, seq, _ = x.shape
    t = batch * seq
    x = x.reshape(t, D_MODEL)
    bf = lambda w: w.astype(BF16)
    ab_splits = (SSD_INNER, SSD_CONV_DIM, SSD_HEADS, S5_WIDTH)
    cd_splits = (HGRN_WIDTH, HGRN_WIDTH, HGRN_WIDTH, HGRN_WIDTH, GLA_QK, GLA_QK, GLA_WIDTH, GLA_RANK,
                 GLA_WIDTH)
    wg, wu, wd = bf(ffn_w_gate), bf(ffn_w_up), bf(ffn_w_down)
    pg, pp = bf(ple_w_gate), bf(ple_w_proj)
    p = p.reshape(DEPTH, t, PLE_DIM)
    ln_g = ln_g.reshape(DEPTH * 3, 1, D_MODEL)
    ln_b = ln_b.reshape(DEPTH * 3, 1, D_MODEL)
    w_ab_out, w_cd_out = bf(ab_w_out), bf(cd_w_out)
    w_glu, b_glu = bf(s5_w_glu), s5_b_glu[:, None, :]
    for i in range(DEPTH):
        j = i // 2
        x = _ffn_ln(x, wg, wu, wd, ln_g, ln_b, i, 0, 3 * i)
        if i % 2 == 0:
            w, e = ab_w_in[j], np.cumsum((0,) + ab_splits)
            h_ab = _proj_in(x, [(_cols(w, e[1], e[2]), AB_XBC), (_cols(w, e[0], e[1]), AB_Z),
                                (_cols(w, e[3], e[4]), AB_U), (_cols(w, e[2], e[3], AB_PACK - AB_DT), AB_DT)],
                            AB_PACK)
            y_a = _ssd_mixer(h_ab, batch, seq, ssd_conv_w[j], ssd_conv_b[j], ssd_dt_bias[j], ssd_a_log[j],
                             ssd_d[j], ssd_norm_w[j])
            tables = _s5_tables(s5_lambda_re[j], s5_lambda_im[j], s5_log_dt[j], s5_b_re[j], s5_b_im[j],
                                s5_c_re[j], s5_c_im[j], seq // S5_Q)
            y_s = _s5_core(h_ab, s5_d[j].reshape(1, S5_WIDTH).astype(F32), batch, seq, tables)
            mix = ("ab", (y_a, y_s), [(w_ab_out, (j,)), (w_glu, (j,)), (b_glu, (j,))])
        else:
            w, e = cd_w_in[j], np.cumsum((0,) + cd_splits)
            h_cd = _proj_in(x, [(_cols(w, e[0], e[7]), CD_H), (_cols(w, e[8], e[9]), CD_GR),
                                (_cols(w, e[7], e[8], CD_PACK - CD_GLR), CD_GLR)], CD_PACK)
            o_cd = _lin_mixer(h_cd, batch, seq, _hgrn_lower_bound(hgrn_lb_logits, i), hgrn_norm_w[j],
                              gla_w_gate_up[j], gla_b_gate[j], gla_norm_w[j])
            mix = ("cd", (o_cd,), [(w_cd_out, (j,))])
        x = _ffn_ln(x, wg, wu, wd, ln_g, ln_b, i, 1, 3 * i + 2, mix=mix, ple=(p, pg, pp))
    return x.reshape(batch, seq, D_MODEL)
```

```python
import functools
import math

import jax
import jax.numpy as jnp
import numpy as np
from jax import lax
from jax.experimental import pallas as pl
from jax.experimental.pallas import tpu as pltpu

F32 = jnp.float32
BF16 = jnp.bfloat16

D_MODEL = 1024
D_FF = 2816
PLE_DIM = 256
DEPTH = 2
DN_ALPHA = (2.0 * DEPTH) ** 0.25
LN_EPS = 1e-5
NEG_BIG = -1e30
SSD_HEADS = 16
SSD_HEAD_DIM = 64
SSD_GROUPS = 4
SSD_STATE = 128
SSD_CONV = 4
SSD_INNER = 1024
SSD_BC = SSD_GROUPS * SSD_STATE
SSD_CONV_DIM = SSD_INNER + 2 * SSD_BC
SSD_GROUP_WIDTH = SSD_INNER // SSD_GROUPS
SSD_HEADS_PER_GROUP = SSD_HEADS // SSD_GROUPS
S5_WIDTH = 1024
S5_GROUPS = 64
S5_GROUP = 16
S5_STATE = 64
LIN_HEADS = 4
HGRN_WIDTH = 512
GLA_DK = 64
GLA_QK = LIN_HEADS * GLA_DK
GLA_WIDTH = 512
GLA_RANK = 16
GLA_TAU = 16.0
HEAD_W = 128

LANES = 128
SUBLANES = 8
VMEM_LIMIT = 58 * 1024 * 1024

TM = 512
TM_FFN = 1024
SUB_FFN = 512
FF_CHUNK = 256
SSD_CHUNK = 128
MIX_STEP_CHUNKS = 4
LIN_CHUNK = 128
S5_Q = 32
S5_ROW = S5_Q * S5_GROUP
S5_TILE_GROUPS = LANES // S5_GROUP
S5_PITCH = 40

AB_XBC, AB_Z, AB_U, AB_DT, AB_PACK = 0, 2048, 3072, 4096, 4224
CD_H, CD_GQK, CD_GV, CD_GR, CD_GLR, CD_PACK = 0, 2048, 2560, 3072, 3584, 3712


def _resident(shape):
    n = len(shape)
    return pl.BlockSpec(shape, lambda *_: (0,) * n, pipeline_mode=pl.Buffered(1))


def _dot(a, b):
    return jnp.dot(a, b, preferred_element_type=F32)


def _dot_nt(a, b):
    return lax.dot_general(a, b, (((1,), (1,)), ((), ())), preferred_element_type=F32)


def _dot_tn(a, b):
    return lax.dot_general(a, b, (((0,), (0,)), ((), ())), preferred_element_type=F32)


def _split3(v):
    hi = v.astype(BF16)
    r = v - hi.astype(F32)
    mid = r.astype(BF16)
    lo = (r - mid.astype(F32)).astype(BF16)
    return hi, mid, lo


def _sel_dot(sel, v):
    hi, mid, lo = _split3(v)
    return _dot(sel, hi) + _dot(sel, mid) + _dot(sel, lo)


def _dot_sel(v, sel):
    hi, mid, lo = _split3(v)
    return _dot(hi, sel) + _dot(mid, sel) + _dot(lo, sel)


def _sigmoid(x):
    return 1.0 / (1.0 + jnp.exp(-x))


def _silu(x):
    return x * _sigmoid(x)


def _log_sigmoid(x):
    return jnp.minimum(x, 0.0) - jnp.log1p(jnp.exp(-jnp.abs(x)))


def _softplus(x):
    return jnp.maximum(x, 0.0) + jnp.log1p(jnp.exp(-jnp.abs(x)))


def _gelu_tanh(x):
    return 0.5 * x * (1.0 + jnp.tanh(math.sqrt(2.0 / math.pi) * (x + 0.044715 * (x * x * x))))


def _layer_norm(y, g, b):
    mu = jnp.mean(y, axis=-1, keepdims=True)
    yc = y - mu
    var = jnp.mean(yc * yc, axis=-1, keepdims=True)
    return yc * lax.rsqrt(var + LN_EPS) * g + b


def _rms_norm(y, w):
    return y * lax.rsqrt(jnp.mean(y * y, axis=-1, keepdims=True) + LN_EPS) * w


def _params(*sem):
    return pltpu.CompilerParams(dimension_semantics=sem, vmem_limit_bytes=VMEM_LIMIT)


def _ffn_body(mix, with_ple, x_ref, *refs):
    refs = list(refs)
    take = lambda n: [refs.pop(0) for _ in range(n)]
    if mix == "ab":
        ya_ref, ys_ref, wout_ref, wglu_ref, bglu_ref, g1_ref, b1_ref = take(7)
    elif mix == "cd":
        oc_ref, wout_ref, g1_ref, b1_ref = take(4)
    wg_ref, wu_ref, wd_ref, g_ref, b_ref = take(5)
    if with_ple:
        p_ref, pg_ref, pp_ref = take(3)
    o_ref, h_ref = refs
    for r in range(x_ref.shape[0] // SUB_FFN):
        rows = slice(r * SUB_FFN, (r + 1) * SUB_FFN)
        x = x_ref[rows, :]
        if mix == "ab":
            y = _gelu_tanh(ys_ref[rows, :].astype(F32))
            gate = _sigmoid(_dot(y.astype(BF16), wglu_ref[...]) + bglu_ref[...])
            m = _dot(ya_ref[rows, :], wout_ref[0:SSD_INNER, :]) + _dot((y * gate).astype(BF16), wout_ref[SSD_INNER:, :])
            x = _layer_norm(DN_ALPHA * x + m, g1_ref[...], b1_ref[...])
        elif mix == "cd":
            x = _layer_norm(DN_ALPHA * x + _dot(oc_ref[rows, :], wout_ref[...]), g1_ref[...], b1_ref[...])
        xb = x.astype(BF16)
        for c in range(D_FF // FF_CHUNK):
            sl = slice(c * FF_CHUNK, (c + 1) * FF_CHUNK)
            gate = _dot(xb, wg_ref[:, sl])
            up = _dot(xb, wu_ref[:, sl])
            h_ref[:, sl] = (_silu(gate) * up).astype(BF16)
        y = _layer_norm(DN_ALPHA * x + 0.5 * _dot(h_ref[...], wd_ref[...]), g_ref[...], b_ref[...])
        if with_ple:
            gate = _sigmoid(_dot(y.astype(BF16), pg_ref[...]))
            y = y + gate * _dot(p_ref[rows, :].astype(BF16), pp_ref[...])
        o_ref[rows, :] = y


def _pick(a, *idx):
    rest = a.shape[len(idx):]
    return pl.BlockSpec((None,) * len(idx) + rest, lambda *_: idx + (0,) * len(rest),
                        pipeline_mode=pl.Buffered(1))


def _ffn_ln(x, wg, wu, wd, ln_g, ln_b, layer, pos, ln_idx, mix=None, ple=None):
    t = x.shape[0]
    row = lambda i: (i, 0)
    in_specs = [pl.BlockSpec((TM_FFN, D_MODEL), row)]
    args = [x]
    if mix is not None:
        kind, acts, consts = mix
        in_specs += [pl.BlockSpec((TM_FFN, a.shape[1]), row) for a in acts]
        in_specs += [_pick(a, *idx) for a, idx in consts]
        in_specs += [_pick(ln_g, ln_idx - 1), _pick(ln_b, ln_idx - 1)]
        args += list(acts) + [a for a, _ in consts] + [ln_g, ln_b]
    in_specs += [_pick(wg, layer, pos), _pick(wu, layer, pos), _pick(wd, layer, pos),
                 _pick(ln_g, ln_idx), _pick(ln_b, ln_idx)]
    args += [wg, wu, wd, ln_g, ln_b]
    if ple is not None:
        p, pg, pp = ple
        in_specs += [pl.BlockSpec((None, TM_FFN, PLE_DIM), lambda i: (layer, i, 0)), _pick(pg, layer), _pick(pp, layer)]
        args += [p, pg, pp]
    return pl.pallas_call(
        functools.partial(_ffn_body, None if mix is None else mix[0], ple is not None),
        grid=(t // TM_FFN,),
        in_specs=in_specs,
        out_specs=pl.BlockSpec((TM_FFN, D_MODEL), row),
        out_shape=jax.ShapeDtypeStruct((t, D_MODEL), F32),
        scratch_shapes=[pltpu.VMEM((SUB_FFN, D_FF), BF16)],
        compiler_params=_params("parallel"),
        name="ffn_ln" if mix is None else "mix_ffn_ln_ple",
    )(*args)


def _proj_in_body(fields, x_ref, *refs):
    w_refs, o_ref = refs[:-1], refs[-1]
    xb = x_ref[...].astype(BF16)
    step = 512
    for k, src, n, dst in fields:
        for s0 in range(0, n, step):
            s1 = min(s0 + step, n)
            o_ref[:, dst + s0:dst + s1] = _dot(xb, w_refs[k][:, src + s0:src + s1].astype(BF16))


def _proj_in(x, w_stack, layer, extras, fields, n_out):
    t = x.shape[0]
    spans = sorted((dst, dst + n) for _, _, n, dst in fields)
    assert spans[0][0] == 0 and spans[-1][1] == n_out and all(a[1] == b[0] for a, b in zip(spans, spans[1:]))
    assert all(src % LANES == 0 and dst % LANES == 0 for _, src, _, dst in fields)
    return pl.pallas_call(
        functools.partial(_proj_in_body, tuple(fields)),
        grid=(t // TM,),
        in_specs=[pl.BlockSpec((TM, D_MODEL), lambda i: (i, 0)), _pick(w_stack, layer)]
        + [_resident(w.shape) for w in extras],
        out_specs=pl.BlockSpec((TM, n_out), lambda i: (i, 0)),
        out_shape=jax.ShapeDtypeStruct((t, n_out), F32),
        compiler_params=_params("parallel"),
        name="proj_in",
    )(x, w_stack, *extras)


def _ssd_body(xbc_ref, z_ref, dt_ref, convw_ref, convb_ref, dtb_ref, alog_ref, dskip_ref,
              normw_ref, tril_ref, expand_ref, shift_ref, o_ref, xb_ref, state_ref):
    ch = SSD_CHUNK
    gw = SSD_GROUP_WIDTH

    @pl.when(pl.program_id(1) == 0)
    def _():
        xb_ref[0:ch, :] = jnp.zeros((ch, SSD_CONV_DIM), BF16)
        state_ref[...] = jnp.zeros(state_ref.shape, F32)

    def chunk(i, carry):
        rows = pl.ds(pl.multiple_of(i * ch, ch), ch)

        def conv_silu(cols):
            cur = xbc_ref[rows, cols]
            cur_b = cur.astype(BF16)
            xb_ref[ch:2 * ch, cols] = cur_b
            both = xb_ref[:, cols]
            acc = cur * convw_ref[SSD_CONV - 1:SSD_CONV, cols] + convb_ref[:, cols]
            for j in range(1, SSD_CONV):
                acc = acc + _dot(shift_ref[j - 1], both) * convw_ref[SSD_CONV - 1 - j:SSD_CONV - j, cols]
            xb_ref[0:ch, cols] = cur_b
            return _silu(acc)

        dt = _softplus(dt_ref[rows, :] + dtb_ref[...])
        da = dt * (-jnp.exp(alog_ref[...]))
        a_cum = _sel_dot(tril_ref[...], da)
        a_cum_t = a_cum.T
        dt_hi, dt_mid, _ = _split3(dt)
        dt_e = _dot(dt_hi, expand_ref[...]) + _dot(dt_mid, expand_ref[...])
        ac_e = _dot_sel(a_cum, expand_ref[...])
        ac_last = ac_e[ch - 1:ch, :]
        to_end = jnp.exp(ac_last - ac_e)
        carry_scale = jnp.exp(ac_e)
        chunk_decay = jnp.exp(ac_last)

        row = lax.broadcasted_iota(jnp.int32, (ch, ch), 0)
        col = lax.broadcasted_iota(jnp.int32, (ch, ch), 1)
        causal = col <= row
        lane_head = lax.broadcasted_iota(jnp.int32, (ch, gw), 1) >> int(math.log2(SSD_HEAD_DIM))

        groups = range(SSD_GROUPS)
        cols = [slice(g * gw, (g + 1) * gw) for g in groups]
        xs = [conv_silu(cols[g]) for g in groups]
        bc = [conv_silu(slice(SSD_INNER + k * gw, SSD_INNER + (k + 1) * gw)).astype(BF16)
              for k in range(2 * SSD_BC // gw)]
        bc = [v[:, half * SSD_STATE:(half + 1) * SSD_STATE] for v in bc for half in range(gw // SSD_STATE)]
        bs, cs = bc[:SSD_GROUPS], bc[SSD_GROUPS:]
        xdt = [xs[g] * dt_e[:, cols[g]] for g in groups]
        xdt_b = [v.astype(BF16) for v in xdt]
        scores = [_dot_nt(cs[g], bs[g]).astype(BF16) for g in groups]
        ys = []
        for g in groups:
            y_g = jnp.zeros((ch, gw), F32)
            for hh in range(SSD_HEADS_PER_GROUP):
                h = g * SSD_HEADS_PER_GROUP + hh
                seg = a_cum[:, h:h + 1] - a_cum_t[h:h + 1, :]
                decay = jnp.exp(jnp.where(causal, seg, NEG_BIG)).astype(BF16)
                full = _dot(scores[g] * decay, xdt_b[g])
                y_g = jnp.where(lane_head == hh, full, y_g)
            ys.append(y_g)
        for g in groups:
            gs = cols[g]
            state = state_ref[g]
            y_g = ys[g] + _dot(cs[g], state.astype(BF16)) * carry_scale[:, gs]
            state_ref[g] = state * chunk_decay[:, gs] + _dot_tn(bs[g], (xdt[g] * to_end[:, gs]).astype(BF16))
            y_g = y_g + xs[g] * dskip_ref[:, gs]
            y_g = y_g * _silu(z_ref[rows, gs])
            o_ref[rows, gs] = _rms_norm(y_g, normw_ref[:, gs]).astype(o_ref.dtype)
        return carry

    lax.fori_loop(0, xbc_ref.shape[0] // ch, chunk, 0)


def _ssd_mixer(h_ab, batch, seq, conv_w, conv_b, dt_bias, a_log, d_skip, norm_w):
    ch = SSD_CHUNK
    step = ch * MIX_STEP_CHUNKS
    nc = seq // step
    tok = lambda cb: (lambda b, c: (b * nc + c, cb))
    pad_heads = lambda v: jnp.zeros((1, LANES), F32).at[0, :SSD_HEADS].set(v.astype(F32))
    tril = jnp.asarray(np.tril(np.ones((ch, ch), np.float32)), BF16)
    expand = np.zeros((LANES, SSD_INNER), np.float32)
    for h in range(SSD_HEADS):
        expand[h, h * SSD_HEAD_DIM:(h + 1) * SSD_HEAD_DIM] = 1.0
    expand = jnp.asarray(expand, BF16)
    shift = np.zeros((SSD_CONV - 1, ch, 2 * ch), np.float32)
    for j in range(1, SSD_CONV):
        shift[j - 1, np.arange(ch), ch + np.arange(ch) - j] = 1.0
    shift = jnp.asarray(shift, BF16)
    dskip_e = jnp.repeat(d_skip.astype(F32), SSD_HEAD_DIM)[None, :]
    consts = [conv_w, conv_b[None, :], pad_heads(dt_bias), pad_heads(a_log), dskip_e,
              norm_w[None, :], tril, expand, shift]
    in_specs = [
        pl.BlockSpec((step, SSD_CONV_DIM), tok(AB_XBC // SSD_CONV_DIM)),
        pl.BlockSpec((step, SSD_INNER), tok(AB_Z // SSD_INNER)),
        pl.BlockSpec((step, LANES), tok(AB_DT // LANES)),
    ] + [_resident(a.shape) for a in consts]
    return pl.pallas_call(
        _ssd_body,
        grid=(batch, nc),
        in_specs=in_specs,
        out_specs=pl.BlockSpec((step, SSD_INNER), tok(0)),
        out_shape=jax.ShapeDtypeStruct((batch * seq, SSD_INNER), BF16),
        scratch_shapes=[pltpu.VMEM((2 * ch, SSD_CONV_DIM), BF16),
                        pltpu.VMEM((SSD_GROUPS, SSD_STATE, SSD_GROUP_WIDTH), F32)],
        compiler_params=_params("parallel", "arbitrary"),
        name="ssd_mixer",
    )(h_ab, h_ab, h_ab, *consts)


def _s5_slot(g, s):
    return (s + g) % S5_TILE_GROUPS


def _s5_body(nblk, u_ref, d_ref, m_ref, wst_ref, wofft_ref, a1_ref, a2_ref, o_ref, uscr_ref, yscr_ref, ug_ref,
             yg_ref):
    ng = S5_TILE_GROUPS
    ncol = S5_Q // ng
    nstrip = nblk // SUBLANES
    slot = lax.broadcasted_iota(jnp.int32, (SUBLANES, LANES), 1) >> int(math.log2(S5_GROUP))
    blk = lax.broadcasted_iota(jnp.int32, (nblk, 2 * S5_STATE), 0)

    def pitch_in(c, carry):
        src = pl.multiple_of(c * S5_Q, S5_Q)
        dst = pl.multiple_of(c * S5_PITCH, SUBLANES)
        uscr_ref[pl.ds(dst, S5_Q), :] = u_ref[pl.ds(src, S5_Q), :]
        return carry
    lax.fori_loop(0, nblk, pitch_in, 0, unroll=8)

    def gather(i, carry):
        base = pl.multiple_of(i * (SUBLANES * S5_PITCH), SUBLANES)
        row = pl.multiple_of(i * SUBLANES, SUBLANES)
        for m in range(ncol):
            rolled = []
            for s in range(ng):
                us = uscr_ref[pl.ds(base + ng * m + s, SUBLANES, stride=S5_PITCH), :]
                rolled.append(pltpu.roll(us, S5_GROUP * s, axis=1) if s else us)
            for g in range(ng):
                v = rolled[0]
                for s in range(1, ng):
                    v = jnp.where(slot == _s5_slot(g, s), rolled[s], v)
                ug_ref[pl.ds(row, SUBLANES), (g * ncol + m) * LANES:(g * ncol + m + 1) * LANES] = v
        return carry
    lax.fori_loop(0, nstrip, gather, 0, unroll=2)

    us = [ug_ref[:, g * S5_ROW:(g + 1) * S5_ROW].astype(BF16) for g in range(ng)]
    xs = [_dot(us[g], wst_ref[g]) for g in range(ng)]
    for k in range(int(math.log2(nblk))):
        sh = 1 << k
        for g in range(ng):
            prev = jnp.where(blk >= sh, pltpu.roll(xs[g], sh, axis=0), 0.0)
            xs[g] = (xs[g] + prev * a1_ref[g, k:k + 1, :]
                     + pltpu.roll(prev, S5_STATE, axis=1) * a2_ref[g, k:k + 1, :])
    for g in range(ng):
        x_in = jnp.where(blk >= 1, pltpu.roll(xs[g], 1, axis=0), 0.0)
        yg_ref[:, g * S5_ROW:(g + 1) * S5_ROW] = (_dot(us[g], m_ref[g])
                                                   + _dot_nt(x_in.astype(BF16), wofft_ref[g]))

    def scatter(i, carry):
        base = pl.multiple_of(i * (SUBLANES * S5_PITCH), SUBLANES)
        row = pl.multiple_of(i * SUBLANES, SUBLANES)
        for m in range(ncol):
            cols = [yg_ref[pl.ds(row, SUBLANES), (g * ncol + m) * LANES:(g * ncol + m + 1) * LANES]
                    for g in range(ng)]
            for tt in range(ng):
                z = cols[0]
                for g in range(1, ng):
                    z = jnp.where(slot == _s5_slot(g, tt), cols[g], z)
                sh = (LANES - S5_GROUP * tt) % LANES
                yscr_ref[pl.ds(base + ng * m + tt, SUBLANES, stride=S5_PITCH), :] = (
                    pltpu.roll(z, sh, axis=1) if sh else z)
        return carry
    lax.fori_loop(0, nstrip, scatter, 0, unroll=2)

    def pitch_out(c, carry):
        src = pl.multiple_of(c * S5_PITCH, SUBLANES)
        dst = pl.multiple_of(c * S5_Q, S5_Q)
        y = yscr_ref[pl.ds(src, S5_Q), :] + d_ref[...] * u_ref[pl.ds(dst, S5_Q), :]
        o_ref[pl.ds(dst, S5_Q), :] = y.astype(o_ref.dtype)
        return carry
    lax.fori_loop(0, nblk, pitch_out, 0, unroll=8)


def _s5_core(h_ab, d_skip, batch, seq, tables):
    nblk = seq // S5_Q
    ng = S5_TILE_GROUPS
    grp = lambda a: pl.BlockSpec((ng,) + a.shape[1:], lambda j, b: (j, 0, 0))
    return pl.pallas_call(
        functools.partial(_s5_body, nblk),
        grid=(S5_GROUPS // ng, batch),
        in_specs=[pl.BlockSpec((seq, LANES), lambda j, b: (b, AB_U // LANES + j)),
                  pl.BlockSpec((1, LANES), lambda j, b: (0, j))]
        + [grp(a) for a in tables],
        out_specs=pl.BlockSpec((seq, LANES), lambda j, b: (b, j)),
        out_shape=jax.ShapeDtypeStruct((batch * seq, S5_WIDTH), BF16),
        scratch_shapes=[pltpu.VMEM((nblk * S5_PITCH, LANES), F32), pltpu.VMEM((nblk * S5_PITCH, LANES), F32),
                        pltpu.VMEM((nblk, ng * S5_ROW), F32), pltpu.VMEM((nblk, ng * S5_ROW), F32)],
        compiler_params=_params("parallel", "parallel"),
        name="s5_core",
    )(h_ab, d_skip, *tables)


def _s5_table_body(kt_ref, pw_rr_ref, pw_ii_ref, b1_ref, b2_ref, c1_ref, c2_ref, m_ref, wst_ref, wofft_ref):
    ng = S5_TILE_GROUPS
    ncol = S5_Q // ng
    lane = lax.broadcasted_iota(jnp.int32, (S5_GROUP, LANES), 1)
    for g in range(ng):
        strip = [jnp.zeros((S5_GROUP, LANES), F32)] * ncol + [kt_ref[g, :, c * LANES:(c + 1) * LANES]
                                                            for c in range(ncol)]
        b1, b2, c1, c2 = b1_ref[g], b2_ref[g], c1_ref[g], c2_ref[g]
        for s in range(S5_Q):
            start = ncol * LANES - S5_GROUP * s
            a, sh = start // LANES, start % LANES
            k = ng * (s // ng) + _s5_slot(g, s % ng)
            rows = slice(S5_GROUP * k, S5_GROUP * (k + 1))
            for c in range(ncol):
                if sh:
                    w = jnp.where(lane < LANES - sh, pltpu.roll(strip[a + c], LANES - sh, axis=1),
                                  pltpu.roll(strip[a + c + 1], LANES - sh, axis=1))
                else:
                    w = strip[a + c]
                if g:
                    w = pltpu.roll(w, S5_GROUP * g, axis=1)
                m_ref[g, rows, c * LANES:(c + 1) * LANES] = w.astype(BF16)
            e = S5_Q - 1 - s
            wst_ref[g, rows, :] = (pw_rr_ref[g, e:e + 1, :] * b1 + pw_ii_ref[g, e:e + 1, :] * b2).astype(BF16)
            wofft_ref[g, rows, :] = (pw_rr_ref[g, s + 1:s + 2, :] * c1
                                     + pw_ii_ref[g, s + 1:s + 2, :] * c2).astype(BF16)


def _s5_tables(lam_re, lam_im, log_dt, b_re, b_im, c_re, c_im, nblk):
    q, ng = S5_Q, S5_TILE_GROUPS
    hp = lax.Precision.HIGHEST
    lam = lax.complex(lam_re.astype(F32), lam_im.astype(F32))
    ldt = lam * jnp.exp(log_dt.astype(F32))[:, None]
    lam_bar = jnp.exp(ldt)
    b_bar = ((lam_bar - 1.0) / lam)[..., None] * lax.complex(b_re.astype(F32), b_im.astype(F32))
    c = lax.complex(c_re.astype(F32), c_im.astype(F32))
    tau = jnp.arange(q + 1, dtype=F32)
    pw = jnp.exp(ldt[:, None, :] * tau[None, :, None])
    kt = jnp.real(jnp.einsum('gpn,gtn,gnq->gqtp', c, pw[:, :q], b_bar, precision=hp)).reshape(
        S5_GROUPS, S5_GROUP, S5_ROW)
    halves = lambda lo, hi: jnp.concatenate([lo, hi], axis=-1)
    bt = jnp.transpose(b_bar, (0, 2, 1))
    small = [halves(jnp.real(pw), jnp.real(pw)), halves(jnp.imag(pw), jnp.imag(pw)),
             halves(jnp.real(bt), jnp.imag(bt)), halves(-jnp.imag(bt), jnp.real(bt)),
             halves(jnp.real(c), -jnp.imag(c)), halves(-jnp.imag(c), -jnp.real(c))]
    grp = lambda shape: pl.BlockSpec((ng,) + shape, lambda j: (j, 0, 0))
    shapes = [(S5_ROW, S5_ROW), (S5_ROW, 2 * S5_STATE), (S5_ROW, 2 * S5_STATE)]
    m, wst, wofft = pl.pallas_call(
        _s5_table_body,
        grid=(S5_GROUPS // ng,),
        in_specs=[grp(kt.shape[1:])] + [grp(a.shape[1:]) for a in small],
        out_specs=[grp(sh) for sh in shapes],
        out_shape=[jax.ShapeDtypeStruct((S5_GROUPS,) + sh, BF16) for sh in shapes],
        compiler_params=_params("parallel"),
        name="s5_tables",
    )(kt, *small)
    nlev = int(math.log2(nblk))
    step = jnp.exp(ldt[:, None, :] * (q * 2.0 ** jnp.arange(nlev, dtype=F32))[None, :, None])
    a1 = halves(jnp.real(step), jnp.real(step))
    a2 = halves(-jnp.imag(step), jnp.imag(step))
    return m, wst, wofft, a1, a2


def _lin_levels(c):
    return [c >> (i + 1) for i in range(int(math.log2(c)))]


def _lin_tables(c):
    t = np.arange(c)[:, None]
    j = np.arange(c)[None, :]
    lvl = np.full((c, c), -1, np.int32)
    lvl[np.arange(c), np.arange(c)] = 0
    sgn = []
    for i, b in enumerate(_lin_levels(c)):
        same = (t // (2 * b)) == (j // (2 * b))
        second = (t % (2 * b)) >= b
        lvl[same & second & ((j % (2 * b)) < b)] = i + 1
        sgn.append(np.broadcast_to(np.where(second, 1.0, -1.0) * math.log2(math.e), (c, LANES)))
    return (jnp.asarray((j <= t).astype(np.float32), BF16), jnp.asarray(lvl),
            jnp.asarray(np.stack(sgn), F32))


def _gate_factors(g, tril, sgn_ref, gcum_ref):
    c = LIN_CHUNK
    gcum = _sel_dot(tril, g)
    gcum_ref[...] = gcum
    pos = lax.broadcasted_iota(jnp.int32, (c, LANES), 0)
    factors = []
    for i, b in enumerate(_lin_levels(c)):
        if 2 * b >= SUBLANES:
            mids = [jnp.broadcast_to(gcum_ref[blk * 2 * b + b - 1:blk * 2 * b + b, :], (2 * b, LANES))
                    for blk in range(c // (2 * b))]
            gmid = mids[0] if len(mids) == 1 else jnp.concatenate(mids, axis=0)
        elif b == 2:
            p4 = pos & 3
            gmid = jnp.where(p4 == 0, pltpu.roll(gcum, c - 1, axis=0),
                             jnp.where(p4 == 1, gcum,
                                       jnp.where(p4 == 2, pltpu.roll(gcum, 1, axis=0),
                                                 pltpu.roll(gcum, 2, axis=0))))
        else:
            gmid = jnp.where((pos & 1) == 1, pltpu.roll(gcum, 1, axis=0), gcum)
        factors.append(jnp.exp2((gcum - gmid) * sgn_ref[i]).astype(BF16))
    e_cum = jnp.exp(gcum)
    return e_cum, e_cum.astype(BF16), jnp.exp(gcum[c - 1:c, :] - gcum).astype(BF16), factors


def _lin_body(h_ref, gqk_ref, gv_ref, gr_ref, glr_ref, loglb_ref, log1mlb_ref, onemlb_ref, hnorm_ref,
              wup_ref, bgate_ref, gnorm_ref, tril_ref, lvl_ref, sgn_ref, o_ref, state_ref, gcum_ref):
    c = LIN_CHUNK

    @pl.when(pl.program_id(1) == 0)
    def _():
        state_ref[...] = jnp.zeros(state_ref.shape, F32)

    def chunk(ci, carry):
        rows = pl.ds(pl.multiple_of(ci * c, c), c)

        hq = h_ref[rows, 0:HGRN_WIDTH]
        hf = h_ref[rows, HGRN_WIDTH:2 * HGRN_WIDTH]
        lo = loglb_ref[...]
        hi_ = log1mlb_ref[...] + _log_sigmoid(hf)
        log_f = jnp.maximum(lo, hi_) + jnp.log1p(jnp.exp(-jnp.abs(lo - hi_)))
        q_c = _silu(hq)
        k_c = onemlb_ref[...] * _sigmoid(-hf)
        pre = _dot(glr_ref[rows, :].astype(BF16), wup_ref[...]) + bgate_ref[...]
        log_a = _log_sigmoid(pre) * (1.0 / GLA_TAU)
        q_d = gqk_ref[rows, 0:GLA_QK] * (GLA_DK ** -0.5)
        k_d = gqk_ref[rows, GLA_QK:2 * GLA_QK]

        lvl = lvl_ref[...]
        lane = lax.broadcasted_iota(jnp.int32, (c, LANES), 1)
        tril = tril_ref[...]

        heads = []
        for hd in range(LIN_HEADS):
            ls = slice(hd * HEAD_W, (hd + 1) * HEAD_W)
            heads.append((hd, q_c[:, ls], k_c[:, ls], hd,
                          h_ref[rows, 2 * HGRN_WIDTH + hd * HEAD_W:2 * HGRN_WIDTH + (hd + 1) * HEAD_W],
                          hnorm_ref[:, ls],
                          h_ref[rows, 3 * HGRN_WIDTH + hd * HEAD_W:3 * HGRN_WIDTH + (hd + 1) * HEAD_W], ls))
        for hd in range(LIN_HEADS):
            tile, half = divmod(hd, LANES // GLA_DK)
            ts = slice(tile * LANES, (tile + 1) * LANES)
            ls = slice(hd * HEAD_W, (hd + 1) * HEAD_W)
            mine = (lane >> int(math.log2(GLA_DK))) == half
            heads.append((LIN_HEADS + hd, jnp.where(mine, q_d[:, ts], 0.0), jnp.where(mine, k_d[:, ts], 0.0),
                          LIN_HEADS + tile, gv_ref[rows, ls], gnorm_ref[:, ls], gr_ref[rows, ls],
                          slice(HGRN_WIDTH + hd * HEAD_W, HGRN_WIDTH + (hd + 1) * HEAD_W)))

        gates = [log_f[:, t * LANES:(t + 1) * LANES] for t in range(LIN_HEADS)]
        gates += [log_a[:, t * LANES:(t + 1) * LANES] for t in range(GLA_QK // LANES)]
        factors = [_gate_factors(g, tril, sgn_ref, gcum_ref.at[t]) for t, g in enumerate(gates)]
        qkb = [(q.astype(BF16), k.astype(BF16)) for _, q, k, *_ in heads]
        attns = [None] * len(heads)
        for i in range(1 + len(_lin_levels(c))):
            mask = lvl == i
            for n, ((qb, kb), head) in enumerate(zip(qkb, heads)):
                if i == 0:
                    attns[n] = jnp.where(mask, _dot_nt(qb, kb), 0.0)
                else:
                    e = factors[head[3]][3][i - 1]
                    attns[n] = jnp.where(mask, _dot_nt(qb * e, kb * e), attns[n])
        attns = [a.astype(BF16) for a in attns]
        for (qb, kb), attn, (idx, _, _, tile, v, norm_w, gate, cols) in zip(qkb, attns, heads):
            e_cum, e_cum_b, e_end_b, _ = factors[tile]
            vb = v.astype(BF16)
            state_t = state_ref[idx]
            o = _dot(attn, vb) + _dot_nt(qb * e_cum_b, state_t.astype(BF16))
            state_ref[idx] = state_t * e_cum[c - 1:c, :] + _dot_tn(vb, kb * e_end_b)
            o_ref[rows, cols] = (_rms_norm(o, norm_w) * _silu(gate)).astype(o_ref.dtype)
        return carry

    lax.fori_loop(0, h_ref.shape[0] // c, chunk, 0)


def _lin_mixer(h_cd, batch, seq, lb, hgrn_norm_w, gla_w_gate_up, gla_b_gate, gla_norm_w):
    c = LIN_CHUNK
    step = c * MIX_STEP_CHUNKS
    nc = seq // step
    tok = lambda cb: (lambda b, i: (b * nc + i, cb))
    lb = lb.astype(F32)[None, :]
    wup = jnp.zeros((LANES, GLA_QK), F32).at[:GLA_RANK].set(gla_w_gate_up).astype(BF16)
    tril, lvl, sgn = _lin_tables(c)
    consts = [jnp.log(lb), jnp.log1p(-lb), 1.0 - lb, hgrn_norm_w[None, :], wup, gla_b_gate[None, :],
              gla_norm_w[None, :], tril, lvl, sgn]
    in_specs = [
        pl.BlockSpec((step, 4 * HGRN_WIDTH), tok(CD_H // (4 * HGRN_WIDTH))),
        pl.BlockSpec((step, 2 * GLA_QK), tok(CD_GQK // (2 * GLA_QK))),
        pl.BlockSpec((step, GLA_WIDTH), tok(CD_GV // GLA_WIDTH)),
        pl.BlockSpec((step, GLA_WIDTH), tok(CD_GR // GLA_WIDTH)),
        pl.BlockSpec((step, LANES), tok(CD_GLR // LANES)),
    ] + [_resident(a.shape) for a in consts]
    return pl.pallas_call(
        _lin_body,
        grid=(batch, nc),
        in_specs=in_specs,
        out_specs=pl.BlockSpec((step, HGRN_WIDTH + GLA_WIDTH), tok(0)),
        out_shape=jax.ShapeDtypeStruct((batch * seq, HGRN_WIDTH + GLA_WIDTH), BF16),
        scratch_shapes=[pltpu.VMEM((2 * LIN_HEADS, HEAD_W, HEAD_W), F32),
                        pltpu.VMEM((LIN_HEADS + GLA_QK // LANES, c, LANES), F32)],
        compiler_params=_params("parallel", "arbitrary"),
        name="lin_mixer",
    )(h_cd, h_cd, h_cd, h_cd, h_cd, *consts)


def _cols(w, start, stop, width=None):
    part = w[:, start:stop]
    if width is not None and width > stop - start:
        part = jnp.pad(part, ((0, 0), (0, width - (stop - start))))
    return part


def _hgrn_lower_bound(lb_logits, layer):
    cum = jnp.cumsum(jax.nn.softmax(lb_logits.astype(F32), axis=0), axis=0)
    return cum[layer] - cum[0]


def kernel(x, p, ln_g, ln_b, ffn_w_gate, ffn_w_up, ffn_w_down, ple_w_gate, ple_w_proj, ab_w_in, ab_w_out,
           ssd_conv_w, ssd_conv_b, ssd_dt_bias, ssd_a_log, ssd_d, ssd_norm_w, s5_lambda_re, s5_lambda_im,
           s5_log_dt, s5_b_re, s5_b_im, s5_c_re, s5_c_im, s5_d, s5_w_glu, s5_b_glu, cd_w_in, cd_w_out,
           hgrn_lb_logits, hgrn_norm_w, gla_w_gate_up, gla_b_gate, gla_norm_w):
    batch, seq, _ = x.shape
    t = batch * seq
    x = x.reshape(t, D_MODEL)
    bf = lambda w: w.astype(BF16)
    ab_splits = (SSD_INNER, SSD_CONV_DIM, SSD_HEADS, S5_WIDTH)
    cd_splits = (HGRN_WIDTH, HGRN_WIDTH, HGRN_WIDTH, HGRN_WIDTH, GLA_QK, GLA_QK, GLA_WIDTH, GLA_RANK,
                 GLA_WIDTH)
    wg, wu, wd = bf(ffn_w_gate), bf(ffn_w_up), bf(ffn_w_down)
    pg, pp = bf(ple_w_gate), bf(ple_w_proj)
    p = p.reshape(DEPTH, t, PLE_DIM)
    ln_g = ln_g.reshape(DEPTH * 3, 1, D_MODEL)
    ln_b = ln_b.reshape(DEPTH * 3, 1, D_MODEL)
    w_ab_out, w_cd_out = bf(ab_w_out), bf(cd_w_out)
    w_glu, b_glu = bf(s5_w_glu), s5_b_glu[:, None, :]
    for i in range(DEPTH):
        j = i // 2
        x = _ffn_ln(x, wg, wu, wd, ln_g, ln_b, i, 0, 3 * i)
        if i % 2 == 0:
            w, e = ab_w_in[j], np.cumsum((0,) + ab_splits)
            h_ab = _proj_in(x, ab_w_in, j, [_cols(w, e[3], e[4]), _cols(w, e[2], e[3], AB_PACK - AB_DT)],
                            [(0, e[1], e[2] - e[1], AB_XBC), (0, e[0], e[1] - e[0], AB_Z),
                             (1, 0, e[4] - e[3], AB_U), (2, 0, AB_PACK - AB_DT, AB_DT)], AB_PACK)
            y_a = _ssd_mixer(h_ab, batch, seq, ssd_conv_w[j], ssd_conv_b[j], ssd_dt_bias[j], ssd_a_log[j],
                             ssd_d[j], ssd_norm_w[j])
            tables = _s5_tables(s5_lambda_re[j], s5_lambda_im[j], s5_log_dt[j], s5_b_re[j], s5_b_im[j],
                                s5_c_re[j], s5_c_im[j], seq // S5_Q)
            y_s = _s5_core(h_ab, s5_d[j].reshape(1, S5_WIDTH).astype(F32), batch, seq, tables)
            mix = ("ab", (y_a, y_s), [(w_ab_out, (j,)), (w_glu, (j,)), (b_glu, (j,))])
        else:
            w, e = cd_w_in[j], np.cumsum((0,) + cd_splits)
            h_cd = _proj_in(x, cd_w_in, j, [_cols(w, e[8], e[9]), _cols(w, e[7], e[8], CD_PACK - CD_GLR)],
                            [(0, e[0], e[7] - e[0], CD_H), (1, 0, e[9] - e[8], CD_GR),
                             (2, 0, CD_PACK - CD_GLR, CD_GLR)], CD_PACK)
            o_cd = _lin_mixer(h_cd, batch, seq, _hgrn_lower_bound(hgrn_lb_logits, i), hgrn_norm_w[j],
                              gla_w_gate_up[j], gla_b_gate[j], gla_norm_w[j])
            mix = ("cd", (o_cd,), [(w_cd_out, (j,))])
        x = _ffn_ln(x, wg, wu, wd, ln_g, ln_b, i, 1, 3 * i + 2, mix=mix, ple=(p, pg, pp))
    return x.reshape(batch, seq, D_MODEL)
```

```python
import functools
import math

import jax
import jax.numpy as jnp
import numpy as np
from jax import lax
from jax.experimental import pallas as pl
from jax.experimental.pallas import tpu as pltpu

F32 = jnp.float32
BF16 = jnp.bfloat16

D_MODEL = 1024
D_FF = 2816
PLE_DIM = 256
DEPTH = 2
DN_ALPHA = (2.0 * DEPTH) ** 0.25
LN_EPS = 1e-5
NEG_BIG = -1e30
SSD_HEADS = 16
SSD_HEAD_DIM = 64
SSD_GROUPS = 4
SSD_STATE = 128
SSD_CONV = 4
SSD_INNER = 1024
SSD_BC = SSD_GROUPS * SSD_STATE
SSD_CONV_DIM = SSD_INNER + 2 * SSD_BC
SSD_GROUP_WIDTH = SSD_INNER // SSD_GROUPS
SSD_HEADS_PER_GROUP = SSD_HEADS // SSD_GROUPS
S5_WIDTH = 1024
S5_GROUPS = 64
S5_GROUP = 16
S5_STATE = 64
LIN_HEADS = 4
HGRN_WIDTH = 512
GLA_DK = 64
GLA_QK = LIN_HEADS * GLA_DK
GLA_WIDTH = 512
GLA_RANK = 16
GLA_TAU = 16.0
HEAD_W = 128

LANES = 128
SUBLANES = 8
VMEM_LIMIT = 58 * 1024 * 1024

TM = 512
TM_FFN = 1024
SUB_FFN = 512
FF_CHUNK = 256
PROJ_COLS = 512
SSD_CHUNK = 128
MIX_STEP_CHUNKS = 4
LIN_CHUNK = 128
S5_Q = 32
S5_ROW = S5_Q * S5_GROUP
S5_TILE_GROUPS = LANES // S5_GROUP
S5_PITCH = 40

AB_XBC, AB_Z, AB_U, AB_DT, AB_PACK = 0, 2048, 3072, 4096, 4224
CD_H, CD_GQK, CD_GV, CD_GR, CD_GLR, CD_PACK = 0, 2048, 2560, 3072, 3584, 3712


def _resident(shape):
    n = len(shape)
    return pl.BlockSpec(shape, lambda *_: (0,) * n, pipeline_mode=pl.Buffered(1))


def _dot(a, b):
    return jnp.dot(a, b, preferred_element_type=F32)


def _dot_nt(a, b):
    return lax.dot_general(a, b, (((1,), (1,)), ((), ())), preferred_element_type=F32)


def _dot_tn(a, b):
    return lax.dot_general(a, b, (((0,), (0,)), ((), ())), preferred_element_type=F32)


def _split3(v):
    hi = v.astype(BF16)
    r = v - hi.astype(F32)
    mid = r.astype(BF16)
    lo = (r - mid.astype(F32)).astype(BF16)
    return hi, mid, lo


def _sel_dot(sel, v):
    hi, mid, lo = _split3(v)
    return _dot(sel, hi) + _dot(sel, mid) + _dot(sel, lo)


def _dot_sel(v, sel):
    hi, mid, lo = _split3(v)
    return _dot(hi, sel) + _dot(mid, sel) + _dot(lo, sel)


def _sigmoid(x):
    return 1.0 / (1.0 + jnp.exp(-x))


def _silu(x):
    return x * _sigmoid(x)


def _log_sigmoid(x):
    return jnp.minimum(x, 0.0) - jnp.log1p(jnp.exp(-jnp.abs(x)))


def _softplus(x):
    return jnp.maximum(x, 0.0) + jnp.log1p(jnp.exp(-jnp.abs(x)))


def _gelu_tanh(x):
    return 0.5 * x * (1.0 + jnp.tanh(math.sqrt(2.0 / math.pi) * (x + 0.044715 * (x * x * x))))


def _layer_norm(y, g, b):
    mu = jnp.mean(y, axis=-1, keepdims=True)
    yc = y - mu
    var = jnp.mean(yc * yc, axis=-1, keepdims=True)
    return yc * lax.rsqrt(var + LN_EPS) * g + b


def _rms_norm(y, w):
    return y * lax.rsqrt(jnp.mean(y * y, axis=-1, keepdims=True) + LN_EPS) * w


def _params(*sem):
    return pltpu.CompilerParams(dimension_semantics=sem, vmem_limit_bytes=VMEM_LIMIT)


def _ffn_body(mix, with_ple, x_ref, *refs):
    refs = list(refs)
    take = lambda n: [refs.pop(0) for _ in range(n)]
    if mix == "ab":
        ya_ref, ys_ref, wout_ref, wglu_ref, bglu_ref, g1_ref, b1_ref = take(7)
    elif mix == "cd":
        oc_ref, wout_ref, g1_ref, b1_ref = take(4)
    wg_ref, wu_ref, wd_ref, g_ref, b_ref = take(5)
    if with_ple:
        p_ref, pg_ref, pp_ref = take(3)
    o_ref, h_ref = refs
    for r in range(x_ref.shape[0] // SUB_FFN):
        rows = slice(r * SUB_FFN, (r + 1) * SUB_FFN)
        x = x_ref[rows, :]
        if mix == "ab":
            y = _gelu_tanh(ys_ref[rows, :].astype(F32))
            gate = _sigmoid(_dot(y.astype(BF16), wglu_ref[...]) + bglu_ref[...])
            m = _dot(ya_ref[rows, :], wout_ref[0:SSD_INNER, :]) + _dot((y * gate).astype(BF16), wout_ref[SSD_INNER:, :])
            x = _layer_norm(DN_ALPHA * x + m, g1_ref[...], b1_ref[...])
        elif mix == "cd":
            x = _layer_norm(DN_ALPHA * x + _dot(oc_ref[rows, :], wout_ref[...]), g1_ref[...], b1_ref[...])
        xb = x.astype(BF16)
        for c in range(D_FF // FF_CHUNK):
            sl = slice(c * FF_CHUNK, (c + 1) * FF_CHUNK)
            gate = _dot(xb, wg_ref[:, sl])
            up = _dot(xb, wu_ref[:, sl])
            h_ref[:, sl] = (_silu(gate) * up).astype(BF16)
        y = _layer_norm(DN_ALPHA * x + 0.5 * _dot(h_ref[...], wd_ref[...]), g_ref[...], b_ref[...])
        if with_ple:
            gate = _sigmoid(_dot(y.astype(BF16), pg_ref[...]))
            y = y + gate * _dot(p_ref[rows, :].astype(BF16), pp_ref[...])
        o_ref[rows, :] = y


def _pick(a, *idx):
    rest = a.shape[len(idx):]
    return pl.BlockSpec((None,) * len(idx) + rest, lambda *_: idx + (0,) * len(rest),
                        pipeline_mode=pl.Buffered(1))


def _ffn_ln(x, wg, wu, wd, ln_g, ln_b, layer, pos, ln_idx, mix=None, ple=None):
    t = x.shape[0]
    row = lambda i: (i, 0)
    in_specs = [pl.BlockSpec((TM_FFN, D_MODEL), row)]
    args = [x]
    if mix is not None:
        kind, acts, consts = mix
        in_specs += [pl.BlockSpec((TM_FFN, a.shape[1]), row) for a in acts]
        in_specs += [_pick(a, *idx) for a, idx in consts]
        in_specs += [_pick(ln_g, ln_idx - 1), _pick(ln_b, ln_idx - 1)]
        args += list(acts) + [a for a, _ in consts] + [ln_g, ln_b]
    in_specs += [_pick(wg, layer, pos), _pick(wu, layer, pos), _pick(wd, layer, pos),
                 _pick(ln_g, ln_idx), _pick(ln_b, ln_idx)]
    args += [wg, wu, wd, ln_g, ln_b]
    if ple is not None:
        p, pg, pp = ple
        in_specs += [pl.BlockSpec((None, TM_FFN, PLE_DIM), lambda i: (layer, i, 0)), _pick(pg, layer), _pick(pp, layer)]
        args += [p, pg, pp]
    return pl.pallas_call(
        functools.partial(_ffn_body, None if mix is None else mix[0], ple is not None),
        grid=(t // TM_FFN,),
        in_specs=in_specs,
        out_specs=pl.BlockSpec((TM_FFN, D_MODEL), row),
        out_shape=jax.ShapeDtypeStruct((t, D_MODEL), F32),
        scratch_shapes=[pltpu.VMEM((SUB_FFN, D_FF), BF16)],
        compiler_params=_params("parallel"),
        name="ffn_ln" if mix is None else "mix_ffn_ln_ple",
    )(*args)


def _proj_in_body(starts, x_ref, *refs):
    w_refs, o_ref = refs[:-1], refs[-1]
    xb = x_ref[...].astype(BF16)
    for w_ref, c0 in zip(w_refs, starts):
        n = w_ref.shape[1]
        for s0 in range(0, n, PROJ_COLS):
            s1 = min(s0 + PROJ_COLS, n)
            o_ref[:, c0 + s0:c0 + s1] = _dot(xb, w_ref[:, s0:s1])


def _proj_in(x, parts, n_out):
    t = x.shape[0]
    weights = [w for w, _ in parts]
    spans = sorted((c0, c0 + w.shape[1]) for w, c0 in parts)
    assert spans[0][0] == 0 and spans[-1][1] == n_out and all(a[1] == b[0] for a, b in zip(spans, spans[1:]))
    return pl.pallas_call(
        functools.partial(_proj_in_body, tuple(c0 for _, c0 in parts)),
        grid=(t // TM,),
        in_specs=[pl.BlockSpec((TM, D_MODEL), lambda i: (i, 0))] + [_resident(w.shape) for w in weights],
        out_specs=pl.BlockSpec((TM, n_out), lambda i: (i, 0)),
        out_shape=jax.ShapeDtypeStruct((t, n_out), F32),
        compiler_params=_params("parallel"),
        name="proj_in",
    )(x, *weights)


def _ssd_body(xbc_ref, z_ref, dt_ref, convw_ref, convb_ref, dtb_ref, alog_ref, dskip_ref,
              normw_ref, tril_ref, expand_ref, shift_ref, o_ref, xb_ref, state_ref):
    ch = SSD_CHUNK
    gw = SSD_GROUP_WIDTH

    @pl.when(pl.program_id(1) == 0)
    def _():
        xb_ref[0:ch, :] = jnp.zeros((ch, SSD_CONV_DIM), BF16)
        state_ref[...] = jnp.zeros(state_ref.shape, F32)

    def chunk(i, carry):
        rows = pl.ds(pl.multiple_of(i * ch, ch), ch)

        def conv_silu(cols):
            cur = xbc_ref[rows, cols]
            cur_b = cur.astype(BF16)
            xb_ref[ch:2 * ch, cols] = cur_b
            both = xb_ref[:, cols]
            acc = cur * convw_ref[SSD_CONV - 1:SSD_CONV, cols] + convb_ref[:, cols]
            for j in range(1, SSD_CONV):
                acc = acc + _dot(shift_ref[j - 1], both) * convw_ref[SSD_CONV - 1 - j:SSD_CONV - j, cols]
            xb_ref[0:ch, cols] = cur_b
            return _silu(acc)

        dt = _softplus(dt_ref[rows, :] + dtb_ref[...])
        da = dt * (-jnp.exp(alog_ref[...]))
        a_cum = _sel_dot(tril_ref[...], da)
        a_cum_t = a_cum.T
        dt_hi, dt_mid, _ = _split3(dt)
        dt_e = _dot(dt_hi, expand_ref[...]) + _dot(dt_mid, expand_ref[...])
        ac_e = _dot_sel(a_cum, expand_ref[...])
        ac_last = ac_e[ch - 1:ch, :]
        to_end = jnp.exp(ac_last - ac_e)
        carry_scale = jnp.exp(ac_e)
        chunk_decay = jnp.exp(ac_last)

        row = lax.broadcasted_iota(jnp.int32, (ch, ch), 0)
        col = lax.broadcasted_iota(jnp.int32, (ch, ch), 1)
        causal = col <= row
        lane_head = lax.broadcasted_iota(jnp.int32, (ch, gw), 1) >> int(math.log2(SSD_HEAD_DIM))

        groups = range(SSD_GROUPS)
        cols = [slice(g * gw, (g + 1) * gw) for g in groups]
        xs = [conv_silu(cols[g]) for g in groups]
        bc = [conv_silu(slice(SSD_INNER + k * gw, SSD_INNER + (k + 1) * gw)).astype(BF16)
              for k in range(2 * SSD_BC // gw)]
        bc = [v[:, half * SSD_STATE:(half + 1) * SSD_STATE] for v in bc for half in range(gw // SSD_STATE)]
        bs, cs = bc[:SSD_GROUPS], bc[SSD_GROUPS:]
        xdt = [xs[g] * dt_e[:, cols[g]] for g in groups]
        xdt_b = [v.astype(BF16) for v in xdt]
        scores = [_dot_nt(cs[g], bs[g]).astype(BF16) for g in groups]
        ys = []
        for g in groups:
            y_g = jnp.zeros((ch, gw), F32)
            for hh in range(SSD_HEADS_PER_GROUP):
                h = g * SSD_HEADS_PER_GROUP + hh
                seg = a_cum[:, h:h + 1] - a_cum_t[h:h + 1, :]
                decay = jnp.exp(jnp.where(causal, seg, NEG_BIG)).astype(BF16)
                full = _dot(scores[g] * decay, xdt_b[g])
                y_g = jnp.where(lane_head == hh, full, y_g)
            ys.append(y_g)
        for g in groups:
            gs = cols[g]
            state = state_ref[g]
            y_g = ys[g] + _dot(cs[g], state.astype(BF16)) * carry_scale[:, gs]
            state_ref[g] = state * chunk_decay[:, gs] + _dot_tn(bs[g], (xdt[g] * to_end[:, gs]).astype(BF16))
            y_g = y_g + xs[g] * dskip_ref[:, gs]
            y_g = y_g * _silu(z_ref[rows, gs])
            o_ref[rows, gs] = _rms_norm(y_g, normw_ref[:, gs]).astype(o_ref.dtype)
        return carry

    lax.fori_loop(0, xbc_ref.shape[0] // ch, chunk, 0)


def _ssd_mixer(h_ab, batch, seq, conv_w, conv_b, dt_bias, a_log, d_skip, norm_w):
    ch = SSD_CHUNK
    step = ch * MIX_STEP_CHUNKS
    nc = seq // step
    tok = lambda cb: (lambda b, c: (b * nc + c, cb))
    pad_heads = lambda v: jnp.zeros((1, LANES), F32).at[0, :SSD_HEADS].set(v.astype(F32))
    tril = jnp.asarray(np.tril(np.ones((ch, ch), np.float32)), BF16)
    expand = np.zeros((LANES, SSD_INNER), np.float32)
    for h in range(SSD_HEADS):
        expand[h, h * SSD_HEAD_DIM:(h + 1) * SSD_HEAD_DIM] = 1.0
    expand = jnp.asarray(expand, BF16)
    shift = np.zeros((SSD_CONV - 1, ch, 2 * ch), np.float32)
    for j in range(1, SSD_CONV):
        shift[j - 1, np.arange(ch), ch + np.arange(ch) - j] = 1.0
    shift = jnp.asarray(shift, BF16)
    dskip_e = jnp.repeat(d_skip.astype(F32), SSD_HEAD_DIM)[None, :]
    consts = [conv_w, conv_b[None, :], pad_heads(dt_bias), pad_heads(a_log), dskip_e,
              norm_w[None, :], tril, expand, shift]
    in_specs = [
        pl.BlockSpec((step, SSD_CONV_DIM), tok(AB_XBC // SSD_CONV_DIM)),
        pl.BlockSpec((step, SSD_INNER), tok(AB_Z // SSD_INNER)),
        pl.BlockSpec((step, LANES), tok(AB_DT // LANES)),
    ] + [_resident(a.shape) for a in consts]
    return pl.pallas_call(
        _ssd_body,
        grid=(batch, nc),
        in_specs=in_specs,
        out_specs=pl.BlockSpec((step, SSD_INNER), tok(0)),
        out_shape=jax.ShapeDtypeStruct((batch * seq, SSD_INNER), BF16),
        scratch_shapes=[pltpu.VMEM((2 * ch, SSD_CONV_DIM), BF16),
                        pltpu.VMEM((SSD_GROUPS, SSD_STATE, SSD_GROUP_WIDTH), F32)],
        compiler_params=_params("parallel", "arbitrary"),
        name="ssd_mixer",
    )(h_ab, h_ab, h_ab, *consts)


def _s5_slot(g, s):
    return (s + g) % S5_TILE_GROUPS


def _s5_body(nblk, u_ref, d_ref, m_ref, wst_ref, wofft_ref, a1_ref, a2_ref, o_ref, uscr_ref, yscr_ref, ug_ref,
             yg_ref):
    ng = S5_TILE_GROUPS
    ncol = S5_Q // ng
    nstrip = nblk // SUBLANES
    slot = lax.broadcasted_iota(jnp.int32, (SUBLANES, LANES), 1) >> int(math.log2(S5_GROUP))
    blk = lax.broadcasted_iota(jnp.int32, (nblk, 2 * S5_STATE), 0)

    def pitch_in(c, carry):
        src = pl.multiple_of(c * S5_Q, S5_Q)
        dst = pl.multiple_of(c * S5_PITCH, SUBLANES)
        uscr_ref[pl.ds(dst, S5_Q), :] = u_ref[pl.ds(src, S5_Q), :]
        return carry
    lax.fori_loop(0, nblk, pitch_in, 0, unroll=8)

    def gather(i, carry):
        base = pl.multiple_of(i * (SUBLANES * S5_PITCH), SUBLANES)
        row = pl.multiple_of(i * SUBLANES, SUBLANES)
        for m in range(ncol):
            rolled = []
            for s in range(ng):
                us = uscr_ref[pl.ds(base + ng * m + s, SUBLANES, stride=S5_PITCH), :]
                rolled.append(pltpu.roll(us, S5_GROUP * s, axis=1) if s else us)
            for g in range(ng):
                v = rolled[0]
                for s in range(1, ng):
                    v = jnp.where(slot == _s5_slot(g, s), rolled[s], v)
                ug_ref[pl.ds(row, SUBLANES), (g * ncol + m) * LANES:(g * ncol + m + 1) * LANES] = v
        return carry
    lax.fori_loop(0, nstrip, gather, 0, unroll=2)

    us = [ug_ref[:, g * S5_ROW:(g + 1) * S5_ROW].astype(BF16) for g in range(ng)]
    xs = [_dot(us[g], wst_ref[g]) for g in range(ng)]
    for k in range(int(math.log2(nblk))):
        sh = 1 << k
        for g in range(ng):
            prev = jnp.where(blk >= sh, pltpu.roll(xs[g], sh, axis=0), 0.0)
            xs[g] = (xs[g] + prev * a1_ref[g, k:k + 1, :]
                     + pltpu.roll(prev, S5_STATE, axis=1) * a2_ref[g, k:k + 1, :])
    for g in range(ng):
        x_in = jnp.where(blk >= 1, pltpu.roll(xs[g], 1, axis=0), 0.0)
        yg_ref[:, g * S5_ROW:(g + 1) * S5_ROW] = (_dot(us[g], m_ref[g])
                                                   + _dot_nt(x_in.astype(BF16), wofft_ref[g]))

    def scatter(i, carry):
        base = pl.multiple_of(i * (SUBLANES * S5_PITCH), SUBLANES)
        row = pl.multiple_of(i * SUBLANES, SUBLANES)
        for m in range(ncol):
            cols = [yg_ref[pl.ds(row, SUBLANES), (g * ncol + m) * LANES:(g * ncol + m + 1) * LANES]
                    for g in range(ng)]
            for tt in range(ng):
                z = cols[0]
                for g in range(1, ng):
                    z = jnp.where(slot == _s5_slot(g, tt), cols[g], z)
                sh = (LANES - S5_GROUP * tt) % LANES
                yscr_ref[pl.ds(base + ng * m + tt, SUBLANES, stride=S5_PITCH), :] = (
                    pltpu.roll(z, sh, axis=1) if sh else z)
        return carry
    lax.fori_loop(0, nstrip, scatter, 0, unroll=2)

    def pitch_out(c, carry):
        src = pl.multiple_of(c * S5_PITCH, SUBLANES)
        dst = pl.multiple_of(c * S5_Q, S5_Q)
        y = yscr_ref[pl.ds(src, S5_Q), :] + d_ref[...] * u_ref[pl.ds(dst, S5_Q), :]
        o_ref[pl.ds(dst, S5_Q), :] = y.astype(o_ref.dtype)
        return carry
    lax.fori_loop(0, nblk, pitch_out, 0, unroll=8)


def _s5_core(h_ab, d_skip, batch, seq, tables):
    nblk = seq // S5_Q
    ng = S5_TILE_GROUPS
    grp = lambda a: pl.BlockSpec((ng,) + a.shape[1:], lambda j, b: (j, 0, 0))
    return pl.pallas_call(
        functools.partial(_s5_body, nblk),
        grid=(S5_GROUPS // ng, batch),
        in_specs=[pl.BlockSpec((seq, LANES), lambda j, b: (b, AB_U // LANES + j)),
                  pl.BlockSpec((1, LANES), lambda j, b: (0, j))]
        + [grp(a) for a in tables],
        out_specs=pl.BlockSpec((seq, LANES), lambda j, b: (b, j)),
        out_shape=jax.ShapeDtypeStruct((batch * seq, S5_WIDTH), BF16),
        scratch_shapes=[pltpu.VMEM((nblk * S5_PITCH, LANES), F32), pltpu.VMEM((nblk * S5_PITCH, LANES), F32),
                        pltpu.VMEM((nblk, ng * S5_ROW), F32), pltpu.VMEM((nblk, ng * S5_ROW), F32)],
        compiler_params=_params("parallel", "parallel"),
        name="s5_core",
    )(h_ab, d_skip, *tables)


def _s5_table_body(kt_ref, pw_rr_ref, pw_ii_ref, b1_ref, b2_ref, c1_ref, c2_ref, m_ref, wst_ref, wofft_ref):
    ng = S5_TILE_GROUPS
    ncol = S5_Q // ng
    lane = lax.broadcasted_iota(jnp.int32, (S5_GROUP, LANES), 1)
    for g in range(ng):
        strip = [jnp.zeros((S5_GROUP, LANES), F32)] * ncol + [kt_ref[g, :, c * LANES:(c + 1) * LANES]
                                                            for c in range(ncol)]
        b1, b2, c1, c2 = b1_ref[g], b2_ref[g], c1_ref[g], c2_ref[g]
        for s in range(S5_Q):
            start = ncol * LANES - S5_GROUP * s
            a, sh = start // LANES, start % LANES
            k = ng * (s // ng) + _s5_slot(g, s % ng)
            rows = slice(S5_GROUP * k, S5_GROUP * (k + 1))
            for c in range(ncol):
                if sh:
                    w = jnp.where(lane < LANES - sh, pltpu.roll(strip[a + c], LANES - sh, axis=1),
                                  pltpu.roll(strip[a + c + 1], LANES - sh, axis=1))
                else:
                    w = strip[a + c]
                if g:
                    w = pltpu.roll(w, S5_GROUP * g, axis=1)
                m_ref[g, rows, c * LANES:(c + 1) * LANES] = w.astype(BF16)
            e = S5_Q - 1 - s
            wst_ref[g, rows, :] = (pw_rr_ref[g, e:e + 1, :] * b1 + pw_ii_ref[g, e:e + 1, :] * b2).astype(BF16)
            wofft_ref[g, rows, :] = (pw_rr_ref[g, s + 1:s + 2, :] * c1
                                     + pw_ii_ref[g, s + 1:s + 2, :] * c2).astype(BF16)


def _s5_tables(lam_re, lam_im, log_dt, b_re, b_im, c_re, c_im, nblk):
    q, ng = S5_Q, S5_TILE_GROUPS
    hp = lax.Precision.HIGHEST
    lam = lax.complex(lam_re.astype(F32), lam_im.astype(F32))
    ldt = lam * jnp.exp(log_dt.astype(F32))[:, None]
    lam_bar = jnp.exp(ldt)
    b_bar = ((lam_bar - 1.0) / lam)[..., None] * lax.complex(b_re.astype(F32), b_im.astype(F32))
    c = lax.complex(c_re.astype(F32), c_im.astype(F32))
    tau = jnp.arange(q + 1, dtype=F32)
    pw = jnp.exp(ldt[:, None, :] * tau[None, :, None])
    kt = jnp.real(jnp.einsum('gpn,gtn,gnq->gqtp', c, pw[:, :q], b_bar, precision=hp)).reshape(
        S5_GROUPS, S5_GROUP, S5_ROW)
    halves = lambda lo, hi: jnp.concatenate([lo, hi], axis=-1)
    bt = jnp.transpose(b_bar, (0, 2, 1))
    small = [halves(jnp.real(pw), jnp.real(pw)), halves(jnp.imag(pw), jnp.imag(pw)),
             halves(jnp.real(bt), jnp.imag(bt)), halves(-jnp.imag(bt), jnp.real(bt)),
             halves(jnp.real(c), -jnp.imag(c)), halves(-jnp.imag(c), -jnp.real(c))]
    grp = lambda shape: pl.BlockSpec((ng,) + shape, lambda j: (j, 0, 0))
    shapes = [(S5_ROW, S5_ROW), (S5_ROW, 2 * S5_STATE), (S5_ROW, 2 * S5_STATE)]
    m, wst, wofft = pl.pallas_call(
        _s5_table_body,
        grid=(S5_GROUPS // ng,),
        in_specs=[grp(kt.shape[1:])] + [grp(a.shape[1:]) for a in small],
        out_specs=[grp(sh) for sh in shapes],
        out_shape=[jax.ShapeDtypeStruct((S5_GROUPS,) + sh, BF16) for sh in shapes],
        compiler_params=_params("parallel"),
        name="s5_tables",
    )(kt, *small)
    nlev = int(math.log2(nblk))
    step = jnp.exp(ldt[:, None, :] * (q * 2.0 ** jnp.arange(nlev, dtype=F32))[None, :, None])
    a1 = halves(jnp.real(step), jnp.real(step))
    a2 = halves(-jnp.imag(step), jnp.imag(step))
    return m, wst, wofft, a1, a2


def _lin_levels(c):
    return [c >> (i + 1) for i in range(int(math.log2(c)))]


def _lin_tables(c):
    t = np.arange(c)[:, None]
    j = np.arange(c)[None, :]
    lvl = np.full((c, c), -1, np.int32)
    lvl[np.arange(c), np.arange(c)] = 0
    sgn = []
    for i, b in enumerate(_lin_levels(c)):
        same = (t // (2 * b)) == (j // (2 * b))
        second = (t % (2 * b)) >= b
        lvl[same & second & ((j % (2 * b)) < b)] = i + 1
        sgn.append(np.broadcast_to(np.where(second, 1.0, -1.0) * math.log2(math.e), (c, LANES)))
    return (jnp.asarray((j <= t).astype(np.float32), BF16), jnp.asarray(lvl),
            jnp.asarray(np.stack(sgn), F32))


def _gate_factors(g, tril, sgn_ref, gcum_ref):
    c = LIN_CHUNK
    gcum = _sel_dot(tril, g)
    gcum_ref[...] = gcum
    pos = lax.broadcasted_iota(jnp.int32, (c, LANES), 0)
    factors = []
    for i, b in enumerate(_lin_levels(c)):
        if 2 * b >= SUBLANES:
            mids = [jnp.broadcast_to(gcum_ref[blk * 2 * b + b - 1:blk * 2 * b + b, :], (2 * b, LANES))
                    for blk in range(c // (2 * b))]
            gmid = mids[0] if len(mids) == 1 else jnp.concatenate(mids, axis=0)
        elif b == 2:
            p4 = pos & 3
            gmid = jnp.where(p4 == 0, pltpu.roll(gcum, c - 1, axis=0),
                             jnp.where(p4 == 1, gcum,
                                       jnp.where(p4 == 2, pltpu.roll(gcum, 1, axis=0),
                                                 pltpu.roll(gcum, 2, axis=0))))
        else:
            gmid = jnp.where((pos & 1) == 1, pltpu.roll(gcum, 1, axis=0), gcum)
        factors.append(jnp.exp2((gcum - gmid) * sgn_ref[i]).astype(BF16))
    e_cum = jnp.exp(gcum)
    return e_cum, e_cum.astype(BF16), jnp.exp(gcum[c - 1:c, :] - gcum).astype(BF16), factors


def _lin_body(h_ref, gqk_ref, gv_ref, gr_ref, glr_ref, lb_ref, onemlb_ref, hnorm_ref,
              wup_ref, bgate_ref, gnorm_ref, tril_ref, lvl_ref, sgn_ref, o_ref, state_ref, gcum_ref):
    c = LIN_CHUNK

    @pl.when(pl.program_id(1) == 0)
    def _():
        state_ref[...] = jnp.zeros(state_ref.shape, F32)

    def chunk(ci, carry):
        rows = pl.ds(pl.multiple_of(ci * c, c), c)

        hq = h_ref[rows, 0:HGRN_WIDTH]
        hf = h_ref[rows, HGRN_WIDTH:2 * HGRN_WIDTH]
        gap = onemlb_ref[...] * _sigmoid(hf)
        log_f = jnp.maximum(jnp.log(lb_ref[...] + gap), NEG_BIG)
        q_c = _silu(hq)
        k_c = onemlb_ref[...] - gap
        pre = _dot(glr_ref[rows, :].astype(BF16), wup_ref[...]) + bgate_ref[...]
        log_a = _log_sigmoid(pre) * (1.0 / GLA_TAU)
        q_d = gqk_ref[rows, 0:GLA_QK] * (GLA_DK ** -0.5)
        k_d = gqk_ref[rows, GLA_QK:2 * GLA_QK]

        lvl = lvl_ref[...]
        lane = lax.broadcasted_iota(jnp.int32, (c, LANES), 1)
        tril = tril_ref[...]

        heads = []
        for hd in range(LIN_HEADS):
            ls = slice(hd * HEAD_W, (hd + 1) * HEAD_W)
            heads.append((hd, q_c[:, ls], k_c[:, ls], hd,
                          h_ref[rows, 2 * HGRN_WIDTH + hd * HEAD_W:2 * HGRN_WIDTH + (hd + 1) * HEAD_W],
                          hnorm_ref[:, ls],
                          h_ref[rows, 3 * HGRN_WIDTH + hd * HEAD_W:3 * HGRN_WIDTH + (hd + 1) * HEAD_W], ls))
        for hd in range(LIN_HEADS):
            tile, half = divmod(hd, LANES // GLA_DK)
            ts = slice(tile * LANES, (tile + 1) * LANES)
            ls = slice(hd * HEAD_W, (hd + 1) * HEAD_W)
            mine = (lane >> int(math.log2(GLA_DK))) == half
            heads.append((LIN_HEADS + hd, jnp.where(mine, q_d[:, ts], 0.0), jnp.where(mine, k_d[:, ts], 0.0),
                          LIN_HEADS + tile, gv_ref[rows, ls], gnorm_ref[:, ls], gr_ref[rows, ls],
                          slice(HGRN_WIDTH + hd * HEAD_W, HGRN_WIDTH + (hd + 1) * HEAD_W)))

        gates = [log_f[:, t * LANES:(t + 1) * LANES] for t in range(LIN_HEADS)]
        gates += [log_a[:, t * LANES:(t + 1) * LANES] for t in range(GLA_QK // LANES)]
        factors = [_gate_factors(g, tril, sgn_ref, gcum_ref.at[t]) for t, g in enumerate(gates)]
        qkb = [(q.astype(BF16), k.astype(BF16)) for _, q, k, *_ in heads]
        attns = [None] * len(heads)
        for i in range(1 + len(_lin_levels(c))):
            mask = lvl == i
            for n, ((qb, kb), head) in enumerate(zip(qkb, heads)):
                if i == 0:
                    attns[n] = jnp.where(mask, _dot_nt(qb, kb), 0.0)
                else:
                    e = factors[head[3]][3][i - 1]
                    attns[n] = jnp.where(mask, _dot_nt(qb * e, kb * e), attns[n])
        attns = [a.astype(BF16) for a in attns]
        for (qb, kb), attn, (idx, _, _, tile, v, norm_w, gate, cols) in zip(qkb, attns, heads):
            e_cum, e_cum_b, e_end_b, _ = factors[tile]
            vb = v.astype(BF16)
            state_t = state_ref[idx]
            o = _dot(attn, vb) + _dot_nt(qb * e_cum_b, state_t.astype(BF16))
            state_ref[idx] = state_t * e_cum[c - 1:c, :] + _dot_tn(vb, kb * e_end_b)
            o_ref[rows, cols] = (_rms_norm(o, norm_w) * _silu(gate)).astype(o_ref.dtype)
        return carry

    lax.fori_loop(0, h_ref.shape[0] // c, chunk, 0)


def _lin_mixer(h_cd, batch, seq, lb, hgrn_norm_w, gla_w_gate_up, gla_b_gate, gla_norm_w):
    c = LIN_CHUNK
    step = c * MIX_STEP_CHUNKS
    nc = seq // step
    tok = lambda cb: (lambda b, i: (b * nc + i, cb))
    lb = lb.astype(F32)[None, :]
    wup = jnp.zeros((LANES, GLA_QK), F32).at[:GLA_RANK].set(gla_w_gate_up).astype(BF16)
    tril, lvl, sgn = _lin_tables(c)
    consts = [lb, 1.0 - lb, hgrn_norm_w[None, :], wup, gla_b_gate[None, :],
              gla_norm_w[None, :], tril, lvl, sgn]
    in_specs = [
        pl.BlockSpec((step, 4 * HGRN_WIDTH), tok(CD_H // (4 * HGRN_WIDTH))),
        pl.BlockSpec((step, 2 * GLA_QK), tok(CD_GQK // (2 * GLA_QK))),
        pl.BlockSpec((step, GLA_WIDTH), tok(CD_GV // GLA_WIDTH)),
        pl.BlockSpec((step, GLA_WIDTH), tok(CD_GR // GLA_WIDTH)),
        pl.BlockSpec((step, LANES), tok(CD_GLR // LANES)),
    ] + [_resident(a.shape) for a in consts]
    return pl.pallas_call(
        _lin_body,
        grid=(batch, nc),
        in_specs=in_specs,
        out_specs=pl.BlockSpec((step, HGRN_WIDTH + GLA_WIDTH), tok(0)),
        out_shape=jax.ShapeDtypeStruct((batch * seq, HGRN_WIDTH + GLA_WIDTH), BF16),
        scratch_shapes=[pltpu.VMEM((2 * LIN_HEADS, HEAD_W, HEAD_W), F32),
                        pltpu.VMEM((LIN_HEADS + GLA_QK // LANES, c, LANES), F32)],
        compiler_params=_params("parallel", "arbitrary"),
        name="lin_mixer",
    )(h_cd, h_cd, h_cd, h_cd, h_cd, *consts)


def _cols(w, start, stop, width=None):
    part = w[:, start:stop].astype(BF16)
    if width is not None and width > stop - start:
        part = jnp.pad(part, ((0, 0), (0, width - (stop - start))))
    return part


def _hgrn_lower_bound(lb_logits, layer):
    cum = jnp.cumsum(jax.nn.softmax(lb_logits.astype(F32), axis=0), axis=0)
    return cum[layer] - cum[0]


def kernel(x, p, ln_g, ln_b, ffn_w_gate, ffn_w_up, ffn_w_down, ple_w_gate, ple_w_proj, ab_w_in, ab_w_out,
           ssd_conv_w, ssd_conv_b, ssd_dt_bias, ssd_a_log, ssd_d, ssd_norm_w, s5_lambda_re, s5_lambda_im,
           s5_log_dt, s5_b_re, s5_b_im, s5_c_re, s5_c_im, s5_d, s5_w_glu, s5_b_glu, cd_w_in, cd_w_out,
           hgrn_lb_logits, hgrn_norm_w, gla_w_gate_up, gla_b_gate, gla_norm_w):
    batch, seq, _ = x.shape
    t = batch * seq
    x = x.reshape(t, D_MODEL)
    bf = lambda w: w.astype(BF16)
    ab_splits = (SSD_INNER, SSD_CONV_DIM, SSD_HEADS, S5_WIDTH)
    cd_splits = (HGRN_WIDTH, HGRN_WIDTH, HGRN_WIDTH, HGRN_WIDTH, GLA_QK, GLA_QK, GLA_WIDTH, GLA_RANK,
                 GLA_WIDTH)
    wg, wu, wd = bf(ffn_w_gate), bf(ffn_w_up), bf(ffn_w_down)
    pg, pp = bf(ple_w_gate), bf(ple_w_proj)
    p = p.reshape(DEPTH, t, PLE_DIM)
    ln_g = ln_g.reshape(DEPTH * 3, 1, D_MODEL)
    ln_b = ln_b.reshape(DEPTH * 3, 1, D_MODEL)
    w_ab_out, w_cd_out = bf(ab_w_out), bf(cd_w_out)
    w_glu, b_glu = bf(s5_w_glu), s5_b_glu[:, None, :]
    for i in range(DEPTH):
        j = i // 2
        x = _ffn_ln(x, wg, wu, wd, ln_g, ln_b, i, 0, 3 * i)
        if i % 2 == 0:
            w, e = ab_w_in[j], np.cumsum((0,) + ab_splits)
            h_ab = _proj_in(x, [(_cols(w, e[1], e[2]), AB_XBC), (_cols(w, e[0], e[1]), AB_Z),
                                (_cols(w, e[3], e[4]), AB_U), (_cols(w, e[2], e[3], AB_PACK - AB_DT), AB_DT)],
                            AB_PACK)
            y_a = _ssd_mixer(h_ab, batch, seq, ssd_conv_w[j], ssd_conv_b[j], ssd_dt_bias[j], ssd_a_log[j],
                             ssd_d[j], ssd_norm_w[j])
            tables = _s5_tables(s5_lambda_re[j], s5_lambda_im[j], s5_log_dt[j], s5_b_re[j], s5_b_im[j],
                                s5_c_re[j], s5_c_im[j], seq // S5_Q)
            y_s = _s5_core(h_ab, s5_d[j].reshape(1, S5_WIDTH).astype(F32), batch, seq, tables)
            mix = ("ab", (y_a, y_s), [(w_ab_out, (j,)), (w_glu, (j,)), (b_glu, (j,))])
        else:
            w, e = cd_w_in[j], np.cumsum((0,) + cd_splits)
            h_cd = _proj_in(x, [(_cols(w, e[0], e[7]), CD_H), (_cols(w, e[8], e[9]), CD_GR),
                                (_cols(w, e[7], e[8], CD_PACK - CD_GLR), CD_GLR)], CD_PACK)
            o_cd = _lin_mixer(h_cd, batch, seq, _hgrn_lower_bound(hgrn_lb_logits, i), hgrn_norm_w[j],
                              gla_w_gate_up[j], gla_b_gate[j], gla_norm_w[j])
            mix = ("cd", (o_cd,), [(w_cd_out, (j,))])
        x = _ffn_ln(x, wg, wu, wd, ln_g, ln_b, i, 1, 3 * i + 2, mix=mix, ple=(p, pg, pp))
    return x.reshape(batch, seq, D_MODEL)
```

```python
import functools
import math

import jax
import jax.numpy as jnp
import numpy as np
from jax import lax
from jax.experimental import pallas as pl
from jax.experimental.pallas import tpu as pltpu

F32 = jnp.float32
BF16 = jnp.bfloat16

D_MODEL = 1024
D_FF = 2816
PLE_DIM = 256
DEPTH = 2
DN_ALPHA = (2.0 * DEPTH) ** 0.25
LN_EPS = 1e-5
NEG_BIG = -1e30
SSD_HEADS = 16
SSD_HEAD_DIM = 64
SSD_GROUPS = 4
SSD_STATE = 128
SSD_CONV = 4
SSD_INNER = 1024
SSD_BC = SSD_GROUPS * SSD_STATE
SSD_CONV_DIM = SSD_INNER + 2 * SSD_BC
SSD_GROUP_WIDTH = SSD_INNER // SSD_GROUPS
SSD_HEADS_PER_GROUP = SSD_HEADS // SSD_GROUPS
S5_WIDTH = 1024
S5_GROUPS = 64
S5_GROUP = 16
S5_STATE = 64
LIN_HEADS = 4
HGRN_WIDTH = 512
GLA_DK = 64
GLA_QK = LIN_HEADS * GLA_DK
GLA_WIDTH = 512
GLA_RANK = 16
GLA_TAU = 16.0
HEAD_W = 128

LANES = 128
SUBLANES = 8
VMEM_LIMIT = 58 * 1024 * 1024

TM = 512
TM_FFN = 1024
SUB_FFN = 512
FF_CHUNK = 256
PROJ_COLS = 512
SSD_CHUNK = 128
MIX_STEP_CHUNKS = 4
LIN_CHUNK = 128
S5_Q = 32
S5_ROW = S5_Q * S5_GROUP
S5_TILE_GROUPS = LANES // S5_GROUP
S5_PITCH = 40

AB_XBC, AB_Z, AB_U, AB_DT, AB_PACK = 0, 2048, 3072, 4096, 4224
CD_H, CD_GQK, CD_GV, CD_GR, CD_GLR, CD_PACK = 0, 2048, 2560, 3072, 3584, 3712


def _resident(shape):
    n = len(shape)
    return pl.BlockSpec(shape, lambda *_: (0,) * n, pipeline_mode=pl.Buffered(1))


def _dot(a, b):
    return jnp.dot(a, b, preferred_element_type=F32)


def _dot_nt(a, b):
    return lax.dot_general(a, b, (((1,), (1,)), ((), ())), preferred_element_type=F32)


def _dot_tn(a, b):
    return lax.dot_general(a, b, (((0,), (0,)), ((), ())), preferred_element_type=F32)


def _split3(v):
    hi = v.astype(BF16)
    r = v - hi.astype(F32)
    mid = r.astype(BF16)
    lo = (r - mid.astype(F32)).astype(BF16)
    return hi, mid, lo


def _sel_dot(sel, v):
    hi, mid, lo = _split3(v)
    return _dot(sel, hi) + _dot(sel, mid) + _dot(sel, lo)


def _dot_sel(v, sel):
    hi, mid, lo = _split3(v)
    return _dot(hi, sel) + _dot(mid, sel) + _dot(lo, sel)


def _sigmoid(x):
    return 1.0 / (1.0 + jnp.exp(-x))


def _silu(x):
    h = 0.5 * x
    return h * jnp.tanh(h) + h


def _log_sigmoid(x):
    return jnp.minimum(x, 0.0) - jnp.log1p(jnp.exp(-jnp.abs(x)))


def _softplus(x):
    return jnp.maximum(x, 0.0) + jnp.log1p(jnp.exp(-jnp.abs(x)))


def _gelu_tanh(x):
    return 0.5 * x * (1.0 + jnp.tanh(math.sqrt(2.0 / math.pi) * (x + 0.044715 * (x * x * x))))


def _layer_norm(y, g, b):
    mu = jnp.mean(y, axis=-1, keepdims=True)
    yc = y - mu
    var = jnp.mean(yc * yc, axis=-1, keepdims=True)
    return yc * lax.rsqrt(var + LN_EPS) * g + b


def _rms_norm(y, w):
    return y * lax.rsqrt(jnp.mean(y * y, axis=-1, keepdims=True) + LN_EPS) * w


def _params(*sem):
    return pltpu.CompilerParams(dimension_semantics=sem, vmem_limit_bytes=VMEM_LIMIT)


def _ffn_body(mix, with_ple, x_ref, *refs):
    refs = list(refs)
    take = lambda n: [refs.pop(0) for _ in range(n)]
    if mix == "ab":
        ya_ref, ys_ref, wout_ref, wglu_ref, bglu_ref, g1_ref, b1_ref = take(7)
    elif mix == "cd":
        oc_ref, wout_ref, g1_ref, b1_ref = take(4)
    wg_ref, wu_ref, wd_ref, g_ref, b_ref = take(5)
    if with_ple:
        p_ref, pg_ref, pp_ref = take(3)
    o_ref, h_ref = refs
    for r in range(x_ref.shape[0] // SUB_FFN):
        rows = slice(r * SUB_FFN, (r + 1) * SUB_FFN)
        x = x_ref[rows, :]
        if mix == "ab":
            y = _gelu_tanh(ys_ref[rows, :].astype(F32))
            gate = _sigmoid(_dot(y.astype(BF16), wglu_ref[...]) + bglu_ref[...])
            m = _dot(ya_ref[rows, :], wout_ref[0:SSD_INNER, :]) + _dot((y * gate).astype(BF16), wout_ref[SSD_INNER:, :])
            x = _layer_norm(DN_ALPHA * x + m, g1_ref[...], b1_ref[...])
        elif mix == "cd":
            x = _layer_norm(DN_ALPHA * x + _dot(oc_ref[rows, :], wout_ref[...]), g1_ref[...], b1_ref[...])
        xb = x.astype(BF16)
        for c in range(D_FF // FF_CHUNK):
            sl = slice(c * FF_CHUNK, (c + 1) * FF_CHUNK)
            gate = _dot(xb, wg_ref[:, sl])
            up = _dot(xb, wu_ref[:, sl])
            h_ref[:, sl] = (_silu(gate) * up).astype(BF16)
        y = _layer_norm(DN_ALPHA * x + 0.5 * _dot(h_ref[...], wd_ref[...]), g_ref[...], b_ref[...])
        if with_ple:
            gate = _sigmoid(_dot(y.astype(BF16), pg_ref[...]))
            y = y + gate * _dot(p_ref[rows, :].astype(BF16), pp_ref[...])
        o_ref[rows, :] = y


def _pick(a, *idx):
    rest = a.shape[len(idx):]
    return pl.BlockSpec((None,) * len(idx) + rest, lambda *_: idx + (0,) * len(rest),
                        pipeline_mode=pl.Buffered(1))


def _ffn_ln(x, wg, wu, wd, ln_g, ln_b, layer, pos, ln_idx, mix=None, ple=None):
    t = x.shape[0]
    row = lambda i: (i, 0)
    in_specs = [pl.BlockSpec((TM_FFN, D_MODEL), row)]
    args = [x]
    if mix is not None:
        kind, acts, consts = mix
        in_specs += [pl.BlockSpec((TM_FFN, a.shape[1]), row) for a in acts]
        in_specs += [_pick(a, *idx) for a, idx in consts]
        in_specs += [_pick(ln_g, ln_idx - 1), _pick(ln_b, ln_idx - 1)]
        args += list(acts) + [a for a, _ in consts] + [ln_g, ln_b]
    in_specs += [_pick(wg, layer, pos), _pick(wu, layer, pos), _pick(wd, layer, pos),
                 _pick(ln_g, ln_idx), _pick(ln_b, ln_idx)]
    args += [wg, wu, wd, ln_g, ln_b]
    if ple is not None:
        p, pg, pp = ple
        in_specs += [pl.BlockSpec((None, TM_FFN, PLE_DIM), lambda i: (layer, i, 0)), _pick(pg, layer), _pick(pp, layer)]
        args += [p, pg, pp]
    return pl.pallas_call(
        functools.partial(_ffn_body, None if mix is None else mix[0], ple is not None),
        grid=(t // TM_FFN,),
        in_specs=in_specs,
        out_specs=pl.BlockSpec((TM_FFN, D_MODEL), row),
        out_shape=jax.ShapeDtypeStruct((t, D_MODEL), F32),
        scratch_shapes=[pltpu.VMEM((SUB_FFN, D_FF), BF16)],
        compiler_params=_params("parallel"),
        name="ffn_ln" if mix is None else "mix_ffn_ln_ple",
    )(*args)


def _proj_in_body(starts, x_ref, *refs):
    w_refs, o_ref = refs[:-1], refs[-1]
    xb = x_ref[...].astype(BF16)
    for w_ref, c0 in zip(w_refs, starts):
        n = w_ref.shape[1]
        for s0 in range(0, n, PROJ_COLS):
            s1 = min(s0 + PROJ_COLS, n)
            o_ref[:, c0 + s0:c0 + s1] = _dot(xb, w_ref[:, s0:s1])


def _proj_in(x, parts, n_out):
    t = x.shape[0]
    weights = [w for w, _ in parts]
    spans = sorted((c0, c0 + w.shape[1]) for w, c0 in parts)
    assert spans[0][0] == 0 and spans[-1][1] == n_out and all(a[1] == b[0] for a, b in zip(spans, spans[1:]))
    return pl.pallas_call(
        functools.partial(_proj_in_body, tuple(c0 for _, c0 in parts)),
        grid=(t // TM,),
        in_specs=[pl.BlockSpec((TM, D_MODEL), lambda i: (i, 0))] + [_resident(w.shape) for w in weights],
        out_specs=pl.BlockSpec((TM, n_out), lambda i: (i, 0)),
        out_shape=jax.ShapeDtypeStruct((t, n_out), F32),
        compiler_params=_params("parallel"),
        name="proj_in",
    )(x, *weights)


def _ssd_body(xbc_ref, z_ref, dt_ref, convw_ref, convb_ref, dtb_ref, alog_ref, dskip_ref,
              normw_ref, tril_ref, expand_ref, shift_ref, o_ref, xb_ref, state_ref):
    ch = SSD_CHUNK
    gw = SSD_GROUP_WIDTH

    @pl.when(pl.program_id(1) == 0)
    def _():
        xb_ref[0:ch, :] = jnp.zeros((ch, SSD_CONV_DIM), BF16)
        state_ref[...] = jnp.zeros(state_ref.shape, F32)

    def chunk(i, carry):
        rows = pl.ds(pl.multiple_of(i * ch, ch), ch)

        def conv_silu(cols):
            cur = xbc_ref[rows, cols]
            cur_b = cur.astype(BF16)
            xb_ref[ch:2 * ch, cols] = cur_b
            both = xb_ref[:, cols]
            acc = cur * convw_ref[SSD_CONV - 1:SSD_CONV, cols] + convb_ref[:, cols]
            for j in range(1, SSD_CONV):
                acc = acc + _dot(shift_ref[j - 1], both) * convw_ref[SSD_CONV - 1 - j:SSD_CONV - j, cols]
            xb_ref[0:ch, cols] = cur_b
            return _silu(acc)

        dt = _softplus(dt_ref[rows, :] + dtb_ref[...])
        da = dt * (-jnp.exp(alog_ref[...]))
        a_cum = _sel_dot(tril_ref[...], da)
        a_cum_t = a_cum.T
        dt_hi, dt_mid, _ = _split3(dt)
        dt_e = _dot(dt_hi, expand_ref[...]) + _dot(dt_mid, expand_ref[...])
        ac_e = _dot_sel(a_cum, expand_ref[...])
        ac_last = ac_e[ch - 1:ch, :]
        to_end = jnp.exp(ac_last - ac_e)
        carry_scale = jnp.exp(ac_e)
        chunk_decay = jnp.exp(ac_last)

        row = lax.broadcasted_iota(jnp.int32, (ch, ch), 0)
        col = lax.broadcasted_iota(jnp.int32, (ch, ch), 1)
        causal = col <= row
        lane_head = lax.broadcasted_iota(jnp.int32, (ch, gw), 1) >> int(math.log2(SSD_HEAD_DIM))

        groups = range(SSD_GROUPS)
        cols = [slice(g * gw, (g + 1) * gw) for g in groups]
        xs = [conv_silu(cols[g]) for g in groups]
        bc = [conv_silu(slice(SSD_INNER + k * gw, SSD_INNER + (k + 1) * gw)).astype(BF16)
              for k in range(2 * SSD_BC // gw)]
        bc = [v[:, half * SSD_STATE:(half + 1) * SSD_STATE] for v in bc for half in range(gw // SSD_STATE)]
        bs, cs = bc[:SSD_GROUPS], bc[SSD_GROUPS:]
        xdt = [xs[g] * dt_e[:, cols[g]] for g in groups]
        xdt_b = [v.astype(BF16) for v in xdt]
        scores = [_dot_nt(cs[g], bs[g]).astype(BF16) for g in groups]
        ys = []
        for g in groups:
            y_g = jnp.zeros((ch, gw), F32)
            for hh in range(SSD_HEADS_PER_GROUP):
                h = g * SSD_HEADS_PER_GROUP + hh
                seg = a_cum[:, h:h + 1] - a_cum_t[h:h + 1, :]
                decay = jnp.exp(jnp.where(causal, seg, NEG_BIG)).astype(BF16)
                full = _dot(scores[g] * decay, xdt_b[g])
                y_g = jnp.where(lane_head == hh, full, y_g)
            ys.append(y_g)
        for g in groups:
            gs = cols[g]
            state = state_ref[g]
            y_g = ys[g] + _dot(cs[g], state.astype(BF16)) * carry_scale[:, gs]
            state_ref[g] = state * chunk_decay[:, gs] + _dot_tn(bs[g], (xdt[g] * to_end[:, gs]).astype(BF16))
            y_g = y_g + xs[g] * dskip_ref[:, gs]
            y_g = y_g * _silu(z_ref[rows, gs])
            o_ref[rows, gs] = _rms_norm(y_g, normw_ref[:, gs]).astype(o_ref.dtype)
        return carry

    lax.fori_loop(0, xbc_ref.shape[0] // ch, chunk, 0)


def _ssd_mixer(h_ab, batch, seq, conv_w, conv_b, dt_bias, a_log, d_skip, norm_w):
    ch = SSD_CHUNK
    step = ch * MIX_STEP_CHUNKS
    nc = seq // step
    tok = lambda cb: (lambda b, c: (b * nc + c, cb))
    pad_heads = lambda v: jnp.zeros((1, LANES), F32).at[0, :SSD_HEADS].set(v.astype(F32))
    tril = jnp.asarray(np.tril(np.ones((ch, ch), np.float32)), BF16)
    expand = np.zeros((LANES, SSD_INNER), np.float32)
    for h in range(SSD_HEADS):
        expand[h, h * SSD_HEAD_DIM:(h + 1) * SSD_HEAD_DIM] = 1.0
    expand = jnp.asarray(expand, BF16)
    shift = np.zeros((SSD_CONV - 1, ch, 2 * ch), np.float32)
    for j in range(1, SSD_CONV):
        shift[j - 1, np.arange(ch), ch + np.arange(ch) - j] = 1.0
    shift = jnp.asarray(shift, BF16)
    dskip_e = jnp.repeat(d_skip.astype(F32), SSD_HEAD_DIM)[None, :]
    consts = [conv_w, conv_b[None, :], pad_heads(dt_bias), pad_heads(a_log), dskip_e,
              norm_w[None, :], tril, expand, shift]
    in_specs = [
        pl.BlockSpec((step, SSD_CONV_DIM), tok(AB_XBC // SSD_CONV_DIM)),
        pl.BlockSpec((step, SSD_INNER), tok(AB_Z // SSD_INNER)),
        pl.BlockSpec((step, LANES), tok(AB_DT // LANES)),
    ] + [_resident(a.shape) for a in consts]
    return pl.pallas_call(
        _ssd_body,
        grid=(batch, nc),
        in_specs=in_specs,
        out_specs=pl.BlockSpec((step, SSD_INNER), tok(0)),
        out_shape=jax.ShapeDtypeStruct((batch * seq, SSD_INNER), BF16),
        scratch_shapes=[pltpu.VMEM((2 * ch, SSD_CONV_DIM), BF16),
                        pltpu.VMEM((SSD_GROUPS, SSD_STATE, SSD_GROUP_WIDTH), F32)],
        compiler_params=_params("parallel", "arbitrary"),
        name="ssd_mixer",
    )(h_ab, h_ab, h_ab, *consts)


def _s5_slot(g, s):
    return (s + g) % S5_TILE_GROUPS


def _s5_body(nblk, u_ref, d_ref, m_ref, wst_ref, wofft_ref, a1_ref, a2_ref, o_ref, uscr_ref, yscr_ref, ug_ref,
             yg_ref):
    ng = S5_TILE_GROUPS
    ncol = S5_Q // ng
    nstrip = nblk // SUBLANES
    slot = lax.broadcasted_iota(jnp.int32, (SUBLANES, LANES), 1) >> int(math.log2(S5_GROUP))
    blk = lax.broadcasted_iota(jnp.int32, (nblk, 2 * S5_STATE), 0)

    def pitch_in(c, carry):
        src = pl.multiple_of(c * S5_Q, S5_Q)
        dst = pl.multiple_of(c * S5_PITCH, SUBLANES)
        uscr_ref[pl.ds(dst, S5_Q), :] = u_ref[pl.ds(src, S5_Q), :]
        return carry
    lax.fori_loop(0, nblk, pitch_in, 0, unroll=8)

    def gather(i, carry):
        base = pl.multiple_of(i * (SUBLANES * S5_PITCH), SUBLANES)
        row = pl.multiple_of(i * SUBLANES, SUBLANES)
        for m in range(ncol):
            rolled = []
            for s in range(ng):
                us = uscr_ref[pl.ds(base + ng * m + s, SUBLANES, stride=S5_PITCH), :]
                rolled.append(pltpu.roll(us, S5_GROUP * s, axis=1) if s else us)
            for g in range(ng):
                v = rolled[0]
                for s in range(1, ng):
                    v = jnp.where(slot == _s5_slot(g, s), rolled[s], v)
                ug_ref[pl.ds(row, SUBLANES), (g * ncol + m) * LANES:(g * ncol + m + 1) * LANES] = v
        return carry
    lax.fori_loop(0, nstrip, gather, 0, unroll=2)

    us = [ug_ref[:, g * S5_ROW:(g + 1) * S5_ROW].astype(BF16) for g in range(ng)]
    xs = [_dot(us[g], wst_ref[g]) for g in range(ng)]
    for k in range(int(math.log2(nblk))):
        sh = 1 << k
        for g in range(ng):
            prev = jnp.where(blk >= sh, pltpu.roll(xs[g], sh, axis=0), 0.0)
            xs[g] = (xs[g] + prev * a1_ref[g, k:k + 1, :]
                     + pltpu.roll(prev, S5_STATE, axis=1) * a2_ref[g, k:k + 1, :])
    for g in range(ng):
        x_in = jnp.where(blk >= 1, pltpu.roll(xs[g], 1, axis=0), 0.0)
        yg_ref[:, g * S5_ROW:(g + 1) * S5_ROW] = (_dot(us[g], m_ref[g])
                                                   + _dot_nt(x_in.astype(BF16), wofft_ref[g]))

    def scatter(i, carry):
        base = pl.multiple_of(i * (SUBLANES * S5_PITCH), SUBLANES)
        row = pl.multiple_of(i * SUBLANES, SUBLANES)
        for m in range(ncol):
            cols = [yg_ref[pl.ds(row, SUBLANES), (g * ncol + m) * LANES:(g * ncol + m + 1) * LANES]
                    for g in range(ng)]
            for tt in range(ng):
                z = cols[0]
                for g in range(1, ng):
                    z = jnp.where(slot == _s5_slot(g, tt), cols[g], z)
                sh = (LANES - S5_GROUP * tt) % LANES
                yscr_ref[pl.ds(base + ng * m + tt, SUBLANES, stride=S5_PITCH), :] = (
                    pltpu.roll(z, sh, axis=1) if sh else z)
        return carry
    lax.fori_loop(0, nstrip, scatter, 0, unroll=2)

    def pitch_out(c, carry):
        src = pl.multiple_of(c * S5_PITCH, SUBLANES)
        dst = pl.multiple_of(c * S5_Q, S5_Q)
        y = yscr_ref[pl.ds(src, S5_Q), :] + d_ref[...] * u_ref[pl.ds(dst, S5_Q), :]
        o_ref[pl.ds(dst, S5_Q), :] = y.astype(o_ref.dtype)
        return carry
    lax.fori_loop(0, nblk, pitch_out, 0, unroll=8)


def _s5_core(h_ab, d_skip, batch, seq, tables):
    nblk = seq // S5_Q
    ng = S5_TILE_GROUPS
    grp = lambda a: pl.BlockSpec((ng,) + a.shape[1:], lambda j, b: (j, 0, 0))
    return pl.pallas_call(
        functools.partial(_s5_body, nblk),
        grid=(S5_GROUPS // ng, batch),
        in_specs=[pl.BlockSpec((seq, LANES), lambda j, b: (b, AB_U // LANES + j)),
                  pl.BlockSpec((1, LANES), lambda j, b: (0, j))]
        + [grp(a) for a in tables],
        out_specs=pl.BlockSpec((seq, LANES), lambda j, b: (b, j)),
        out_shape=jax.ShapeDtypeStruct((batch * seq, S5_WIDTH), BF16),
        scratch_shapes=[pltpu.VMEM((nblk * S5_PITCH, LANES), F32), pltpu.VMEM((nblk * S5_PITCH, LANES), F32),
                        pltpu.VMEM((nblk, ng * S5_ROW), F32), pltpu.VMEM((nblk, ng * S5_ROW), F32)],
        compiler_params=_params("parallel", "parallel"),
        name="s5_core",
    )(h_ab, d_skip, *tables)


def _s5_table_body(kt_ref, pw_rr_ref, pw_ii_ref, b1_ref, b2_ref, c1_ref, c2_ref, m_ref, wst_ref, wofft_ref):
    ng = S5_TILE_GROUPS
    ncol = S5_Q // ng
    lane = lax.broadcasted_iota(jnp.int32, (S5_GROUP, LANES), 1)
    for g in range(ng):
        strip = [jnp.zeros((S5_GROUP, LANES), F32)] * ncol + [kt_ref[g, :, c * LANES:(c + 1) * LANES]
                                                            for c in range(ncol)]
        b1, b2, c1, c2 = b1_ref[g], b2_ref[g], c1_ref[g], c2_ref[g]
        for s in range(S5_Q):
            start = ncol * LANES - S5_GROUP * s
            a, sh = start // LANES, start % LANES
            k = ng * (s // ng) + _s5_slot(g, s % ng)
            rows = slice(S5_GROUP * k, S5_GROUP * (k + 1))
            for c in range(ncol):
                if sh:
                    w = jnp.where(lane < LANES - sh, pltpu.roll(strip[a + c], LANES - sh, axis=1),
                                  pltpu.roll(strip[a + c + 1], LANES - sh, axis=1))
                else:
                    w = strip[a + c]
                if g:
                    w = pltpu.roll(w, S5_GROUP * g, axis=1)
                m_ref[g, rows, c * LANES:(c + 1) * LANES] = w.astype(BF16)
            e = S5_Q - 1 - s
            wst_ref[g, rows, :] = (pw_rr_ref[g, e:e + 1, :] * b1 + pw_ii_ref[g, e:e + 1, :] * b2).astype(BF16)
            wofft_ref[g, rows, :] = (pw_rr_ref[g, s + 1:s + 2, :] * c1
                                     + pw_ii_ref[g, s + 1:s + 2, :] * c2).astype(BF16)


def _s5_tables(lam_re, lam_im, log_dt, b_re, b_im, c_re, c_im, nblk):
    q, ng = S5_Q, S5_TILE_GROUPS
    hp = lax.Precision.HIGHEST
    lam = lax.complex(lam_re.astype(F32), lam_im.astype(F32))
    ldt = lam * jnp.exp(log_dt.astype(F32))[:, None]
    lam_bar = jnp.exp(ldt)
    b_bar = ((lam_bar - 1.0) / lam)[..., None] * lax.complex(b_re.astype(F32), b_im.astype(F32))
    c = lax.complex(c_re.astype(F32), c_im.astype(F32))
    tau = jnp.arange(q + 1, dtype=F32)
    pw = jnp.exp(ldt[:, None, :] * tau[None, :, None])
    kt = jnp.real(jnp.einsum('gpn,gtn,gnq->gqtp', c, pw[:, :q], b_bar, precision=hp)).reshape(
        S5_GROUPS, S5_GROUP, S5_ROW)
    halves = lambda lo, hi: jnp.concatenate([lo, hi], axis=-1)
    bt = jnp.transpose(b_bar, (0, 2, 1))
    small = [halves(jnp.real(pw), jnp.real(pw)), halves(jnp.imag(pw), jnp.imag(pw)),
             halves(jnp.real(bt), jnp.imag(bt)), halves(-jnp.imag(bt), jnp.real(bt)),
             halves(jnp.real(c), -jnp.imag(c)), halves(-jnp.imag(c), -jnp.real(c))]
    grp = lambda shape: pl.BlockSpec((ng,) + shape, lambda j: (j, 0, 0))
    shapes = [(S5_ROW, S5_ROW), (S5_ROW, 2 * S5_STATE), (S5_ROW, 2 * S5_STATE)]
    m, wst, wofft = pl.pallas_call(
        _s5_table_body,
        grid=(S5_GROUPS // ng,),
        in_specs=[grp(kt.shape[1:])] + [grp(a.shape[1:]) for a in small],
        out_specs=[grp(sh) for sh in shapes],
        out_shape=[jax.ShapeDtypeStruct((S5_GROUPS,) + sh, BF16) for sh in shapes],
        compiler_params=_params("parallel"),
        name="s5_tables",
    )(kt, *small)
    nlev = int(math.log2(nblk))
    step = jnp.exp(ldt[:, None, :] * (q * 2.0 ** jnp.arange(nlev, dtype=F32))[None, :, None])
    a1 = halves(jnp.real(step), jnp.real(step))
    a2 = halves(-jnp.imag(step), jnp.imag(step))
    return m, wst, wofft, a1, a2


def _lin_levels(c):
    return [c >> (i + 1) for i in range(int(math.log2(c)))]


def _lin_tables(c):
    t = np.arange(c)[:, None]
    j = np.arange(c)[None, :]
    lvl = np.full((c, c), -1, np.int32)
    lvl[np.arange(c), np.arange(c)] = 0
    sgn = []
    for i, b in enumerate(_lin_levels(c)):
        same = (t // (2 * b)) == (j // (2 * b))
        second = (t % (2 * b)) >= b
        lvl[same & second & ((j % (2 * b)) < b)] = i + 1
        sgn.append(np.broadcast_to(np.where(second, 1.0, -1.0) * math.log2(math.e), (c, LANES)))
    return (jnp.asarray((j <= t).astype(np.float32), BF16), jnp.asarray(lvl),
            jnp.asarray(np.stack(sgn), F32))


def _gate_factors(g, tril, sgn_ref, gcum_ref):
    c = LIN_CHUNK
    gcum = _sel_dot(tril, g)
    gcum_ref[...] = gcum
    pos = lax.broadcasted_iota(jnp.int32, (c, LANES), 0)
    factors = []
    for i, b in enumerate(_lin_levels(c)):
        if 2 * b >= SUBLANES:
            mids = [jnp.broadcast_to(gcum_ref[blk * 2 * b + b - 1:blk * 2 * b + b, :], (2 * b, LANES))
                    for blk in range(c // (2 * b))]
            gmid = mids[0] if len(mids) == 1 else jnp.concatenate(mids, axis=0)
        elif b == 2:
            p4 = pos & 3
            gmid = jnp.where(p4 == 0, pltpu.roll(gcum, c - 1, axis=0),
                             jnp.where(p4 == 1, gcum,
                                       jnp.where(p4 == 2, pltpu.roll(gcum, 1, axis=0),
                                                 pltpu.roll(gcum, 2, axis=0))))
        else:
            gmid = jnp.where((pos & 1) == 1, pltpu.roll(gcum, 1, axis=0), gcum)
        factors.append(jnp.exp2((gcum - gmid) * sgn_ref[i]).astype(BF16))
    e_cum = jnp.exp(gcum)
    return e_cum, e_cum.astype(BF16), jnp.exp(gcum[c - 1:c, :] - gcum).astype(BF16), factors


def _lin_body(h_ref, gqk_ref, gv_ref, gr_ref, glr_ref, lb_ref, onemlb_ref, hnorm_ref,
              wup_ref, bgate_ref, gnorm_ref, tril_ref, lvl_ref, sgn_ref, o_ref, state_ref, gcum_ref):
    c = LIN_CHUNK

    @pl.when(pl.program_id(1) == 0)
    def _():
        state_ref[...] = jnp.zeros(state_ref.shape, F32)

    def chunk(ci, carry):
        rows = pl.ds(pl.multiple_of(ci * c, c), c)

        hq = h_ref[rows, 0:HGRN_WIDTH]
        hf = h_ref[rows, HGRN_WIDTH:2 * HGRN_WIDTH]
        gap = onemlb_ref[...] * _sigmoid(hf)
        log_f = jnp.maximum(jnp.log(lb_ref[...] + gap), NEG_BIG)
        q_c = _silu(hq)
        k_c = onemlb_ref[...] - gap
        pre = _dot(glr_ref[rows, :].astype(BF16), wup_ref[...]) + bgate_ref[...]
        log_a = _log_sigmoid(pre) * (1.0 / GLA_TAU)
        q_d = gqk_ref[rows, 0:GLA_QK] * (GLA_DK ** -0.5)
        k_d = gqk_ref[rows, GLA_QK:2 * GLA_QK]

        lvl = lvl_ref[...]
        lane = lax.broadcasted_iota(jnp.int32, (c, LANES), 1)
        tril = tril_ref[...]

        heads = []
        for hd in range(LIN_HEADS):
            ls = slice(hd * HEAD_W, (hd + 1) * HEAD_W)
            heads.append((hd, q_c[:, ls], k_c[:, ls], hd,
                          h_ref[rows, 2 * HGRN_WIDTH + hd * HEAD_W:2 * HGRN_WIDTH + (hd + 1) * HEAD_W],
                          hnorm_ref[:, ls],
                          h_ref[rows, 3 * HGRN_WIDTH + hd * HEAD_W:3 * HGRN_WIDTH + (hd + 1) * HEAD_W], ls))
        for hd in range(LIN_HEADS):
            tile, half = divmod(hd, LANES // GLA_DK)
            ts = slice(tile * LANES, (tile + 1) * LANES)
            ls = slice(hd * HEAD_W, (hd + 1) * HEAD_W)
            mine = (lane >> int(math.log2(GLA_DK))) == half
            heads.append((LIN_HEADS + hd, jnp.where(mine, q_d[:, ts], 0.0), jnp.where(mine, k_d[:, ts], 0.0),
                          LIN_HEADS + tile, gv_ref[rows, ls], gnorm_ref[:, ls], gr_ref[rows, ls],
                          slice(HGRN_WIDTH + hd * HEAD_W, HGRN_WIDTH + (hd + 1) * HEAD_W)))

        gates = [log_f[:, t * LANES:(t + 1) * LANES] for t in range(LIN_HEADS)]
        gates += [log_a[:, t * LANES:(t + 1) * LANES] for t in range(GLA_QK // LANES)]
        factors = [_gate_factors(g, tril, sgn_ref, gcum_ref.at[t]) for t, g in enumerate(gates)]
        qkb = [(q.astype(BF16), k.astype(BF16)) for _, q, k, *_ in heads]
        attns = [None] * len(heads)
        for i in range(1 + len(_lin_levels(c))):
            mask = lvl == i
            for n, ((qb, kb), head) in enumerate(zip(qkb, heads)):
                if i == 0:
                    attns[n] = jnp.where(mask, _dot_nt(qb, kb), 0.0)
                else:
                    e = factors[head[3]][3][i - 1]
                    attns[n] = jnp.where(mask, _dot_nt(qb * e, kb * e), attns[n])
        attns = [a.astype(BF16) for a in attns]
        for (qb, kb), attn, (idx, _, _, tile, v, norm_w, gate, cols) in zip(qkb, attns, heads):
            e_cum, e_cum_b, e_end_b, _ = factors[tile]
            vb = v.astype(BF16)
            state_t = state_ref[idx]
            o = _dot(attn, vb) + _dot_nt(qb * e_cum_b, state_t.astype(BF16))
            state_ref[idx] = state_t * e_cum[c - 1:c, :] + _dot_tn(vb, kb * e_end_b)
            o_ref[rows, cols] = (_rms_norm(o, norm_w) * _silu(gate)).astype(o_ref.dtype)
        return carry

    lax.fori_loop(0, h_ref.shape[0] // c, chunk, 0)


def _lin_mixer(h_cd, batch, seq, lb, hgrn_norm_w, gla_w_gate_up, gla_b_gate, gla_norm_w):
    c = LIN_CHUNK
    step = c * MIX_STEP_CHUNKS
    nc = seq // step
    tok = lambda cb: (lambda b, i: (b * nc + i, cb))
    lb = lb.astype(F32)[None, :]
    wup = jnp.zeros((LANES, GLA_QK), F32).at[:GLA_RANK].set(gla_w_gate_up).astype(BF16)
    tril, lvl, sgn = _lin_tables(c)
    consts = [lb, 1.0 - lb, hgrn_norm_w[None, :], wup, gla_b_gate[None, :],
              gla_norm_w[None, :], tril, lvl, sgn]
    in_specs = [
        pl.BlockSpec((step, 4 * HGRN_WIDTH), tok(CD_H // (4 * HGRN_WIDTH))),
        pl.BlockSpec((step, 2 * GLA_QK), tok(CD_GQK // (2 * GLA_QK))),
        pl.BlockSpec((step, GLA_WIDTH), tok(CD_GV // GLA_WIDTH)),
        pl.BlockSpec((step, GLA_WIDTH), tok(CD_GR // GLA_WIDTH)),
        pl.BlockSpec((step, LANES), tok(CD_GLR // LANES)),
    ] + [_resident(a.shape) for a in consts]
    return pl.pallas_call(
        _lin_body,
        grid=(batch, nc),
        in_specs=in_specs,
        out_specs=pl.BlockSpec((step, HGRN_WIDTH + GLA_WIDTH), tok(0)),
        out_shape=jax.ShapeDtypeStruct((batch * seq, HGRN_WIDTH + GLA_WIDTH), BF16),
        scratch_shapes=[pltpu.VMEM((2 * LIN_HEADS, HEAD_W, HEAD_W), F32),
                        pltpu.VMEM((LIN_HEADS + GLA_QK // LANES, c, LANES), F32)],
        compiler_params=_params("parallel", "arbitrary"),
        name="lin_mixer",
    )(h_cd, h_cd, h_cd, h_cd, h_cd, *consts)


def _cols(w, start, stop, width=None):
    part = w[:, start:stop].astype(BF16)
    if width is not None and width > stop - start:
        part = jnp.pad(part, ((0, 0), (0, width - (stop - start))))
    return part


def _hgrn_lower_bound(lb_logits, layer):
    cum = jnp.cumsum(jax.nn.softmax(lb_logits.astype(F32), axis=0), axis=0)
    return cum[layer] - cum[0]


def kernel(x, p, ln_g, ln_b, ffn_w_gate, ffn_w_up, ffn_w_down, ple_w_gate, ple_w_proj, ab_w_in, ab_w_out,
           ssd_conv_w, ssd_conv_b, ssd_dt_bias, ssd_a_log, ssd_d, ssd_norm_w, s5_lambda_re, s5_lambda_im,
           s5_log_dt, s5_b_re, s5_b_im, s5_c_re, s5_c_im, s5_d, s5_w_glu, s5_b_glu, cd_w_in, cd_w_out,
           hgrn_lb_logits, hgrn_norm_w, gla_w_gate_up, gla_b_gate, gla_norm_w):
    batch, seq, _ = x.shape
    t = batch * seq
    x = x.reshape(t, D_MODEL)
    bf = lambda w: w.astype(BF16)
    ab_splits = (SSD_INNER, SSD_CONV_DIM, SSD_HEADS, S5_WIDTH)
    cd_splits = (HGRN_WIDTH, HGRN_WIDTH, HGRN_WIDTH, HGRN_WIDTH, GLA_QK, GLA_QK, GLA_WIDTH, GLA_RANK,
                 GLA_WIDTH)
    wg, wu, wd = bf(ffn_w_gate), bf(ffn_w_up), bf(ffn_w_down)
    pg, pp = bf(ple_w_gate), bf(ple_w_proj)
    p = p.reshape(DEPTH, t, PLE_DIM)
    ln_g = ln_g.reshape(DEPTH * 3, 1, D_MODEL)
    ln_b = ln_b.reshape(DEPTH * 3, 1, D_MODEL)
    w_ab_out, w_cd_out = bf(ab_w_out), bf(cd_w_out)
    w_glu, b_glu = bf(s5_w_glu), s5_b_glu[:, None, :]
    for i in range(DEPTH):
        j = i // 2
        x = _ffn_ln(x, wg, wu, wd, ln_g, ln_b, i, 0, 3 * i)
        if i % 2 == 0:
            w, e = ab_w_in[j], np.cumsum((0,) + ab_splits)
            h_ab = _proj_in(x, [(_cols(w, e[1], e[2]), AB_XBC), (_cols(w, e[0], e[1]), AB_Z),
                                (_cols(w, e[3], e[4]), AB_U), (_cols(w, e[2], e[3], AB_PACK - AB_DT), AB_DT)],
                            AB_PACK)
            y_a = _ssd_mixer(h_ab, batch, seq, ssd_conv_w[j], ssd_conv_b[j], ssd_dt_bias[j], ssd_a_log[j],
                             ssd_d[j], ssd_norm_w[j])
            tables = _s5_tables(s5_lambda_re[j], s5_lambda_im[j], s5_log_dt[j], s5_b_re[j], s5_b_im[j],
                                s5_c_re[j], s5_c_im[j], seq // S5_Q)
            y_s = _s5_core(h_ab, s5_d[j].reshape(1, S5_WIDTH).astype(F32), batch, seq, tables)
            mix = ("ab", (y_a, y_s), [(w_ab_out, (j,)), (w_glu, (j,)), (b_glu, (j,))])
        else:
            w, e = cd_w_in[j], np.cumsum((0,) + cd_splits)
            h_cd = _proj_in(x, [(_cols(w, e[0], e[7]), CD_H), (_cols(w, e[8], e[9]), CD_GR),
                                (_cols(w, e[7], e[8], CD_PACK - CD_GLR), CD_GLR)], CD_PACK)
            o_cd = _lin_mixer(h_cd, batch, seq, _hgrn_lower_bound(hgrn_lb_logits, i), hgrn_norm_w[j],
                              gla_w_gate_up[j], gla_b_gate[j], gla_norm_w[j])
            mix = ("cd", (o_cd,), [(w_cd_out, (j,))])
        x = _ffn_ln(x, wg, wu, wd, ln_g, ln_b, i, 1, 3 * i + 2, mix=mix, ple=(p, pg, pp))
    return x.reshape(batch, seq, D_MODEL)
```

```python
import functools
import math

import jax
import jax.numpy as jnp
import numpy as np
from jax import lax
from jax.experimental import pallas as pl
from jax.experimental.pallas import tpu as pltpu

F32 = jnp.float32
BF16 = jnp.bfloat16

D_MODEL = 1024
D_FF = 2816
PLE_DIM = 256
DEPTH = 2
DN_ALPHA = (2.0 * DEPTH) ** 0.25
LN_EPS = 1e-5
NEG_BIG = -1e30
SSD_HEADS = 16
SSD_HEAD_DIM = 64
SSD_GROUPS = 4
SSD_STATE = 128
SSD_CONV = 4
SSD_INNER = 1024
SSD_BC = SSD_GROUPS * SSD_STATE
SSD_CONV_DIM = SSD_INNER + 2 * SSD_BC
SSD_GROUP_WIDTH = SSD_INNER // SSD_GROUPS
SSD_HEADS_PER_GROUP = SSD_HEADS // SSD_GROUPS
S5_WIDTH = 1024
S5_GROUPS = 64
S5_GROUP = 16
S5_STATE = 64
LIN_HEADS = 4
HGRN_WIDTH = 512
GLA_DK = 64
GLA_QK = LIN_HEADS * GLA_DK
GLA_WIDTH = 512
GLA_RANK = 16
GLA_TAU = 16.0
HEAD_W = 128

LANES = 128
SUBLANES = 8
VMEM_LIMIT = 58 * 1024 * 1024

TM = 512
TM_FFN = 1024
SUB_FFN = 512
FF_CHUNK = 256
PROJ_COLS = 512
SSD_CHUNK = 128
MIX_STEP_CHUNKS = 4
LIN_CHUNK = 128
S5_Q = 32
S5_ROW = S5_Q * S5_GROUP
S5_TILE_GROUPS = LANES // S5_GROUP
S5_PITCH = 40

AB_XBC, AB_Z, AB_U, AB_DT, AB_PACK = 0, 2048, 3072, 4096, 4224
CD_H, CD_GQK, CD_GV, CD_GR, CD_GLR, CD_PACK = 0, 2048, 2560, 3072, 3584, 3712


def _resident(shape):
    n = len(shape)
    return pl.BlockSpec(shape, lambda *_: (0,) * n, pipeline_mode=pl.Buffered(1))


def _dot(a, b):
    return jnp.dot(a, b, preferred_element_type=F32)


def _dot_nt(a, b):
    return lax.dot_general(a, b, (((1,), (1,)), ((), ())), preferred_element_type=F32)


def _dot_tn(a, b):
    return lax.dot_general(a, b, (((0,), (0,)), ((), ())), preferred_element_type=F32)


def _split3(v):
    hi = v.astype(BF16)
    r = v - hi.astype(F32)
    mid = r.astype(BF16)
    lo = (r - mid.astype(F32)).astype(BF16)
    return hi, mid, lo


def _sel_dot(sel, v):
    hi, mid, lo = _split3(v)
    return _dot(sel, hi) + _dot(sel, mid) + _dot(sel, lo)


def _dot_sel(v, sel):
    hi, mid, lo = _split3(v)
    return _dot(hi, sel) + _dot(mid, sel) + _dot(lo, sel)


def _sigmoid(x):
    return 1.0 / (1.0 + jnp.exp(-x))


def _silu(x):
    h = 0.5 * x
    return h * jnp.tanh(h) + h


def _log_sigmoid(x):
    return jnp.minimum(x, 0.0) - jnp.log1p(jnp.exp(-jnp.abs(x)))


def _softplus(x):
    return jnp.maximum(x, 0.0) + jnp.log1p(jnp.exp(-jnp.abs(x)))


def _gelu_tanh(x):
    return 0.5 * x * (1.0 + jnp.tanh(math.sqrt(2.0 / math.pi) * (x + 0.044715 * (x * x * x))))


def _layer_norm(y, g, b):
    mu = jnp.mean(y, axis=-1, keepdims=True)
    yc = y - mu
    var = jnp.mean(yc * yc, axis=-1, keepdims=True)
    return yc * lax.rsqrt(var + LN_EPS) * g + b


def _rms_norm(y, w):
    return y * lax.rsqrt(jnp.mean(y * y, axis=-1, keepdims=True) + LN_EPS) * w


def _params(*sem):
    return pltpu.CompilerParams(dimension_semantics=sem, vmem_limit_bytes=VMEM_LIMIT)


def _ffn_body(mix, with_ple, x_ref, *refs):
    refs = list(refs)
    take = lambda n: [refs.pop(0) for _ in range(n)]
    if mix == "ab":
        ya_ref, ys_ref, wout_ref, wglu_ref, bglu_ref, g1_ref, b1_ref = take(7)
    elif mix == "cd":
        oc_ref, wout_ref, g1_ref, b1_ref = take(4)
    wg_ref, wu_ref, wd_ref, g_ref, b_ref = take(5)
    if with_ple:
        p_ref, pg_ref, pp_ref = take(3)
    o_ref, h_ref = refs
    for r in range(x_ref.shape[0] // SUB_FFN):
        rows = slice(r * SUB_FFN, (r + 1) * SUB_FFN)
        x = x_ref[rows, :]
        if mix == "ab":
            y = _gelu_tanh(ys_ref[rows, :].astype(F32))
            gate = _sigmoid(_dot(y.astype(BF16), wglu_ref[...]) + bglu_ref[...])
            m = _dot(ya_ref[rows, :], wout_ref[0:SSD_INNER, :]) + _dot((y * gate).astype(BF16), wout_ref[SSD_INNER:, :])
            x = _layer_norm(DN_ALPHA * x + m, g1_ref[...], b1_ref[...])
        elif mix == "cd":
            x = _layer_norm(DN_ALPHA * x + _dot(oc_ref[rows, :], wout_ref[...]), g1_ref[...], b1_ref[...])
        xb = x.astype(BF16)
        for c in range(D_FF // FF_CHUNK):
            sl = slice(c * FF_CHUNK, (c + 1) * FF_CHUNK)
            gate = _dot(xb, wg_ref[:, sl])
            up = _dot(xb, wu_ref[:, sl])
            h_ref[:, sl] = (_silu(gate) * up).astype(BF16)
        y = _layer_norm(DN_ALPHA * x + 0.5 * _dot(h_ref[...], wd_ref[...]), g_ref[...], b_ref[...])
        if with_ple:
            gate = _sigmoid(_dot(y.astype(BF16), pg_ref[...]))
            y = y + gate * _dot(p_ref[rows, :].astype(BF16), pp_ref[...])
        o_ref[rows, :] = y


def _pick(a, *idx):
    rest = a.shape[len(idx):]
    return pl.BlockSpec((None,) * len(idx) + rest, lambda *_: idx + (0,) * len(rest),
                        pipeline_mode=pl.Buffered(1))


def _ffn_ln(x, wg, wu, wd, ln_g, ln_b, layer, pos, ln_idx, mix=None, ple=None):
    t = x.shape[0]
    row = lambda i: (i, 0)
    in_specs = [pl.BlockSpec((TM_FFN, D_MODEL), row)]
    args = [x]
    if mix is not None:
        kind, acts, consts = mix
        in_specs += [pl.BlockSpec((TM_FFN, a.shape[1]), row) for a in acts]
        in_specs += [_pick(a, *idx) for a, idx in consts]
        in_specs += [_pick(ln_g, ln_idx - 1), _pick(ln_b, ln_idx - 1)]
        args += list(acts) + [a for a, _ in consts] + [ln_g, ln_b]
    in_specs += [_pick(wg, layer, pos), _pick(wu, layer, pos), _pick(wd, layer, pos),
                 _pick(ln_g, ln_idx), _pick(ln_b, ln_idx)]
    args += [wg, wu, wd, ln_g, ln_b]
    if ple is not None:
        p, pg, pp = ple
        in_specs += [pl.BlockSpec((None, TM_FFN, PLE_DIM), lambda i: (layer, i, 0)), _pick(pg, layer), _pick(pp, layer)]
        args += [p, pg, pp]
    return pl.pallas_call(
        functools.partial(_ffn_body, None if mix is None else mix[0], ple is not None),
        grid=(t // TM_FFN,),
        in_specs=in_specs,
        out_specs=pl.BlockSpec((TM_FFN, D_MODEL), row),
        out_shape=jax.ShapeDtypeStruct((t, D_MODEL), F32),
        scratch_shapes=[pltpu.VMEM((SUB_FFN, D_FF), BF16)],
        compiler_params=_params("parallel"),
        name="ffn_ln" if mix is None else "mix_ffn_ln_ple",
    )(*args)


def _proj_in_body(starts, x_ref, *refs):
    w_refs, o_ref = refs[:-1], refs[-1]
    xb = x_ref[...].astype(BF16)
    for w_ref, c0 in zip(w_refs, starts):
        n = w_ref.shape[1]
        for s0 in range(0, n, PROJ_COLS):
            s1 = min(s0 + PROJ_COLS, n)
            o_ref[:, c0 + s0:c0 + s1] = _dot(xb, w_ref[:, s0:s1])


def _proj_in(x, parts, n_out):
    t = x.shape[0]
    weights = [w for w, _ in parts]
    spans = sorted((c0, c0 + w.shape[1]) for w, c0 in parts)
    assert spans[0][0] == 0 and spans[-1][1] == n_out and all(a[1] == b[0] for a, b in zip(spans, spans[1:]))
    return pl.pallas_call(
        functools.partial(_proj_in_body, tuple(c0 for _, c0 in parts)),
        grid=(t // TM,),
        in_specs=[pl.BlockSpec((TM, D_MODEL), lambda i: (i, 0))] + [_resident(w.shape) for w in weights],
        out_specs=pl.BlockSpec((TM, n_out), lambda i: (i, 0)),
        out_shape=jax.ShapeDtypeStruct((t, n_out), F32),
        compiler_params=_params("parallel"),
        name="proj_in",
    )(x, *weights)


def _ssd_body(xbc_ref, z_ref, dt_ref, convw_ref, convb_ref, dtb_ref, alog_ref, dskip_ref,
              normw_ref, tril_ref, expand_ref, shift_ref, o_ref, xb_ref, state_ref):
    ch = SSD_CHUNK
    gw = SSD_GROUP_WIDTH

    @pl.when(pl.program_id(1) == 0)
    def _():
        xb_ref[0:ch, :] = jnp.zeros((ch, SSD_CONV_DIM), BF16)
        state_ref[...] = jnp.zeros(state_ref.shape, F32)

    def chunk(i, carry):
        rows = pl.ds(pl.multiple_of(i * ch, ch), ch)

        def conv_silu(cols):
            cur = xbc_ref[rows, cols]
            cur_b = cur.astype(BF16)
            xb_ref[ch:2 * ch, cols] = cur_b
            both = xb_ref[:, cols]
            acc = cur * convw_ref[SSD_CONV - 1:SSD_CONV, cols] + convb_ref[:, cols]
            for j in range(1, SSD_CONV):
                acc = acc + _dot(shift_ref[j - 1], both) * convw_ref[SSD_CONV - 1 - j:SSD_CONV - j, cols]
            xb_ref[0:ch, cols] = cur_b
            return _silu(acc)

        dt = _softplus(dt_ref[rows, :] + dtb_ref[...])
        da = dt * (-jnp.exp(alog_ref[...]))
        a_cum = _sel_dot(tril_ref[...], da) * math.log2(math.e)
        a_cum_t = a_cum.T
        dt_hi, dt_mid, _ = _split3(dt)
        dt_e = _dot(dt_hi, expand_ref[...]) + _dot(dt_mid, expand_ref[...])
        ac_e = _dot_sel(a_cum, expand_ref[...])
        ac_last = ac_e[ch - 1:ch, :]
        to_end = jnp.exp2(ac_last - ac_e)
        carry_scale = jnp.exp2(ac_e)
        chunk_decay = jnp.exp2(ac_last)

        row = lax.broadcasted_iota(jnp.int32, (ch, ch), 0)
        col = lax.broadcasted_iota(jnp.int32, (ch, ch), 1)
        causal = col <= row
        lane_head = lax.broadcasted_iota(jnp.int32, (ch, gw), 1) >> int(math.log2(SSD_HEAD_DIM))
        head_lanes = [lane_head == hh for hh in range(SSD_HEADS_PER_GROUP)]

        groups = range(SSD_GROUPS)
        cols = [slice(g * gw, (g + 1) * gw) for g in groups]
        xs = [conv_silu(cols[g]) for g in groups]
        bc = [conv_silu(slice(SSD_INNER + k * gw, SSD_INNER + (k + 1) * gw)).astype(BF16)
              for k in range(2 * SSD_BC // gw)]
        bc = [v[:, half * SSD_STATE:(half + 1) * SSD_STATE] for v in bc for half in range(gw // SSD_STATE)]
        bs, cs = bc[:SSD_GROUPS], bc[SSD_GROUPS:]
        xdt = [xs[g] * dt_e[:, cols[g]] for g in groups]
        xdt_b = [v.astype(BF16) for v in xdt]
        scores = [_dot_nt(cs[g], bs[g]).astype(BF16) for g in groups]
        ys = []
        for g in groups:
            y_g = jnp.zeros((ch, gw), F32)
            for hh in range(SSD_HEADS_PER_GROUP):
                h = g * SSD_HEADS_PER_GROUP + hh
                seg = a_cum[:, h:h + 1] - a_cum_t[h:h + 1, :]
                decay = jnp.exp2(jnp.where(causal, seg, NEG_BIG)).astype(BF16)
                full = _dot(scores[g] * decay, xdt_b[g])
                y_g = jnp.where(head_lanes[hh], full, y_g)
            ys.append(y_g)
        for g in groups:
            gs = cols[g]
            state = state_ref[g]
            y_g = ys[g] + _dot(cs[g], state.astype(BF16)) * carry_scale[:, gs]
            state_ref[g] = state * chunk_decay[:, gs] + _dot_tn(bs[g], (xdt[g] * to_end[:, gs]).astype(BF16))
            y_g = y_g + xs[g] * dskip_ref[:, gs]
            y_g = y_g * _silu(z_ref[rows, gs])
            o_ref[rows, gs] = _rms_norm(y_g, normw_ref[:, gs]).astype(o_ref.dtype)
        return carry

    lax.fori_loop(0, xbc_ref.shape[0] // ch, chunk, 0)


def _ssd_mixer(h_ab, batch, seq, conv_w, conv_b, dt_bias, a_log, d_skip, norm_w):
    ch = SSD_CHUNK
    step = ch * MIX_STEP_CHUNKS
    nc = seq // step
    tok = lambda cb: (lambda b, c: (b * nc + c, cb))
    pad_heads = lambda v: jnp.zeros((1, LANES), F32).at[0, :SSD_HEADS].set(v.astype(F32))
    tril = jnp.asarray(np.tril(np.ones((ch, ch), np.float32)), BF16)
    expand = np.zeros((LANES, SSD_INNER), np.float32)
    for h in range(SSD_HEADS):
        expand[h, h * SSD_HEAD_DIM:(h + 1) * SSD_HEAD_DIM] = 1.0
    expand = jnp.asarray(expand, BF16)
    shift = np.zeros((SSD_CONV - 1, ch, 2 * ch), np.float32)
    for j in range(1, SSD_CONV):
        shift[j - 1, np.arange(ch), ch + np.arange(ch) - j] = 1.0
    shift = jnp.asarray(shift, BF16)
    dskip_e = jnp.repeat(d_skip.astype(F32), SSD_HEAD_DIM)[None, :]
    consts = [conv_w, conv_b[None, :], pad_heads(dt_bias), pad_heads(a_log), dskip_e,
              norm_w[None, :], tril, expand, shift]
    in_specs = [
        pl.BlockSpec((step, SSD_CONV_DIM), tok(AB_XBC // SSD_CONV_DIM)),
        pl.BlockSpec((step, SSD_INNER), tok(AB_Z // SSD_INNER)),
        pl.BlockSpec((step, LANES), tok(AB_DT // LANES)),
    ] + [_resident(a.shape) for a in consts]
    return pl.pallas_call(
        _ssd_body,
        grid=(batch, nc),
        in_specs=in_specs,
        out_specs=pl.BlockSpec((step, SSD_INNER), tok(0)),
        out_shape=jax.ShapeDtypeStruct((batch * seq, SSD_INNER), BF16),
        scratch_shapes=[pltpu.VMEM((2 * ch, SSD_CONV_DIM), BF16),
                        pltpu.VMEM((SSD_GROUPS, SSD_STATE, SSD_GROUP_WIDTH), F32)],
        compiler_params=_params("parallel", "arbitrary"),
        name="ssd_mixer",
    )(h_ab, h_ab, h_ab, *consts)


def _s5_slot(g, s):
    return (s + g) % S5_TILE_GROUPS


def _s5_body(nblk, u_ref, d_ref, m_ref, wst_ref, wofft_ref, a1_ref, a2_ref, o_ref, uscr_ref, yscr_ref, ug_ref,
             yg_ref):
    ng = S5_TILE_GROUPS
    ncol = S5_Q // ng
    nstrip = nblk // SUBLANES
    slot = lax.broadcasted_iota(jnp.int32, (SUBLANES, LANES), 1) >> int(math.log2(S5_GROUP))
    blk = lax.broadcasted_iota(jnp.int32, (nblk, 2 * S5_STATE), 0)

    def pitch_in(c, carry):
        src = pl.multiple_of(c * S5_Q, S5_Q)
        dst = pl.multiple_of(c * S5_PITCH, SUBLANES)
        uscr_ref[pl.ds(dst, S5_Q), :] = u_ref[pl.ds(src, S5_Q), :]
        return carry
    lax.fori_loop(0, nblk, pitch_in, 0, unroll=8)

    def gather(i, carry):
        base = pl.multiple_of(i * (SUBLANES * S5_PITCH), SUBLANES)
        row = pl.multiple_of(i * SUBLANES, SUBLANES)
        for m in range(ncol):
            rolled = []
            for s in range(ng):
                us = uscr_ref[pl.ds(base + ng * m + s, SUBLANES, stride=S5_PITCH), :]
                rolled.append(pltpu.roll(us, S5_GROUP * s, axis=1) if s else us)
            for g in range(ng):
                v = rolled[0]
                for s in range(1, ng):
                    v = jnp.where(slot == _s5_slot(g, s), rolled[s], v)
                ug_ref[pl.ds(row, SUBLANES), (g * ncol + m) * LANES:(g * ncol + m + 1) * LANES] = v
        return carry
    lax.fori_loop(0, nstrip, gather, 0, unroll=2)

    us = [ug_ref[:, g * S5_ROW:(g + 1) * S5_ROW].astype(BF16) for g in range(ng)]
    xs = [_dot(us[g], wst_ref[g]) for g in range(ng)]
    for k in range(int(math.log2(nblk))):
        sh = 1 << k
        for g in range(ng):
            prev = jnp.where(blk >= sh, pltpu.roll(xs[g], sh, axis=0), 0.0)
            xs[g] = (xs[g] + prev * a1_ref[g, k:k + 1, :]
                     + pltpu.roll(prev, S5_STATE, axis=1) * a2_ref[g, k:k + 1, :])
    for g in range(ng):
        x_in = jnp.where(blk >= 1, pltpu.roll(xs[g], 1, axis=0), 0.0)
        yg_ref[:, g * S5_ROW:(g + 1) * S5_ROW] = (_dot(us[g], m_ref[g])
                                                   + _dot_nt(x_in.astype(BF16), wofft_ref[g]))

    def scatter(i, carry):
        base = pl.multiple_of(i * (SUBLANES * S5_PITCH), SUBLANES)
        row = pl.multiple_of(i * SUBLANES, SUBLANES)
        for m in range(ncol):
            cols = [yg_ref[pl.ds(row, SUBLANES), (g * ncol + m) * LANES:(g * ncol + m + 1) * LANES]
                    for g in range(ng)]
            for tt in range(ng):
                z = cols[0]
                for g in range(1, ng):
                    z = jnp.where(slot == _s5_slot(g, tt), cols[g], z)
                sh = (LANES - S5_GROUP * tt) % LANES
                yscr_ref[pl.ds(base + ng * m + tt, SUBLANES, stride=S5_PITCH), :] = (
                    pltpu.roll(z, sh, axis=1) if sh else z)
        return carry
    lax.fori_loop(0, nstrip, scatter, 0, unroll=2)

    def pitch_out(c, carry):
        src = pl.multiple_of(c * S5_PITCH, SUBLANES)
        dst = pl.multiple_of(c * S5_Q, S5_Q)
        y = yscr_ref[pl.ds(src, S5_Q), :] + d_ref[...] * u_ref[pl.ds(dst, S5_Q), :]
        o_ref[pl.ds(dst, S5_Q), :] = y.astype(o_ref.dtype)
        return carry
    lax.fori_loop(0, nblk, pitch_out, 0, unroll=8)


def _s5_core(h_ab, d_skip, batch, seq, tables):
    nblk = seq // S5_Q
    ng = S5_TILE_GROUPS
    grp = lambda a: pl.BlockSpec((ng,) + a.shape[1:], lambda j, b: (j, 0, 0))
    return pl.pallas_call(
        functools.partial(_s5_body, nblk),
        grid=(S5_GROUPS // ng, batch),
        in_specs=[pl.BlockSpec((seq, LANES), lambda j, b: (b, AB_U // LANES + j)),
                  pl.BlockSpec((1, LANES), lambda j, b: (0, j))]
        + [grp(a) for a in tables],
        out_specs=pl.BlockSpec((seq, LANES), lambda j, b: (b, j)),
        out_shape=jax.ShapeDtypeStruct((batch * seq, S5_WIDTH), BF16),
        scratch_shapes=[pltpu.VMEM((nblk * S5_PITCH, LANES), F32), pltpu.VMEM((nblk * S5_PITCH, LANES), F32),
                        pltpu.VMEM((nblk, ng * S5_ROW), F32), pltpu.VMEM((nblk, ng * S5_ROW), F32)],
        compiler_params=_params("parallel", "parallel"),
        name="s5_core",
    )(h_ab, d_skip, *tables)


def _s5_table_body(kt_ref, pw_rr_ref, pw_ii_ref, b1_ref, b2_ref, c1_ref, c2_ref, m_ref, wst_ref, wofft_ref):
    ng = S5_TILE_GROUPS
    ncol = S5_Q // ng
    lane = lax.broadcasted_iota(jnp.int32, (S5_GROUP, LANES), 1)
    for g in range(ng):
        strip = [jnp.zeros((S5_GROUP, LANES), F32)] * ncol + [kt_ref[g, :, c * LANES:(c + 1) * LANES]
                                                            for c in range(ncol)]
        b1, b2, c1, c2 = b1_ref[g], b2_ref[g], c1_ref[g], c2_ref[g]
        for s in range(S5_Q):
            start = ncol * LANES - S5_GROUP * s
            a, sh = start // LANES, start % LANES
            k = ng * (s // ng) + _s5_slot(g, s % ng)
            rows = slice(S5_GROUP * k, S5_GROUP * (k + 1))
            for c in range(ncol):
                if sh:
                    w = jnp.where(lane < LANES - sh, pltpu.roll(strip[a + c], LANES - sh, axis=1),
                                  pltpu.roll(strip[a + c + 1], LANES - sh, axis=1))
                else:
                    w = strip[a + c]
                if g:
                    w = pltpu.roll(w, S5_GROUP * g, axis=1)
                m_ref[g, rows, c * LANES:(c + 1) * LANES] = w.astype(BF16)
            e = S5_Q - 1 - s
            wst_ref[g, rows, :] = (pw_rr_ref[g, e:e + 1, :] * b1 + pw_ii_ref[g, e:e + 1, :] * b2).astype(BF16)
            wofft_ref[g, rows, :] = (pw_rr_ref[g, s + 1:s + 2, :] * c1
                                     + pw_ii_ref[g, s + 1:s + 2, :] * c2).astype(BF16)


def _s5_tables(lam_re, lam_im, log_dt, b_re, b_im, c_re, c_im, nblk):
    q, ng = S5_Q, S5_TILE_GROUPS
    hp = lax.Precision.HIGHEST
    lam = lax.complex(lam_re.astype(F32), lam_im.astype(F32))
    ldt = lam * jnp.exp(log_dt.astype(F32))[:, None]
    lam_bar = jnp.exp(ldt)
    b_bar = ((lam_bar - 1.0) / lam)[..., None] * lax.complex(b_re.astype(F32), b_im.astype(F32))
    c = lax.complex(c_re.astype(F32), c_im.astype(F32))
    tau = jnp.arange(q + 1, dtype=F32)
    pw = jnp.exp(ldt[:, None, :] * tau[None, :, None])
    kt = jnp.real(jnp.einsum('gpn,gtn,gnq->gqtp', c, pw[:, :q], b_bar, precision=hp)).reshape(
        S5_GROUPS, S5_GROUP, S5_ROW)
    halves = lambda lo, hi: jnp.concatenate([lo, hi], axis=-1)
    bt = jnp.transpose(b_bar, (0, 2, 1))
    small = [halves(jnp.real(pw), jnp.real(pw)), halves(jnp.imag(pw), jnp.imag(pw)),
             halves(jnp.real(bt), jnp.imag(bt)), halves(-jnp.imag(bt), jnp.real(bt)),
             halves(jnp.real(c), -jnp.imag(c)), halves(-jnp.imag(c), -jnp.real(c))]
    grp = lambda shape: pl.BlockSpec((ng,) + shape, lambda j: (j, 0, 0))
    shapes = [(S5_ROW, S5_ROW), (S5_ROW, 2 * S5_STATE), (S5_ROW, 2 * S5_STATE)]
    m, wst, wofft = pl.pallas_call(
        _s5_table_body,
        grid=(S5_GROUPS // ng,),
        in_specs=[grp(kt.shape[1:])] + [grp(a.shape[1:]) for a in small],
        out_specs=[grp(sh) for sh in shapes],
        out_shape=[jax.ShapeDtypeStruct((S5_GROUPS,) + sh, BF16) for sh in shapes],
        compiler_params=_params("parallel"),
        name="s5_tables",
    )(kt, *small)
    nlev = int(math.log2(nblk))
    step = jnp.exp(ldt[:, None, :] * (q * 2.0 ** jnp.arange(nlev, dtype=F32))[None, :, None])
    a1 = halves(jnp.real(step), jnp.real(step))
    a2 = halves(-jnp.imag(step), jnp.imag(step))
    return m, wst, wofft, a1, a2


def _lin_levels(c):
    return [c >> (i + 1) for i in range(int(math.log2(c)))]


def _lin_tables(c):
    t = np.arange(c)[:, None]
    j = np.arange(c)[None, :]
    lvl = np.full((c, c), -1, np.int32)
    lvl[np.arange(c), np.arange(c)] = 0
    sgn = []
    for i, b in enumerate(_lin_levels(c)):
        same = (t // (2 * b)) == (j // (2 * b))
        second = (t % (2 * b)) >= b
        lvl[same & second & ((j % (2 * b)) < b)] = i + 1
        sgn.append(np.broadcast_to(np.where(second, 1.0, -1.0) * math.log2(math.e), (c, LANES)))
    return (jnp.asarray((j <= t).astype(np.float32), BF16), jnp.asarray(lvl),
            jnp.asarray(np.stack(sgn), F32))


def _gate_factors(g, tril, sgn_ref, gcum_ref):
    c = LIN_CHUNK
    gcum = _sel_dot(tril, g)
    gcum_ref[...] = gcum
    pos = lax.broadcasted_iota(jnp.int32, (c, LANES), 0)
    factors = []
    for i, b in enumerate(_lin_levels(c)):
        if 2 * b >= SUBLANES:
            mids = [jnp.broadcast_to(gcum_ref[blk * 2 * b + b - 1:blk * 2 * b + b, :], (2 * b, LANES))
                    for blk in range(c // (2 * b))]
            gmid = mids[0] if len(mids) == 1 else jnp.concatenate(mids, axis=0)
        elif b == 2:
            p4 = pos & 3
            gmid = jnp.where(p4 == 0, pltpu.roll(gcum, c - 1, axis=0),
                             jnp.where(p4 == 1, gcum,
                                       jnp.where(p4 == 2, pltpu.roll(gcum, 1, axis=0),
                                                 pltpu.roll(gcum, 2, axis=0))))
        else:
            gmid = jnp.where((pos & 1) == 1, pltpu.roll(gcum, 1, axis=0), gcum)
        factors.append(jnp.exp2((gcum - gmid) * sgn_ref[i]).astype(BF16))
    e_cum = jnp.exp(gcum)
    return e_cum, e_cum.astype(BF16), jnp.exp(gcum[c - 1:c, :] - gcum).astype(BF16), factors


def _lin_body(h_ref, gqk_ref, gv_ref, gr_ref, glr_ref, lb_ref, onemlb_ref, hnorm_ref,
              wup_ref, bgate_ref, gnorm_ref, tril_ref, lvl_ref, sgn_ref, o_ref, state_ref, gcum_ref):
    c = LIN_CHUNK

    @pl.when(pl.program_id(1) == 0)
    def _():
        state_ref[...] = jnp.zeros(state_ref.shape, F32)

    def chunk(ci, carry):
        rows = pl.ds(pl.multiple_of(ci * c, c), c)

        hq = h_ref[rows, 0:HGRN_WIDTH]
        hf = h_ref[rows, HGRN_WIDTH:2 * HGRN_WIDTH]
        gap = onemlb_ref[...] * _sigmoid(hf)
        log_f = jnp.maximum(jnp.log(lb_ref[...] + gap), NEG_BIG)
        q_c = _silu(hq)
        k_c = onemlb_ref[...] - gap
        pre = _dot(glr_ref[rows, :].astype(BF16), wup_ref[...]) + bgate_ref[...]
        log_a = _log_sigmoid(pre) * (1.0 / GLA_TAU)
        q_d = gqk_ref[rows, 0:GLA_QK] * (GLA_DK ** -0.5)
        k_d = gqk_ref[rows, GLA_QK:2 * GLA_QK]

        lvl = lvl_ref[...]
        lane = lax.broadcasted_iota(jnp.int32, (c, LANES), 1)
        tril = tril_ref[...]

        heads = []
        for hd in range(LIN_HEADS):
            ls = slice(hd * HEAD_W, (hd + 1) * HEAD_W)
            heads.append((hd, q_c[:, ls], k_c[:, ls], hd,
                          h_ref[rows, 2 * HGRN_WIDTH + hd * HEAD_W:2 * HGRN_WIDTH + (hd + 1) * HEAD_W],
                          hnorm_ref[:, ls],
                          h_ref[rows, 3 * HGRN_WIDTH + hd * HEAD_W:3 * HGRN_WIDTH + (hd + 1) * HEAD_W], ls))
        for hd in range(LIN_HEADS):
            tile, half = divmod(hd, LANES // GLA_DK)
            ts = slice(tile * LANES, (tile + 1) * LANES)
            ls = slice(hd * HEAD_W, (hd + 1) * HEAD_W)
            mine = (lane >> int(math.log2(GLA_DK))) == half
            heads.append((LIN_HEADS + hd, jnp.where(mine, q_d[:, ts], 0.0), jnp.where(mine, k_d[:, ts], 0.0),
                          LIN_HEADS + tile, gv_ref[rows, ls], gnorm_ref[:, ls], gr_ref[rows, ls],
                          slice(HGRN_WIDTH + hd * HEAD_W, HGRN_WIDTH + (hd + 1) * HEAD_W)))

        gates = [log_f[:, t * LANES:(t + 1) * LANES] for t in range(LIN_HEADS)]
        gates += [log_a[:, t * LANES:(t + 1) * LANES] for t in range(GLA_QK // LANES)]
        factors = [_gate_factors(g, tril, sgn_ref, gcum_ref.at[t]) for t, g in enumerate(gates)]
        qkb = [(q.astype(BF16), k.astype(BF16)) for _, q, k, *_ in heads]
        attns = [None] * len(heads)
        for i in range(1 + len(_lin_levels(c))):
            mask = lvl == i
            for n, ((qb, kb), head) in enumerate(zip(qkb, heads)):
                if i == 0:
                    attns[n] = jnp.where(mask, _dot_nt(qb, kb), 0.0)
                else:
                    e = factors[head[3]][3][i - 1]
                    attns[n] = jnp.where(mask, _dot_nt(qb * e, kb * e), attns[n])
        attns = [a.astype(BF16) for a in attns]
        for (qb, kb), attn, (idx, _, _, tile, v, norm_w, gate, cols) in zip(qkb, attns, heads):
            e_cum, e_cum_b, e_end_b, _ = factors[tile]
            vb = v.astype(BF16)
            state_t = state_ref[idx]
            o = _dot(attn, vb) + _dot_nt(qb * e_cum_b, state_t.astype(BF16))
            state_ref[idx] = state_t * e_cum[c - 1:c, :] + _dot_tn(vb, kb * e_end_b)
            o_ref[rows, cols] = (_rms_norm(o, norm_w) * _silu(gate)).astype(o_ref.dtype)
        return carry

    lax.fori_loop(0, h_ref.shape[0] // c, chunk, 0)


def _lin_mixer(h_cd, batch, seq, lb, hgrn_norm_w, gla_w_gate_up, gla_b_gate, gla_norm_w):
    c = LIN_CHUNK
    step = c * MIX_STEP_CHUNKS
    nc = seq // step
    tok = lambda cb: (lambda b, i: (b * nc + i, cb))
    lb = lb.astype(F32)[None, :]
    wup = jnp.zeros((LANES, GLA_QK), F32).at[:GLA_RANK].set(gla_w_gate_up).astype(BF16)
    tril, lvl, sgn = _lin_tables(c)
    consts = [lb, 1.0 - lb, hgrn_norm_w[None, :], wup, gla_b_gate[None, :],
              gla_norm_w[None, :], tril, lvl, sgn]
    in_specs = [
        pl.BlockSpec((step, 4 * HGRN_WIDTH), tok(CD_H // (4 * HGRN_WIDTH))),
        pl.BlockSpec((step, 2 * GLA_QK), tok(CD_GQK // (2 * GLA_QK))),
        pl.BlockSpec((step, GLA_WIDTH), tok(CD_GV // GLA_WIDTH)),
        pl.BlockSpec((step, GLA_WIDTH), tok(CD_GR // GLA_WIDTH)),
        pl.BlockSpec((step, LANES), tok(CD_GLR // LANES)),
    ] + [_resident(a.shape) for a in consts]
    return pl.pallas_call(
        _lin_body,
        grid=(batch, nc),
        in_specs=in_specs,
        out_specs=pl.BlockSpec((step, HGRN_WIDTH + GLA_WIDTH), tok(0)),
        out_shape=jax.ShapeDtypeStruct((batch * seq, HGRN_WIDTH + GLA_WIDTH), BF16),
        scratch_shapes=[pltpu.VMEM((2 * LIN_HEADS, HEAD_W, HEAD_W), F32),
                        pltpu.VMEM((LIN_HEADS + GLA_QK // LANES, c, LANES), F32)],
        compiler_params=_params("parallel", "arbitrary"),
        name="lin_mixer",
    )(h_cd, h_cd, h_cd, h_cd, h_cd, *consts)


def _cols(w, start, stop, width=None):
    part = w[:, start:stop].astype(BF16)
    if width is not None and width > stop - start:
        part = jnp.pad(part, ((0, 0), (0, width - (stop - start))))
    return part


def _hgrn_lower_bound(lb_logits, layer):
    cum = jnp.cumsum(jax.nn.softmax(lb_logits.astype(F32), axis=0), axis=0)
    return cum[layer] - cum[0]


def kernel(x, p, ln_g, ln_b, ffn_w_gate, ffn_w_up, ffn_w_down, ple_w_gate, ple_w_proj, ab_w_in, ab_w_out,
           ssd_conv_w, ssd_conv_b, ssd_dt_bias, ssd_a_log, ssd_d, ssd_norm_w, s5_lambda_re, s5_lambda_im,
           s5_log_dt, s5_b_re, s5_b_im, s5_c_re, s5_c_im, s5_d, s5_w_glu, s5_b_glu, cd_w_in, cd_w_out,
           hgrn_lb_logits, hgrn_norm_w, gla_w_gate_up, gla_b_gate, gla_norm_w):
    batch, seq, _ = x.shape
    t = batch * seq
    x = x.reshape(t, D_MODEL)
    bf = lambda w: w.astype(BF16)
    ab_splits = (SSD_INNER, SSD_CONV_DIM, SSD_HEADS, S5_WIDTH)
    cd_splits = (HGRN_WIDTH, HGRN_WIDTH, HGRN_WIDTH, HGRN_WIDTH, GLA_QK, GLA_QK, GLA_WIDTH, GLA_RANK,
                 GLA_WIDTH)
    wg, wu, wd = bf(ffn_w_gate), bf(ffn_w_up), bf(ffn_w_down)
    pg, pp = bf(ple_w_gate), bf(ple_w_proj)
    p = p.reshape(DEPTH, t, PLE_DIM)
    ln_g = ln_g.reshape(DEPTH * 3, 1, D_MODEL)
    ln_b = ln_b.reshape(DEPTH * 3, 1, D_MODEL)
    w_ab_out, w_cd_out = bf(ab_w_out), bf(cd_w_out)
    w_glu, b_glu = bf(s5_w_glu), s5_b_glu[:, None, :]
    for i in range(DEPTH):
        j = i // 2
        x = _ffn_ln(x, wg, wu, wd, ln_g, ln_b, i, 0, 3 * i)
        if i % 2 == 0:
            w, e = ab_w_in[j], np.cumsum((0,) + ab_splits)
            h_ab = _proj_in(x, [(_cols(w, e[1], e[2]), AB_XBC), (_cols(w, e[0], e[1]), AB_Z),
                                (_cols(w, e[3], e[4]), AB_U), (_cols(w, e[2], e[3], AB_PACK - AB_DT), AB_DT)],
                            AB_PACK)
            y_a = _ssd_mixer(h_ab, batch, seq, ssd_conv_w[j], ssd_conv_b[j], ssd_dt_bias[j], ssd_a_log[j],
                             ssd_d[j], ssd_norm_w[j])
            tables = _s5_tables(s5_lambda_re[j], s5_lambda_im[j], s5_log_dt[j], s5_b_re[j], s5_b_im[j],
                                s5_c_re[j], s5_c_im[j], seq // S5_Q)
            y_s = _s5_core(h_ab, s5_d[j].reshape(1, S5_WIDTH).astype(F32), batch, seq, tables)
            mix = ("ab", (y_a, y_s), [(w_ab_out, (j,)), (w_glu, (j,)), (b_glu, (j,))])
        else:
            w, e = cd_w_in[j], np.cumsum((0,) + cd_splits)
            h_cd = _proj_in(x, [(_cols(w, e[0], e[7]), CD_H), (_cols(w, e[8], e[9]), CD_GR),
                                (_cols(w, e[7], e[8], CD_PACK - CD_GLR), CD_GLR)], CD_PACK)
            o_cd = _lin_mixer(h_cd, batch, seq, _hgrn_lower_bound(hgrn_lb_logits, i), hgrn_norm_w[j],
                              gla_w_gate_up[j], gla_b_gate[j], gla_norm_w[j])
            mix = ("cd", (o_cd,), [(w_cd_out, (j,))])
        x = _ffn_ln(x, wg, wu, wd, ln_g, ln_b, i, 1, 3 * i + 2, mix=mix, ple=(p, pg, pp))
    return x.reshape(batch, seq, D_MODEL)
```
